```python
import jax, jax.numpy as jnp
from jax import lax
import numpy as np

D_MODEL = 2048
BATCH = 4
SEQ = 4096
DEPTH = 1

N_DSA_HEADS = 8
DSA_HEAD_DIM = 128
DSA_KV_HEADS = 2
IDX_HEADS = 16
IDX_DIM = 64
TOPK_MAX = 256
MLA_HEADS = 8
MLA_NOPE = 128
MLA_ROPE = 64
MLA_V = 128
KV_RANK = 512
ROPE_THETA = 10000.0
Q_BLOCK = 128
N_GROUPS = 8
EXPERTS_PER_GROUP = 8
N_EXPERTS = N_GROUPS * EXPERTS_PER_GROUP
TOP_K_EXPERTS = 2
EXPERT_FF = 1408
MOE_BLOCK = 128
LN_EPS = 1e-5
RMS_EPS = 1e-6
DEEPNORM_ALPHA = (2.0 * DEPTH) ** 0.25
DEEPNORM_BETA = (8.0 * DEPTH) ** -0.25

DSA_Q_DIM = N_DSA_HEADS * DSA_HEAD_DIM
DSA_KV_DIM = DSA_KV_HEADS * DSA_HEAD_DIM
IDX_Q_DIM = IDX_HEADS * IDX_DIM
MLA_QK_DIM = MLA_NOPE + MLA_ROPE
MLA_Q_DIM = MLA_HEADS * MLA_QK_DIM
MLA_OUT_DIM = MLA_HEADS * MLA_V
IN_SPLITS = (DSA_Q_DIM, DSA_KV_DIM, DSA_KV_DIM, IDX_Q_DIM, IDX_DIM, IDX_HEADS,
             MLA_Q_DIM, KV_RANK, MLA_ROPE, 2 * D_MODEL)
IN_DIM = sum(IN_SPLITS)

kernel_name = "hybrid_dsa_mla_gated_hmoe_deepnorm"


def layer_norm(x, g, b):
    xf = x.astype(jnp.float32)
    mu = jnp.mean(xf, -1, keepdims=True)
    var = jnp.mean(jnp.square(xf - mu), -1, keepdims=True)
    return ((xf - mu) * lax.rsqrt(var + LN_EPS) * g.astype(jnp.float32) + b.astype(jnp.float32)).astype(x.dtype)


def rms_norm(x, g):
    xf = x.astype(jnp.float32)
    return (xf * lax.rsqrt(jnp.mean(jnp.square(xf), -1, keepdims=True) + RMS_EPS) * g.astype(jnp.float32)).astype(x.dtype)


def apply_rope(x, pos):
    half = x.shape[-1] // 2
    freqs = ROPE_THETA ** (-jnp.arange(half, dtype=jnp.float32) / half)
    ang = pos.astype(jnp.float32)[:, None] * freqs
    shape = (1, x.shape[1]) + (1,) * (x.ndim - 3) + (half,)
    cos = jnp.cos(ang).reshape(shape)
    sin = jnp.sin(ang).reshape(shape)
    xf = x.astype(jnp.float32)
    x1, x2 = xf[..., :half], xf[..., half:]
    return jnp.concatenate([x1 * cos - x2 * sin, x1 * sin + x2 * cos], -1).astype(x.dtype)


def alibi_slopes(n_heads):
    return jnp.exp2(-8.0 * (jnp.arange(n_heads, dtype=jnp.float32) + 1.0) / n_heads)


def to_blocks(a):
    b, s = a.shape[:2]
    return jnp.moveaxis(a.reshape((b, s // Q_BLOCK, Q_BLOCK) + a.shape[2:]), 1, 0)


def from_blocks(o):
    o = jnp.moveaxis(o, 0, 1)
    return o.reshape((o.shape[0], o.shape[1] * o.shape[2]) + o.shape[3:])


def dsa_attention(q, k, v, q_idx, k_idx, w_idx):
    b, s = q.shape[:2]
    n_sel = min(TOPK_MAX, s // 4)
    rep = N_DSA_HEADS // DSA_KV_HEADS
    slopes = alibi_slopes(N_DSA_HEADS).reshape(DSA_KV_HEADS, rep)
    key_pos = jnp.arange(s)

    def block(args):
        qb, qib, wb, t0 = args
        qpos = t0 + jnp.arange(Q_BLOCK)
        logits = jnp.einsum('bqhd,bsd->bqhs', qib, k_idx, preferred_element_type=jnp.float32) * IDX_DIM ** -0.5
        score = jnp.einsum('bqhs,bqh->bqs', jax.nn.relu(logits), wb.astype(jnp.float32)) * IDX_HEADS ** -0.5
        causal = key_pos[None, :] <= qpos[:, None]
        score = jnp.where(causal[None], score, -jnp.inf)
        _, sel = lax.top_k(score, n_sel)
        k_sel = jax.vmap(lambda kk, ii: kk[ii])(k, sel)
        v_sel = jax.vmap(lambda vv, ii: vv[ii])(v, sel)
        qg = qb.reshape(b, Q_BLOCK, DSA_KV_HEADS, rep, DSA_HEAD_DIM)
        sc = jnp.einsum('bqgrd,bqkgd->bqgrk', qg, k_sel, preferred_element_type=jnp.float32) * DSA_HEAD_DIM ** -0.5
        dist = (qpos[None, :, None] - sel).astype(jnp.float32)
        sc = sc - slopes[None, None, :, :, None] * dist[:, :, None, None, :]
        valid = sel <= qpos[None, :, None]
        sc = jnp.where(valid[:, :, None, None, :], sc, -jnp.inf)
        p = jax.nn.softmax(sc, axis=-1).astype(v.dtype)
        o = jnp.einsum('bqgrk,bqkgd->bqgrd', p, v_sel)
        return o.reshape(b, Q_BLOCK, DSA_Q_DIM)

    starts = jnp.arange(s // Q_BLOCK) * Q_BLOCK
    out = lax.map(block, (to_blocks(q), to_blocks(q_idx), to_blocks(w_idx), starts))
    return from_blocks(out)


def mla_attention(q, c_kv, k_rope, w_ukv, g_ckv):
    b, s = q.shape[:2]
    pos = jnp.arange(s)
    q_nope = q[..., :MLA_NOPE]
    q_pe = apply_rope(q[..., MLA_NOPE:], pos)
    k_pe = apply_rope(k_rope, pos)
    kv = jnp.einsum('bsr,rhe->bshe', rms_norm(c_kv, g_ckv), w_ukv)
    k_nope, v = kv[..., :MLA_NOPE], kv[..., MLA_NOPE:]
    scale = MLA_QK_DIM ** -0.5

    def block(args):
        qn, qp, t0 = args
        qpos = t0 + jnp.arange(Q_BLOCK)
        sc = (jnp.einsum('bqhd,bshd->bhqs', qn, k_nope, preferred_element_type=jnp.float32)
              + jnp.einsum('bqhd,bsd->bhqs', qp, k_pe, preferred_element_type=jnp.float32)) * scale
        causal = pos[None, :] <= qpos[:, None]
        sc = jnp.where(causal[None, None], sc, -jnp.inf)
        p = jax.nn.softmax(sc, axis=-1).astype(v.dtype)
        o = jnp.einsum('bhqs,bshd->bqhd', p, v)
        return o.reshape(b, Q_BLOCK, MLA_OUT_DIM)

    starts = jnp.arange(s // Q_BLOCK) * Q_BLOCK
    out = lax.map(block, (to_blocks(q_nope), to_blocks(q_pe), starts))
    return from_blocks(out)


def hierarchical_moe(h, w_rg, b_rg, w_re, b_re, w_gate_e, w_up_e, w_down_e):
    t, d = h.shape
    g_prob = jax.nn.softmax((h @ w_rg).astype(jnp.float32) + b_rg.astype(jnp.float32), axis=-1)
    g_w, g_idx = lax.top_k(g_prob, 1)
    e_logits = ((h @ w_re).astype(jnp.float32) + b_re.astype(jnp.float32)).reshape(t, N_GROUPS, EXPERTS_PER_GROUP)
    e_in = jnp.take_along_axis(e_logits, g_idx[:, :, None], axis=1)[:, 0]
    e_w, e_local = lax.top_k(jax.nn.softmax(e_in, axis=-1), TOP_K_EXPERTS)
    e_w = e_w / jnp.sum(e_w, -1, keepdims=True)
    weights = g_w * e_w
    expert = g_idx * EXPERTS_PER_GROUP + e_local

    n_assign = t * TOP_K_EXPERTS
    flat_e = expert.reshape(-1)
    flat_tok = jnp.repeat(jnp.arange(t), TOP_K_EXPERTS)
    flat_w = weights.reshape(-1)
    order = jnp.argsort(flat_e)
    se, stok, sw = flat_e[order], flat_tok[order], flat_w[order]
    counts = jnp.bincount(flat_e, length=N_EXPERTS)
    start = jnp.cumsum(counts) - counts
    padded = (counts + MOE_BLOCK - 1) // MOE_BLOCK * MOE_BLOCK
    pend = jnp.cumsum(padded)
    pstart = pend - padded
    dest = pstart[se] + (jnp.arange(n_assign) - start[se])
    cap = -(-n_assign // MOE_BLOCK) * MOE_BLOCK + N_EXPERTS * MOE_BLOCK
    n_blk = cap // MOE_BLOCK
    buf_tok = jnp.zeros((cap,), jnp.int32).at[dest].set(stok)
    blk_expert = jnp.minimum(jnp.searchsorted(pend, jnp.arange(n_blk) * MOE_BLOCK, side='right'), N_EXPERTS - 1)
    xb = h[buf_tok].reshape(n_blk, MOE_BLOCK, d)

    def run(args):
        xe, e = args
        return (jax.nn.silu(xe @ w_gate_e[e]) * (xe @ w_up_e[e])) @ w_down_e[e]

    yb = lax.map(run, (xb, blk_expert)).reshape(cap, d)
    y = yb[dest] * sw[:, None].astype(h.dtype)
    return jax.ops.segment_sum(y, stok, num_segments=t)


def hybrid_layer(x, w_in, w_ukv, g_ckv, w_pa, w_pb, w_o, ln1_g, ln1_b,
                 w_rg, b_rg, w_re, b_re, w_gate_e, w_up_e, w_down_e, ln2_g, ln2_b):
    b, s, d = x.shape
    z = x @ w_in
    offs = [int(o) for o in np.cumsum(IN_SPLITS)[:-1]]
    dq, dk, dv, iq, ik, iw, mq, mckv, mkr, gates = jnp.split(z, offs, axis=-1)
    o_a = dsa_attention(dq.reshape(b, s, N_DSA_HEADS, DSA_HEAD_DIM),
                        dk.reshape(b, s, DSA_KV_HEADS, DSA_HEAD_DIM),
                        dv.reshape(b, s, DSA_KV_HEADS, DSA_HEAD_DIM),
                        iq.reshape(b, s, IDX_HEADS, IDX_DIM), ik, iw)
    o_b = mla_attention(mq.reshape(b, s, MLA_HEADS, MLA_QK_DIM), mckv, mkr, w_ukv, g_ckv)
    g_a, g_b = jnp.split(gates, 2, axis=-1)
    merged = jax.nn.sigmoid(g_a) * (o_a @ w_pa) + jax.nn.sigmoid(g_b) * (o_b @ w_pb)
    h = layer_norm(DEEPNORM_ALPHA * x + merged @ w_o, ln1_g, ln1_b)
    f = hierarchical_moe(h.reshape(b * s, d), w_rg, b_rg, w_re, b_re,
                         w_gate_e, w_up_e, w_down_e).reshape(b, s, d)
    return layer_norm(DEEPNORM_ALPHA * h + f, ln2_g, ln2_b)


def setup_inputs(seed: int = 0) -> dict:
    key = jax.random.key(seed)
    ks = jax.random.split(key, 20)

    def nrm(k, shape, scale):
        return jax.random.normal(k, shape, jnp.float32) * scale

    v0 = DSA_Q_DIM + DSA_KV_DIM
    in_scale = jnp.ones((IN_DIM,), jnp.float32).at[v0:v0 + DSA_KV_DIM].set(DEEPNORM_BETA)
    ukv_scale = jnp.concatenate([jnp.ones((MLA_NOPE,), jnp.float32),
                                 jnp.full((MLA_V,), DEEPNORM_BETA, jnp.float32)])
    return {
        "x": nrm(ks[0], (BATCH, SEQ, D_MODEL), 1.0),
        "w_in": nrm(ks[1], (DEPTH, D_MODEL, IN_DIM), D_MODEL ** -0.5) * in_scale,
        "w_ukv": nrm(ks[2], (DEPTH, KV_RANK, MLA_HEADS, MLA_NOPE + MLA_V), KV_RANK ** -0.5) * ukv_scale,
        "g_ckv": 1.0 + nrm(ks[3], (DEPTH, KV_RANK), 0.01),
        "w_pa": nrm(ks[4], (DEPTH, DSA_Q_DIM, D_MODEL), DSA_Q_DIM ** -0.5 * DEEPNORM_BETA),
        "w_pb": nrm(ks[5], (DEPTH, MLA_OUT_DIM, D_MODEL), MLA_OUT_DIM ** -0.5 * DEEPNORM_BETA),
        "w_o": nrm(ks[6], (DEPTH, D_MODEL, D_MODEL), D_MODEL ** -0.5 * DEEPNORM_BETA),
        "ln1_g": 1.0 + nrm(ks[7], (DEPTH, D_MODEL), 0.01),
        "ln1_b": nrm(ks[8], (DEPTH, D_MODEL), 0.01),
        "w_rg": nrm(ks[9], (DEPTH, D_MODEL, N_GROUPS), D_MODEL ** -0.5),
        "b_rg": nrm(ks[10], (DEPTH, N_GROUPS), 0.01),
        "w_re": nrm(ks[11], (DEPTH, D_MODEL, N_EXPERTS), D_MODEL ** -0.5),
        "b_re": nrm(ks[12], (DEPTH, N_EXPERTS), 0.01),
        "w_gate_e": nrm(ks[13], (DEPTH, N_EXPERTS, D_MODEL, EXPERT_FF), D_MODEL ** -0.5),
        "w_up_e": nrm(ks[14], (DEPTH, N_EXPERTS, D_MODEL, EXPERT_FF), D_MODEL ** -0.5),
        "w_down_e": nrm(ks[15], (DEPTH, N_EXPERTS, EXPERT_FF, D_MODEL), EXPERT_FF ** -0.5 * DEEPNORM_BETA),
        "ln2_g": 1.0 + nrm(ks[16], (DEPTH, D_MODEL), 0.01),
        "ln2_b": nrm(ks[17], (DEPTH, D_MODEL), 0.01),
    }


def reference(x, w_in, w_ukv, g_ckv, w_pa, w_pb, w_o, ln1_g, ln1_b, w_rg, b_rg, w_re, b_re,
              w_gate_e, w_up_e, w_down_e, ln2_g, ln2_b):
    for l in range(DEPTH):
        x = hybrid_layer(x, w_in[l], w_ukv[l], g_ckv[l], w_pa[l], w_pb[l], w_o[l], ln1_g[l], ln1_b[l],
                         w_rg[l], b_rg[l], w_re[l], b_re[l], w_gate_e[l], w_up_e[l], w_down_e[l],
                         ln2_g[l], ln2_b[l])
    return x
```

```python
import functools

import numpy as np
import jax
import jax.numpy as jnp
from jax import lax
from jax.experimental import pallas as pl
from jax.experimental.pallas import tpu as pltpu

N_DSA_HEADS = 8
DSA_HEAD_DIM = 128
DSA_KV_HEADS = 2
DSA_REP = N_DSA_HEADS // DSA_KV_HEADS
IDX_HEADS = 16
IDX_DIM = 64
TOPK_MAX = 256
MLA_HEADS = 8
MLA_NOPE = 128
MLA_ROPE = 64
MLA_V = 128
KV_RANK = 512
ROPE_THETA = 10000.0
N_GROUPS = 8
EXPERTS_PER_GROUP = 8
N_EXPERTS = N_GROUPS * EXPERTS_PER_GROUP
LN_EPS = 1e-5
RMS_EPS = 1e-6

DSA_Q_DIM = N_DSA_HEADS * DSA_HEAD_DIM
DSA_KV_DIM = DSA_KV_HEADS * DSA_HEAD_DIM
IDX_Q_DIM = IDX_HEADS * IDX_DIM
MLA_QK_DIM = MLA_NOPE + MLA_ROPE
MLA_Q_DIM = MLA_HEADS * MLA_QK_DIM
MLA_OUT_DIM = MLA_HEADS * MLA_V

LANES = 128
VMEM_LIMIT_MB = 56

DSA_TQ = 128
DSA_KC = 512
MLA_T = 512
MOE_BLK = 256
NEG = -1e30
INT_MIN = -2 ** 31

F32 = jnp.float32
BF16 = jnp.bfloat16


def _cparams(n_axes, vmem_mb=VMEM_LIMIT_MB):
    return pltpu.CompilerParams(dimension_semantics=("arbitrary",) * n_axes,
                                vmem_limit_bytes=vmem_mb * 1024 * 1024)


def _pick(n, prefs):
    for p in prefs:
        if n % p == 0:
            return p
    return n


def _mm_kernel(a_ref, w_ref, o_ref):
    o_ref[...] = jnp.dot(a_ref[...], w_ref[...], preferred_element_type=F32).astype(o_ref.dtype)


def _matmul(a, w, out_dtype, tn):
    m, k = a.shape
    n = w.shape[1]
    tm = _pick(m, (512, 256, 128))
    return pl.pallas_call(
        _mm_kernel,
        grid=(n // tn, m // tm),
        in_specs=[pl.BlockSpec((tm, k), lambda j, i: (i, 0)),
                  pl.BlockSpec((k, tn), lambda j, i: (0, j))],
        out_specs=pl.BlockSpec((tm, tn), lambda j, i: (i, j)),
        out_shape=jax.ShapeDtypeStruct((m, n), out_dtype),
        compiler_params=_cparams(2),
    )(a, w)


def _dsa_kernel(q_ref, iq_ref, w_ref, kidx_ref, k_ref, v_ref, o_ref,
                keys_ref, thr_ref, jcut_ref, m_ref, l_ref, acc_ref, *, tq, kc, n_sel, seq):
    i = pl.program_id(1)
    nch = (i * tq + tq + kc - 1) // kc
    row = lax.broadcasted_iota(jnp.int32, (tq, kc), 0)
    lane = lax.broadcasted_iota(jnp.int32, (tq, kc), 1)
    qpos = i * tq + row

    iq = iq_ref[0].reshape(IDX_HEADS * tq, IDX_DIM)
    w = w_ref[0] * (IDX_DIM ** -0.5 * IDX_HEADS ** -0.5)

    def score_body(c, carry):
        off = pl.multiple_of(c * kc, kc)
        kx = kidx_ref[0, pl.ds(off, kc), :]
        logits = lax.dot_general(iq, kx, (((1,), (1,)), ((), ())), preferred_element_type=F32)
        sc = jnp.zeros((tq, kc), F32)
        for h in range(IDX_HEADS):
            sc = sc + jnp.maximum(logits[h * tq:(h + 1) * tq], 0.0) * w[:, h:h + 1]
        bits = pltpu.bitcast(sc, jnp.int32)
        key = jnp.where(bits < 0, bits ^ jnp.int32(0x7FFFFFFF), bits)
        key = jnp.where(off + lane <= qpos, key, jnp.int32(INT_MIN))
        keys_ref[c] = key
        return carry

    lax.fori_loop(0, nch, score_body, 0)

    thr_ref[...] = jnp.full((tq, 1), INT_MIN + 1, jnp.int32)
    jcut_ref[...] = jnp.full((tq, 1), seq, jnp.int32)

    def count(pred_fn):
        def body(c, acc):
            kch = keys_ref[c]
            for j in range(kc // LANES):
                acc = acc + pred_fn(kch[:, j * LANES:(j + 1) * LANES], c * kc + j * LANES)
            return acc
        acc = lax.fori_loop(0, nch, body, jnp.zeros((tq, LANES), jnp.int32))
        return jnp.sum(acc.astype(F32), axis=1, keepdims=True).astype(jnp.int32)

    lane1 = lax.broadcasted_iota(jnp.int32, (tq, LANES), 1)

    @pl.when(i * tq + tq > n_sel)
    def _():
        def bit_body(b, t):
            cand = t + lax.shift_left(jnp.int32(1), 31 - b)
            cand_b = jnp.broadcast_to(cand, (tq, LANES))
            cnt = count(lambda kch, _: jnp.where(kch >= cand_b, 1, 0))
            return jnp.where(cnt >= n_sel, cand, t)

        t = lax.fori_loop(0, 32, bit_body, jnp.full((tq, 1), INT_MIN, jnp.int32))
        t = jnp.maximum(t, INT_MIN + 1)
        thr_ref[...] = t
        t_b = jnp.broadcast_to(t, (tq, LANES))
        c_ge = count(lambda kch, _: jnp.where(kch >= t_b, 1, 0))
        c_gt = count(lambda kch, _: jnp.where(kch > t_b, 1, 0))
        excess = c_ge > n_sel
        need = n_sel - c_gt

        @pl.when(jnp.max(jnp.where(excess, 1.0, 0.0)) > 0.0)
        def _():
            nbits = max(1, int(np.ceil(np.log2(seq))))

            def jbit(b, jc):
                cand = jc + lax.shift_left(jnp.int32(1), nbits - 1 - b)
                cand_b = jnp.broadcast_to(cand, (tq, LANES))
                cnt = count(lambda kch, base: jnp.where(
                    kch == t_b, jnp.where(base + lane1 < cand_b, 1, 0), 0))
                return jnp.where(cnt < need, cand, jc)

            jc = lax.fori_loop(0, nbits, jbit, jnp.zeros((tq, 1), jnp.int32))
            jcut_ref[...] = jnp.where(excess, jc, seq)

    rows4 = DSA_REP * tq
    qg = [jnp.concatenate([q_ref[:, (g * DSA_REP + r) * DSA_HEAD_DIM:(g * DSA_REP + r + 1) * DSA_HEAD_DIM]
                           for r in range(DSA_REP)], axis=0) for g in range(DSA_KV_HEADS)]
    rr = lax.broadcasted_iota(jnp.int32, (rows4, 1), 0)
    slopes = []
    for g in range(DSA_KV_HEADS):
        sl = jnp.zeros((rows4, 1), F32)
        for r in range(DSA_REP):
            val = 2.0 ** (-8.0 * (g * DSA_REP + r + 1) / N_DSA_HEADS)
            sl = jnp.where(rr >= r * tq, val, sl)
        slopes.append(sl)
    m_ref[...] = jnp.full(m_ref.shape, NEG, F32)
    l_ref[...] = jnp.zeros(l_ref.shape, F32)
    acc_ref[...] = jnp.zeros(acc_ref.shape, F32)
    t = thr_ref[...]
    jc = jcut_ref[...]
    scale = DSA_HEAD_DIM ** -0.5

    def att_body(c, carry):
        off = pl.multiple_of(c * kc, kc)
        key = keys_ref[c]
        kpos = off + lane
        sel = jnp.where(key >= jnp.where(kpos <= jc, t, t + 1), 1, 0)
        dist = (qpos - kpos).astype(F32)
        sel4 = jnp.concatenate([sel] * DSA_REP, axis=0) > 0
        dist4 = jnp.concatenate([dist] * DSA_REP, axis=0)
        for g in range(DSA_KV_HEADS):
            kg = k_ref[pl.ds(off, kc), g * DSA_HEAD_DIM:(g + 1) * DSA_HEAD_DIM]
            vg = v_ref[pl.ds(off, kc), g * DSA_HEAD_DIM:(g + 1) * DSA_HEAD_DIM]
            slope = slopes[g]
            s = lax.dot_general(qg[g], kg, (((1,), (1,)), ((), ())), preferred_element_type=F32)
            s = jnp.where(sel4, s * scale - slope * dist4, NEG)
            m_old = m_ref[g]
            m_new = jnp.maximum(m_old, jnp.max(s, axis=1, keepdims=True))
            p = jnp.exp(s - m_new)
            alpha = jnp.exp(m_old - m_new)
            l_ref[g] = alpha * l_ref[g] + jnp.sum(p, axis=1, keepdims=True)
            acc_ref[g] = alpha * acc_ref[g] + jnp.dot(p.astype(BF16), vg, preferred_element_type=F32)
            m_ref[g] = m_new
        return carry

    lax.fori_loop(0, nch, att_body, 0)
    for g in range(DSA_KV_HEADS):
        o = acc_ref[g] / l_ref[g]
        for r in range(DSA_REP):
            hh = g * DSA_REP + r
            o_ref[:, hh * DSA_HEAD_DIM:(hh + 1) * DSA_HEAD_DIM] = o[r * tq:(r + 1) * tq].astype(o_ref.dtype)


def _dsa_attention(zb, iq_hm, w_idx, kidx, batch, seq, off_k, off_v):
    tq = min(DSA_TQ, seq)
    kc = min(DSA_KC, seq)
    nq = seq // tq
    n_sel = min(TOPK_MAX, seq // 4)
    kern = functools.partial(_dsa_kernel, tq=tq, kc=kc, n_sel=n_sel, seq=seq)
    return pl.pallas_call(
        kern,
        grid=(batch, nq),
        in_specs=[
            pl.BlockSpec((tq, DSA_Q_DIM), lambda b, i: (b * nq + i, 0)),
            pl.BlockSpec((1, IDX_HEADS, tq, IDX_DIM), lambda b, i: (b, 0, i, 0)),
            pl.BlockSpec((1, tq, IDX_HEADS), lambda b, i: (b, i, 0)),
            pl.BlockSpec((1, seq, IDX_DIM), lambda b, i: (b, 0, 0)),
            pl.BlockSpec((seq, DSA_KV_DIM), lambda b, i: (b, off_k // DSA_KV_DIM)),
            pl.BlockSpec((seq, DSA_KV_DIM), lambda b, i: (b, off_v // DSA_KV_DIM)),
        ],
        out_specs=pl.BlockSpec((tq, DSA_Q_DIM), lambda b, i: (b * nq + i, 0)),
        out_shape=jax.ShapeDtypeStruct((batch * seq, DSA_Q_DIM), BF16),
        scratch_shapes=[
            pltpu.VMEM((seq // kc, tq, kc), jnp.int32),
            pltpu.VMEM((tq, 1), jnp.int32),
            pltpu.VMEM((tq, 1), jnp.int32),
            pltpu.VMEM((DSA_KV_HEADS, DSA_REP * tq, 1), F32),
            pltpu.VMEM((DSA_KV_HEADS, DSA_REP * tq, 1), F32),
            pltpu.VMEM((DSA_KV_HEADS, DSA_REP * tq, DSA_HEAD_DIM), F32),
        ],
        compiler_params=_cparams(2),
    )(zb, iq_hm, w_idx, kidx, zb, zb)


def _mla_q_kernel(qn_ref, qr_ref, qs_ref, cos_ref, sin_ref, o_ref):
    pe = (qr_ref[...] * cos_ref[...] + qs_ref[...] * sin_ref[...]).astype(o_ref.dtype)
    for h in range(MLA_HEADS):
        o_ref[0, h, :, 0:MLA_NOPE] = qn_ref[:, h * MLA_NOPE:(h + 1) * MLA_NOPE]
        o_ref[0, h, :, MLA_NOPE:MLA_QK_DIM] = pe[:, h * MLA_ROPE:(h + 1) * MLA_ROPE]


def _mla_q_prep(zb, zs, cos_q, sin_q, batch, seq, off_qn, off_qr, off_qs):
    tm = _pick(seq, (512, 256, 128))
    ns = seq // tm
    wr = MLA_HEADS * MLA_ROPE
    return pl.pallas_call(
        _mla_q_kernel,
        grid=(batch * ns,),
        in_specs=[
            pl.BlockSpec((tm, MLA_HEADS * MLA_NOPE), lambda i: (i, off_qn // (MLA_HEADS * MLA_NOPE))),
            pl.BlockSpec((tm, wr), lambda i: (i, off_qr // wr)),
            pl.BlockSpec((tm, wr), lambda i: (i, off_qs // wr)),
            pl.BlockSpec((tm, wr), lambda i: (i % ns, 0)),
            pl.BlockSpec((tm, wr), lambda i: (i % ns, 0)),
        ],
        out_specs=pl.BlockSpec((1, MLA_HEADS, tm, MLA_QK_DIM), lambda i: (i // ns, 0, i % ns, 0)),
        out_shape=jax.ShapeDtypeStruct((batch, MLA_HEADS, seq, MLA_QK_DIM), BF16),
        compiler_params=_cparams(1),
    )(zb, zs, zs, cos_q, sin_q)


def _mla_kv_kernel(c_ref, kr_ref, g_ref, w_ref, cos_ref, sin_ref, k_ref, v_ref):
    c = c_ref[...]
    cn = c * lax.rsqrt(jnp.mean(c * c, axis=-1, keepdims=True) + RMS_EPS) * g_ref[...]
    kv = jnp.dot(cn.astype(BF16), w_ref[...], preferred_element_type=F32)
    kr = kr_ref[...]
    pe = (kr[:, 0:MLA_ROPE] * cos_ref[...] + kr[:, MLA_ROPE:2 * MLA_ROPE] * sin_ref[...]).astype(k_ref.dtype)
    for h in range(MLA_HEADS):
        k_ref[0, h, :, 0:MLA_NOPE] = kv[:, h * MLA_NOPE:(h + 1) * MLA_NOPE].astype(k_ref.dtype)
        k_ref[0, h, :, MLA_NOPE:MLA_QK_DIM] = pe
        v0 = MLA_HEADS * MLA_NOPE + h * MLA_V
        v_ref[0, h] = kv[:, v0:v0 + MLA_V].astype(v_ref.dtype)


def _mla_kv_prep(zs, g_ckv, w_kv, cos_k, sin_k, batch, seq, off_c, off_kr):
    tm = _pick(seq, (512, 256, 128))
    ns = seq // tm
    nkv = MLA_HEADS * (MLA_NOPE + MLA_V)
    return pl.pallas_call(
        _mla_kv_kernel,
        grid=(batch * ns,),
        in_specs=[
            pl.BlockSpec((tm, KV_RANK), lambda i: (i, off_c // KV_RANK)),
            pl.BlockSpec((tm, 2 * MLA_ROPE), lambda i: (i, off_kr // (2 * MLA_ROPE))),
            pl.BlockSpec((1, KV_RANK), lambda i: (0, 0)),
            pl.BlockSpec((KV_RANK, nkv), lambda i: (0, 0)),
            pl.BlockSpec((tm, MLA_ROPE), lambda i: (i % ns, 0)),
            pl.BlockSpec((tm, MLA_ROPE), lambda i: (i % ns, 0)),
        ],
        out_specs=[
            pl.BlockSpec((1, MLA_HEADS, tm, MLA_QK_DIM), lambda i: (i // ns, 0, i % ns, 0)),
            pl.BlockSpec((1, MLA_HEADS, tm, MLA_V), lambda i: (i // ns, 0, i % ns, 0)),
        ],
        out_shape=[jax.ShapeDtypeStruct((batch, MLA_HEADS, seq, MLA_QK_DIM), BF16),
                   jax.ShapeDtypeStruct((batch, MLA_HEADS, seq, MLA_V), BF16)],
        compiler_params=_cparams(1),
    )(zs, zs, g_ckv, w_kv, cos_k, sin_k)


def _mla_flash_kernel(q_ref, k_ref, v_ref, o_ref, m_ref, l_ref, acc_ref, *, t):
    qi = pl.program_id(2)
    ki = pl.program_id(3)
    scale = MLA_QK_DIM ** -0.5

    @pl.when(ki == 0)
    def _():
        m_ref[...] = jnp.full(m_ref.shape, NEG, F32)
        l_ref[...] = jnp.zeros(l_ref.shape, F32)
        acc_ref[...] = jnp.zeros(acc_ref.shape, F32)

    def step(diagonal):
        s = lax.dot_general(q_ref[0, 0], k_ref[0, 0], (((1,), (1,)), ((), ())),
                            preferred_element_type=F32)
        if diagonal:
            row = lax.broadcasted_iota(jnp.int32, (t, t), 0)
            col = lax.broadcasted_iota(jnp.int32, (t, t), 1)
            s = jnp.where(col <= row, s, NEG)
        m_old = m_ref[...]
        m_new = jnp.maximum(m_old, jnp.max(s, axis=1, keepdims=True))
        p = jnp.exp((s - m_new) * scale)
        alpha = jnp.exp((m_old - m_new) * scale)
        l_ref[...] = alpha * l_ref[...] + jnp.sum(p, axis=1, keepdims=True)
        acc_ref[...] = alpha * acc_ref[...] + jnp.dot(p.astype(BF16), v_ref[0, 0],
                                                      preferred_element_type=F32)
        m_ref[...] = m_new

    @pl.when(ki < qi)
    def _():
        step(False)

    @pl.when(ki == qi)
    def _():
        step(True)
        o_ref[0] = (acc_ref[...] / l_ref[...]).astype(o_ref.dtype)


def _mla_flash(q_cat, k_cat, v, batch, seq):
    t = min(MLA_T, seq)
    n = seq // t
    kern = functools.partial(_mla_flash_kernel, t=t)
    return pl.pallas_call(
        kern,
        grid=(batch, MLA_HEADS, n, n),
        in_specs=[
            pl.BlockSpec((1, 1, t, MLA_QK_DIM), lambda b, h, qi, ki: (b, h, qi, 0)),
            pl.BlockSpec((1, 1, t, MLA_QK_DIM), lambda b, h, qi, ki: (b, h, jnp.minimum(ki, qi), 0)),
            pl.BlockSpec((1, 1, t, MLA_V), lambda b, h, qi, ki: (b, h, jnp.minimum(ki, qi), 0)),
        ],
        out_specs=pl.BlockSpec((1, t, MLA_V), lambda b, h, qi, ki: (b, qi, h)),
        out_shape=jax.ShapeDtypeStruct((batch, seq, MLA_OUT_DIM), BF16),
        scratch_shapes=[pltpu.VMEM((t, 1), F32), pltpu.VMEM((t, 1), F32), pltpu.VMEM((t, MLA_V), F32)],
        compiler_params=_cparams(4),
    )(q_cat, k_cat, v)


def _merge_kernel(oa_ref, ob_ref, wa_ref, wb_ref, ga_ref, gb_ref, o_ref):
    pa = jnp.dot(oa_ref[...], wa_ref[...], preferred_element_type=F32)
    pb = jnp.dot(ob_ref[...], wb_ref[...], preferred_element_type=F32)
    o_ref[...] = (jax.nn.sigmoid(ga_ref[...]) * pa + jax.nn.sigmoid(gb_ref[...]) * pb).astype(o_ref.dtype)


def _gated_merge(o_a, o_b, w_pa, w_pb, gates):
    m = o_a.shape[0]
    d = w_pa.shape[1]
    tm = _pick(m, (512, 256, 128))
    tn = _pick(d, (1024, 512, 256, 128))
    nn = d // tn
    return pl.pallas_call(
        _merge_kernel,
        grid=(nn, m // tm),
        in_specs=[
            pl.BlockSpec((tm, o_a.shape[1]), lambda j, i: (i, 0)),
            pl.BlockSpec((tm, o_b.shape[1]), lambda j, i: (i, 0)),
            pl.BlockSpec((w_pa.shape[0], tn), lambda j, i: (0, j)),
            pl.BlockSpec((w_pb.shape[0], tn), lambda j, i: (0, j)),
            pl.BlockSpec((tm, tn), lambda j, i: (i, j)),
            pl.BlockSpec((tm, tn), lambda j, i: (i, j + nn)),
        ],
        out_specs=pl.BlockSpec((tm, tn), lambda j, i: (i, j)),
        out_shape=jax.ShapeDtypeStruct((m, d), BF16),
        compiler_params=_cparams(2),
    )(o_a, o_b, w_pa, w_pb, gates, gates)


def _layer_norm(y, g, b):
    mu = jnp.mean(y, axis=-1, keepdims=True)
    dlt = y - mu
    var = jnp.mean(dlt * dlt, axis=-1, keepdims=True)
    return dlt * lax.rsqrt(var + LN_EPS) * g + b


def _proj_ln_kernel(a_ref, w_ref, x_ref, g_ref, b_ref, o_ref, *, alpha):
    y = alpha * x_ref[...] + jnp.dot(a_ref[...], w_ref[...], preferred_element_type=F32)
    o_ref[...] = _layer_norm(y, g_ref[...], b_ref[...])


def _proj_ln(a, w, x, g, b, alpha):
    m, d = x.shape
    tm = _pick(m, (256, 128))
    return pl.pallas_call(
        functools.partial(_proj_ln_kernel, alpha=alpha),
        grid=(m // tm,),
        in_specs=[
            pl.BlockSpec((tm, a.shape[1]), lambda i: (i, 0)),
            pl.BlockSpec(w.shape, lambda i: (0, 0)),
            pl.BlockSpec((tm, d), lambda i: (i, 0)),
            pl.BlockSpec((1, d), lambda i: (0, 0)),
            pl.BlockSpec((1, d), lambda i: (0, 0)),
        ],
        out_specs=pl.BlockSpec((tm, d), lambda i: (i, 0)),
        out_shape=jax.ShapeDtypeStruct((m, d), F32),
        compiler_params=_cparams(1),
    )(a, w, x, g, b)


def _router_kernel(h_ref, whi_ref, wlo_ref, b_ref, ids_ref, wts_ref):
    h = h_ref[...]
    h_hi = h.astype(BF16)
    h_lo = (h - h_hi.astype(F32)).astype(BF16)
    logits = (jnp.dot(h_hi, whi_ref[...], preferred_element_type=F32)
              + jnp.dot(h_hi, wlo_ref[...], preferred_element_type=F32)
              + jnp.dot(h_lo, whi_ref[...], preferred_element_type=F32)) + b_ref[...]
    tm = h.shape[0]
    lane = lax.broadcasted_iota(jnp.int32, (tm, LANES), 1)
    lane_f = lane.astype(F32)
    big = float(4 * LANES)
    ninf = -jnp.inf
    gl = jnp.where(lane < N_GROUPS, logits, ninf)
    ge = jnp.exp(gl - jnp.max(gl, axis=1, keepdims=True))
    gprob = ge / jnp.sum(ge, axis=1, keepdims=True)
    gw = jnp.max(gprob, axis=1, keepdims=True)
    gidx = jnp.min(jnp.where(gprob == gw, lane_f, big), axis=1, keepdims=True).astype(jnp.int32)
    in_group = lax.shift_right_arithmetic(lane - N_GROUPS, 3) == gidx
    el = jnp.where(in_group, logits, ninf)
    ee = jnp.exp(el - jnp.max(el, axis=1, keepdims=True))
    ep = jnp.where(in_group, ee / jnp.sum(ee, axis=1, keepdims=True), -1.0)
    p1 = jnp.max(ep, axis=1, keepdims=True)
    i1 = jnp.min(jnp.where(ep == p1, lane_f, big), axis=1, keepdims=True).astype(jnp.int32)
    ep2 = jnp.where(lane == i1, -1.0, ep)
    p2 = jnp.max(ep2, axis=1, keepdims=True)
    i2 = jnp.min(jnp.where(ep2 == p2, lane_f, big), axis=1, keepdims=True).astype(jnp.int32)
    den = p1 + p2
    w1 = gw * (p1 / den)
    w2 = gw * (p2 / den)
    ids_ref[...] = jnp.where(lane == 0, i1 - N_GROUPS, jnp.where(lane == 1, i2 - N_GROUPS, 0))
    wts_ref[...] = jnp.where(lane == 0, w1, jnp.where(lane == 1, w2, 0.0))


def _router(h, w_hi, w_lo, b_r):
    m, d = h.shape
    tm = _pick(m, (512, 256, 128))
    return pl.pallas_call(
        _router_kernel,
        grid=(m // tm,),
        in_specs=[
            pl.BlockSpec((tm, d), lambda i: (i, 0)),
            pl.BlockSpec((d, LANES), lambda i: (0, 0)),
            pl.BlockSpec((d, LANES), lambda i: (0, 0)),
            pl.BlockSpec((1, LANES), lambda i: (0, 0)),
        ],
        out_specs=[pl.BlockSpec((tm, LANES), lambda i: (i, 0)),
                   pl.BlockSpec((tm, LANES), lambda i: (i, 0))],
        out_shape=[jax.ShapeDtypeStruct((m, LANES), jnp.int32),
                   jax.ShapeDtypeStruct((m, LANES), F32)],
        compiler_params=_cparams(1),
    )(h, w_hi, w_lo, b_r)


def _row_copy(src_hbm, src_row, dst_vmem, dst_row, sem):
    return pltpu.make_async_copy(src_hbm.at[pl.ds(src_row, 1)], dst_vmem.at[pl.ds(dst_row, 1)], sem)


def _gather_kernel(tok_ref, h_hbm, o_ref, buf, sem, *, rows):
    base = pl.program_id(0) * rows

    def start(r, c):
        _row_copy(h_hbm, tok_ref[base + r], buf, r, sem).start()
        return c

    def wait(r, c):
        _row_copy(h_hbm, 0, buf, r, sem).wait()
        return c

    lax.fori_loop(0, rows, start, 0)
    lax.fori_loop(0, rows, wait, 0)
    o_ref[...] = buf[...].astype(o_ref.dtype)


def _gather_rows(buf_tok, h):
    cap = buf_tok.shape[0]
    d = h.shape[1]
    rows = MOE_BLK
    return pl.pallas_call(
        functools.partial(_gather_kernel, rows=rows),
        grid_spec=pltpu.PrefetchScalarGridSpec(
            num_scalar_prefetch=1,
            grid=(cap // rows,),
            in_specs=[pl.BlockSpec(memory_space=pl.ANY)],
            out_specs=pl.BlockSpec((rows, d), lambda i, tok: (i, 0)),
            scratch_shapes=[pltpu.VMEM((rows, d), F32), pltpu.SemaphoreType.DMA(())],
        ),
        out_shape=jax.ShapeDtypeStruct((cap, d), BF16),
        compiler_params=_cparams(1),
    )(buf_tok, h)


def _expert_kernel(be_ref, nu_ref, x_ref, wg_ref, wu_ref, wd_ref, o_ref):
    i = pl.program_id(0)

    @pl.when(i < nu_ref[0])
    def _():
        x = x_ref[...]
        g = jnp.dot(x, wg_ref[...], preferred_element_type=F32)
        u = jnp.dot(x, wu_ref[...], preferred_element_type=F32)
        a = (g * jax.nn.sigmoid(g) * u).astype(BF16)
        o_ref[...] = jnp.dot(a, wd_ref[...], preferred_element_type=F32)

    @pl.when(i >= nu_ref[0])
    def _():
        o_ref[...] = jnp.zeros(o_ref.shape, o_ref.dtype)


def _experts(blk_expert, n_used, xb, wg, wu, wd):
    cap, d = xb.shape
    ff = wg.shape[2]
    return pl.pallas_call(
        _expert_kernel,
        grid_spec=pltpu.PrefetchScalarGridSpec(
            num_scalar_prefetch=2,
            grid=(cap // MOE_BLK,),
            in_specs=[
                pl.BlockSpec((MOE_BLK, d), lambda i, be, nu: (i, 0)),
                pl.BlockSpec((None, d, ff), lambda i, be, nu: (be[i], 0, 0)),
                pl.BlockSpec((None, d, ff), lambda i, be, nu: (be[i], 0, 0)),
                pl.BlockSpec((None, ff, d), lambda i, be, nu: (be[i], 0, 0)),
            ],
            out_specs=pl.BlockSpec((MOE_BLK, d), lambda i, be, nu: (i, 0)),
        ),
        out_shape=jax.ShapeDtypeStruct((cap, d), F32),
        compiler_params=_cparams(1),
    )(blk_expert, n_used, xb, wg, wu, wd)


def _combine_kernel(dest_ref, y_hbm, h_ref, wts_ref, g_ref, b_ref, o_ref, buf0, buf1, sem, *, tm, alpha):
    base = pl.program_id(0) * tm

    def start(r, c):
        a = 2 * (base + r)
        _row_copy(y_hbm, dest_ref[a], buf0, r, sem).start()
        _row_copy(y_hbm, dest_ref[a + 1], buf1, r, sem).start()
        return c

    def wait(r, c):
        _row_copy(y_hbm, 0, buf0, r, sem).wait()
        _row_copy(y_hbm, 0, buf1, r, sem).wait()
        return c

    lax.fori_loop(0, tm, start, 0)
    lax.fori_loop(0, tm, wait, 0)
    wts = wts_ref[...]
    f = buf0[...] * wts[:, 0:1] + buf1[...] * wts[:, 1:2]
    o_ref[...] = _layer_norm(alpha * h_ref[...] + f, g_ref[...], b_ref[...])


def _combine_ln(dest, yb, h, wts, g, b, alpha):
    m, d = h.shape
    tm = _pick(m, (256, 128))
    return pl.pallas_call(
        functools.partial(_combine_kernel, tm=tm, alpha=alpha),
        grid_spec=pltpu.PrefetchScalarGridSpec(
            num_scalar_prefetch=1,
            grid=(m // tm,),
            in_specs=[
                pl.BlockSpec(memory_space=pl.ANY),
                pl.BlockSpec((tm, d), lambda i, dst: (i, 0)),
                pl.BlockSpec((tm, LANES), lambda i, dst: (i, 0)),
                pl.BlockSpec((1, d), lambda i, dst: (0, 0)),
                pl.BlockSpec((1, d), lambda i, dst: (0, 0)),
            ],
            out_specs=pl.BlockSpec((tm, d), lambda i, dst: (i, 0)),
            scratch_shapes=[pltpu.VMEM((tm, d), F32), pltpu.VMEM((tm, d), F32),
                            pltpu.SemaphoreType.DMA(())],
        ),
        out_shape=jax.ShapeDtypeStruct((m, d), F32),
        compiler_params=_cparams(1),
    )(dest, yb, h, wts, g, b)


def _dispatch_tables(ids, n_tok):
    flat_e = ids[:, :2].reshape(-1)
    n_assign = flat_e.shape[0]
    onehot = (flat_e[:, None] == jnp.arange(N_EXPERTS, dtype=jnp.int32)[None, :]).astype(jnp.int32)
    csum = jnp.cumsum(onehot, axis=0)
    rank = jnp.sum(onehot * csum, axis=1) - 1
    counts = csum[-1]
    padded = (counts + MOE_BLK - 1) // MOE_BLK * MOE_BLK
    pend = jnp.cumsum(padded)
    pstart = pend - padded
    dest = (pstart[flat_e] + rank).astype(jnp.int32)
    cap = -(-n_assign // MOE_BLK) * MOE_BLK + N_EXPERTS * MOE_BLK
    n_blk = cap // MOE_BLK
    buf_tok = jnp.zeros((cap,), jnp.int32).at[dest].set(jnp.arange(n_assign, dtype=jnp.int32) // 2)
    blk_expert = jnp.minimum(
        jnp.searchsorted(pend, jnp.arange(n_blk, dtype=jnp.int32) * MOE_BLK, side='right'),
        N_EXPERTS - 1).astype(jnp.int32)
    n_used = (pend[-1] // MOE_BLK).astype(jnp.int32).reshape(1)
    return dest, buf_tok, blk_expert, n_used


def _rope_tables(seq):
    half = MLA_ROPE // 2
    freqs = ROPE_THETA ** (-jnp.arange(half, dtype=F32) / half)
    ang = jnp.arange(seq, dtype=F32)[:, None] * freqs
    cos = jnp.cos(ang)
    sin = jnp.sin(ang)
    return jnp.concatenate([cos, cos], -1), jnp.concatenate([sin, sin], -1)


def _swap_halves_neg(w):
    half = w.shape[-1] // 2
    return jnp.concatenate([-w[..., half:], w[..., :half]], axis=-1)


def _layer(x, w_in, w_ukv, g_ckv, w_pa, w_pb, w_o, ln1_g, ln1_b, w_rg, b_rg, w_re, b_re,
           w_gate_e, w_up_e, w_down_e, ln2_g, ln2_b, alpha):
    batch, seq, d = x.shape
    n_tok = batch * seq
    x2 = x.reshape(n_tok, d)
    xb16 = x2.astype(BF16)

    splits = (DSA_Q_DIM, DSA_KV_DIM, DSA_KV_DIM, IDX_Q_DIM, IDX_DIM, IDX_HEADS, MLA_Q_DIM, KV_RANK, MLA_ROPE, 2 * d)
    offs = np.cumsum((0,) + splits)
    w_dq, w_dk, w_dv, w_iq, w_ik, w_iw, w_mq, w_ckv, w_kr, w_gates = [
        w_in[:, offs[j]:offs[j + 1]] for j in range(len(splits))]
    w_mq3 = w_mq.reshape(d, MLA_HEADS, MLA_QK_DIM)
    w_qn = w_mq3[:, :, :MLA_NOPE].reshape(d, MLA_HEADS * MLA_NOPE)
    w_qr = w_mq3[:, :, MLA_NOPE:]
    w_qs = _swap_halves_neg(w_qr)
    wr = MLA_HEADS * MLA_ROPE
    wb = jnp.concatenate([w_dq, w_iq, w_qn, w_dk, w_dv], axis=1).astype(BF16)
    off_iq, off_qn = DSA_Q_DIM, DSA_Q_DIM + IDX_Q_DIM
    off_k = off_qn + MLA_HEADS * MLA_NOPE
    off_v = off_k + DSA_KV_DIM
    pad_i = jnp.zeros((d, LANES - IDX_DIM - IDX_HEADS), F32)
    ws = jnp.concatenate([w_qr.reshape(d, wr), w_qs.reshape(d, wr), w_ckv, w_ik, w_iw, pad_i,
                          w_kr, _swap_halves_neg(w_kr)], axis=1).astype(BF16)
    off_qr, off_qs, off_c = 0, wr, 2 * wr
    off_i = off_c + KV_RANK
    off_kr = off_i + LANES

    zb = _matmul(xb16, wb, BF16, tn=_pick(wb.shape[1], (1792, 512, 256, 128)))
    zs = _matmul(xb16, ws, F32, tn=ws.shape[1])
    gates = _matmul(xb16, w_gates.astype(BF16), F32, tn=_pick(2 * d, (2048, 1024, 512, 256)))

    iq_hm = zb[:, off_iq:off_iq + IDX_Q_DIM].reshape(batch, seq, IDX_HEADS, IDX_DIM).transpose(0, 2, 1, 3)
    kidx = zs[:, off_i:off_i + IDX_DIM].astype(BF16).reshape(batch, seq, IDX_DIM)
    w_idx = zs[:, off_i + IDX_DIM:off_i + IDX_DIM + IDX_HEADS].reshape(batch, seq, IDX_HEADS)
    o_a = _dsa_attention(zb, iq_hm, w_idx, kidx, batch, seq, off_k, off_v)

    cos2, sin2 = _rope_tables(seq)
    w_ukv3 = w_ukv.reshape(KV_RANK, MLA_HEADS, MLA_NOPE + MLA_V)
    w_kv = jnp.concatenate([w_ukv3[:, :, :MLA_NOPE].reshape(KV_RANK, MLA_HEADS * MLA_NOPE),
                            w_ukv3[:, :, MLA_NOPE:].reshape(KV_RANK, MLA_HEADS * MLA_V)], axis=1).astype(BF16)
    q_cat = _mla_q_prep(zb, zs, jnp.tile(cos2, (1, MLA_HEADS)), jnp.tile(sin2, (1, MLA_HEADS)),
                        batch, seq, off_qn, off_qr, off_qs)
    k_cat, v_mla = _mla_kv_prep(zs, g_ckv.reshape(1, KV_RANK), w_kv, cos2, sin2, batch, seq, off_c, off_kr)
    o_b = _mla_flash(q_cat, k_cat, v_mla, batch, seq).reshape(n_tok, MLA_OUT_DIM)

    merged = _gated_merge(o_a, o_b, w_pa.astype(BF16), w_pb.astype(BF16), gates)
    h = _proj_ln(merged, w_o.astype(BF16), x2, ln1_g.reshape(1, d), ln1_b.reshape(1, d), alpha)

    pad_r = jnp.zeros((d, LANES - N_GROUPS - N_EXPERTS), F32)
    w_r = jnp.concatenate([w_rg, w_re, pad_r], axis=1)
    w_r_hi = w_r.astype(BF16)
    w_r_lo = (w_r - w_r_hi.astype(F32)).astype(BF16)
    b_r = jnp.concatenate([b_rg, b_re, jnp.zeros((LANES - N_GROUPS - N_EXPERTS,), F32)]).reshape(1, LANES)
    ids, wts = _router(h, w_r_hi, w_r_lo, b_r)
    dest, buf_tok, blk_expert, n_used = _dispatch_tables(ids, n_tok)
    xb = _gather_rows(buf_tok, h)
    yb = _experts(blk_expert, n_used, xb, w_gate_e.astype(BF16), w_up_e.astype(BF16), w_down_e.astype(BF16))
    out = _combine_ln(dest, yb, h, wts, ln2_g.reshape(1, d), ln2_b.reshape(1, d), alpha)
    return out.reshape(batch, seq, d)


def kernel(x, w_in, w_ukv, g_ckv, w_pa, w_pb, w_o, ln1_g, ln1_b, w_rg, b_rg, w_re, b_re,
           w_gate_e, w_up_e, w_down_e, ln2_g, ln2_b):
    depth = w_in.shape[0]
    alpha = (2.0 * depth) ** 0.25
    for l in range(depth):
        x = _layer(x, w_in[l], w_ukv[l], g_ckv[l], w_pa[l], w_pb[l], w_o[l], ln1_g[l], ln1_b[l],
                   w_rg[l], b_rg[l], w_re[l], b_re[l], w_gate_e[l], w_up_e[l], w_down_e[l],
                   ln2_g[l], ln2_b[l], alpha)
    return x
```

```python
import functools

import numpy as np
import jax
import jax.numpy as jnp
from jax import lax
from jax.experimental import pallas as pl
from jax.experimental.pallas import tpu as pltpu

N_DSA_HEADS = 8
DSA_HEAD_DIM = 128
DSA_KV_HEADS = 2
DSA_REP = N_DSA_HEADS // DSA_KV_HEADS
IDX_HEADS = 16
IDX_DIM = 64
TOPK_MAX = 256
MLA_HEADS = 8
MLA_NOPE = 128
MLA_ROPE = 64
MLA_V = 128
KV_RANK = 512
ROPE_THETA = 10000.0
N_GROUPS = 8
EXPERTS_PER_GROUP = 8
N_EXPERTS = N_GROUPS * EXPERTS_PER_GROUP
LN_EPS = 1e-5
RMS_EPS = 1e-6

DSA_Q_DIM = N_DSA_HEADS * DSA_HEAD_DIM
DSA_KV_DIM = DSA_KV_HEADS * DSA_HEAD_DIM
IDX_Q_DIM = IDX_HEADS * IDX_DIM
MLA_QK_DIM = MLA_NOPE + MLA_ROPE
MLA_Q_DIM = MLA_HEADS * MLA_QK_DIM
MLA_OUT_DIM = MLA_HEADS * MLA_V

LANES = 128
VMEM_LIMIT_MB = 56

DSA_TQ = 128
DSA_KC = 512
MLA_T = 512
MOE_BLK = 256
NEG = -1e30
LOG2E = 1.4426950408889634
INT_MIN = -2 ** 31
FLT_MAX = 3.4028234663852886e38

F32 = jnp.float32
BF16 = jnp.bfloat16


def _cparams(n_axes, vmem_mb=VMEM_LIMIT_MB):
    return pltpu.CompilerParams(dimension_semantics=("arbitrary",) * n_axes,
                                vmem_limit_bytes=vmem_mb * 1024 * 1024)


def _pick(n, prefs):
    for p in prefs:
        if n % p == 0:
            return p
    return n


def _lane_tiles(x):
    return [x[:, j * LANES:(j + 1) * LANES] for j in range(x.shape[1] // LANES)]


def _row_max(x):
    return jnp.max(functools.reduce(jnp.maximum, _lane_tiles(x)), axis=1, keepdims=True)


def _row_sum(x):
    return jnp.sum(functools.reduce(jnp.add, _lane_tiles(x)), axis=1, keepdims=True)


def _mm_kernel(a_ref, w_ref, o_ref):
    o_ref[...] = jnp.dot(a_ref[...], w_ref[...], preferred_element_type=F32).astype(o_ref.dtype)


def _matmul(a, w, out_dtype, tn):
    m, k = a.shape
    n = w.shape[1]
    tm = _pick(m, (512, 256, 128))
    return pl.pallas_call(
        _mm_kernel,
        grid=(n // tn, m // tm),
        in_specs=[pl.BlockSpec((tm, k), lambda j, i: (i, 0)),
                  pl.BlockSpec((k, tn), lambda j, i: (0, j))],
        out_specs=pl.BlockSpec((tm, tn), lambda j, i: (i, j)),
        out_shape=jax.ShapeDtypeStruct((m, n), out_dtype),
        compiler_params=_cparams(2),
    )(a, w)


def _dsa_kernel(q_ref, iq_ref, w_ref, kidx_ref, k_ref, v_ref, o_ref,
                sc_ref, thr_ref, jcut_ref, s_ref, m_ref, l_ref, acc_ref, *, tq, kc, n_sel, seq):
    i = pl.program_id(1)
    nch = (i * tq + tq + kc - 1) // kc
    row = lax.broadcasted_iota(jnp.int32, (tq, kc), 0)
    lane = lax.broadcasted_iota(jnp.int32, (tq, kc), 1)
    qpos = i * tq + row

    iq = iq_ref[0].reshape(IDX_HEADS * tq, IDX_DIM)
    w = w_ref[0] * (IDX_DIM ** -0.5 * IDX_HEADS ** -0.5)

    def score_body(c, carry):
        off = pl.multiple_of(c * kc, kc)
        kx = kidx_ref[0, pl.ds(off, kc), :]
        logits = lax.dot_general(iq, kx, (((1,), (1,)), ((), ())), preferred_element_type=F32)
        sc = jnp.zeros((tq, kc), F32)
        for h in range(IDX_HEADS):
            sc = sc + jnp.maximum(logits[h * tq:(h + 1) * tq], 0.0) * w[:, h:h + 1]
        sc_ref[c] = jnp.where(off + lane <= qpos, sc, -jnp.inf)
        return carry

    lax.fori_loop(0, nch, score_body, 0)

    def key_to_f32(key):
        return pltpu.bitcast(jnp.where(key < 0, key ^ jnp.int32(0x7FFFFFFF), key), F32)

    key_lowest = INT_MIN + 0x00800000
    thr_ref[...] = jnp.full((tq, 1), -FLT_MAX, F32)
    jcut_ref[...] = jnp.full((tq, 1), seq, jnp.int32)

    def count(pred_fn):
        def body(c, acc):
            sch = sc_ref[c]
            for j in range(kc // LANES):
                acc = acc + pred_fn(sch[:, j * LANES:(j + 1) * LANES], c * kc + j * LANES)
            return acc
        acc = lax.fori_loop(0, nch, body, jnp.zeros((tq, LANES), jnp.int32))
        return jnp.sum(acc.astype(F32), axis=1, keepdims=True).astype(jnp.int32)

    lane1 = lax.broadcasted_iota(jnp.int32, (tq, LANES), 1)

    @pl.when(i * tq + tq > n_sel)
    def _():
        def bit_body(b, t):
            cand = t + lax.shift_left(jnp.int32(1), 31 - b)
            cand_b = jnp.broadcast_to(key_to_f32(cand), (tq, LANES))
            cnt = count(lambda sch, _: jnp.where(sch >= cand_b, 1, 0))
            return jnp.where(cnt >= n_sel, cand, t)

        t = lax.fori_loop(0, 32, bit_body, jnp.full((tq, 1), INT_MIN, jnp.int32))
        t_f = key_to_f32(jnp.maximum(t, key_lowest))
        thr_ref[...] = t_f
        t_b = jnp.broadcast_to(t_f, (tq, LANES))
        c_ge = count(lambda sch, _: jnp.where(sch >= t_b, 1, 0))
        c_gt = count(lambda sch, _: jnp.where(sch > t_b, 1, 0))
        excess = c_ge > n_sel
        need = n_sel - c_gt

        @pl.when(jnp.max(jnp.where(excess, 1.0, 0.0)) > 0.0)
        def _():
            nbits = max(1, int(np.ceil(np.log2(seq))))

            def jbit(b, jc):
                cand = jc + lax.shift_left(jnp.int32(1), nbits - 1 - b)
                cand_b = jnp.broadcast_to(cand, (tq, LANES))
                cnt = count(lambda sch, base: jnp.where(
                    sch == t_b, jnp.where(base + lane1 < cand_b, 1, 0), 0))
                return jnp.where(cnt < need, cand, jc)

            jc = lax.fori_loop(0, nbits, jbit, jnp.zeros((tq, 1), jnp.int32))
            jcut_ref[...] = jnp.where(excess, jc, seq)

    qs = (q_ref[...] * (DSA_HEAD_DIM ** -0.5 * LOG2E)).astype(BF16)
    qg = [jnp.concatenate([qs[:, (g * DSA_REP + r) * DSA_HEAD_DIM:(g * DSA_REP + r + 1) * DSA_HEAD_DIM]
                           for r in range(DSA_REP)], axis=0) for g in range(DSA_KV_HEADS)]
    m_ref[...] = jnp.full(m_ref.shape, NEG, F32)
    l_ref[...] = jnp.zeros(l_ref.shape, F32)
    acc_ref[...] = jnp.zeros(acc_ref.shape, F32)
    t = thr_ref[...]
    jc = jcut_ref[...]

    def scores(c, slot):
        off = pl.multiple_of(c * kc, kc)
        for g in range(DSA_KV_HEADS):
            s_ref[slot, g] = lax.dot_general(
                qg[g], k_ref[pl.ds(off, kc), g * DSA_HEAD_DIM:(g + 1) * DSA_HEAD_DIM],
                (((1,), (1,)), ((), ())), preferred_element_type=F32)

    def update(c, slot):
        off = pl.multiple_of(c * kc, kc)
        sc = sc_ref[c]
        kpos = off + lane
        sel = jnp.where(kpos <= jc, jnp.where(sc >= t, 1, 0), jnp.where(sc > t, 1, 0)) > 0
        md = jnp.where(sel, (qpos - kpos).astype(F32), -NEG)
        for g in range(DSA_KV_HEADS):
            s = s_ref[slot, g]
            vg = v_ref[pl.ds(off, kc), g * DSA_HEAD_DIM:(g + 1) * DSA_HEAD_DIM]
            e = jnp.concatenate(
                [s[r * tq:(r + 1) * tq] - md * (LOG2E * 2.0 ** (-8.0 * (g * DSA_REP + r + 1) / N_DSA_HEADS))
                 for r in range(DSA_REP)], axis=0)
            m_old = m_ref[g]
            m_new = jnp.maximum(m_old, _row_max(e))
            p = jnp.exp2(e - m_new)
            alpha = jnp.exp2(m_old - m_new)
            l_ref[g] = alpha * l_ref[g] + _row_sum(p)
            acc_ref[g] = alpha * acc_ref[g] + jnp.dot(p.astype(BF16), vg, preferred_element_type=F32)
            m_ref[g] = m_new

    scores(0, 0)
    n_pairs = (nch - 1) // 2

    def att_body(p, carry):
        c = 2 * p
        scores(c + 1, 1)
        update(c, 0)
        scores(c + 2, 0)
        update(c + 1, 1)
        return carry

    lax.fori_loop(0, n_pairs, att_body, 0)
    c_tail = 2 * n_pairs

    @pl.when(nch - c_tail == 2)
    def _():
        scores(c_tail + 1, 1)
        update(c_tail, 0)
        update(c_tail + 1, 1)

    @pl.when(nch - c_tail == 1)
    def _():
        update(c_tail, 0)

    for g in range(DSA_KV_HEADS):
        o = acc_ref[g] / l_ref[g]
        for r in range(DSA_REP):
            hh = g * DSA_REP + r
            o_ref[:, hh * DSA_HEAD_DIM:(hh + 1) * DSA_HEAD_DIM] = o[r * tq:(r + 1) * tq].astype(o_ref.dtype)


def _dsa_attention(zs, zb, iq_hm, w_idx, kidx, batch, seq, off_k, off_v):
    tq = min(DSA_TQ, seq)
    kc = min(DSA_KC, seq)
    nq = seq // tq
    n_sel = min(TOPK_MAX, seq // 4)
    kern = functools.partial(_dsa_kernel, tq=tq, kc=kc, n_sel=n_sel, seq=seq)
    return pl.pallas_call(
        kern,
        name="dsa",
        grid=(batch, nq),
        in_specs=[
            pl.BlockSpec((tq, DSA_Q_DIM), lambda b, i: (b * nq + i, 0)),
            pl.BlockSpec((1, IDX_HEADS, tq, IDX_DIM), lambda b, i: (b, 0, i, 0)),
            pl.BlockSpec((1, tq, IDX_HEADS), lambda b, i: (b, i, 0)),
            pl.BlockSpec((1, seq, IDX_DIM), lambda b, i: (b, 0, 0)),
            pl.BlockSpec((seq, DSA_KV_DIM), lambda b, i: (b, off_k // DSA_KV_DIM)),
            pl.BlockSpec((seq, DSA_KV_DIM), lambda b, i: (b, off_v // DSA_KV_DIM)),
        ],
        out_specs=pl.BlockSpec((tq, DSA_Q_DIM), lambda b, i: (b * nq + i, 0)),
        out_shape=jax.ShapeDtypeStruct((batch * seq, DSA_Q_DIM), BF16),
        scratch_shapes=[
            pltpu.VMEM((seq // kc, tq, kc), F32),
            pltpu.VMEM((tq, 1), F32),
            pltpu.VMEM((tq, 1), jnp.int32),
            pltpu.VMEM((2, DSA_KV_HEADS, DSA_REP * tq, kc), F32),
            pltpu.VMEM((DSA_KV_HEADS, DSA_REP * tq, 1), F32),
            pltpu.VMEM((DSA_KV_HEADS, DSA_REP * tq, 1), F32),
            pltpu.VMEM((DSA_KV_HEADS, DSA_REP * tq, DSA_HEAD_DIM), F32),
        ],
        compiler_params=_cparams(2),
    )(zs, iq_hm, w_idx, kidx, zb, zb)


def _mla_q_kernel(qn_ref, qr_ref, qs_ref, cos_ref, sin_ref, o_ref):
    c = MLA_QK_DIM ** -0.5 * LOG2E
    pe = ((qr_ref[...] * cos_ref[...] + qs_ref[...] * sin_ref[...]) * c).astype(o_ref.dtype)
    qn = (qn_ref[...] * c).astype(o_ref.dtype)
    for h in range(MLA_HEADS):
        o_ref[0, h, :, 0:MLA_NOPE] = qn[:, h * MLA_NOPE:(h + 1) * MLA_NOPE]
        o_ref[0, h, :, MLA_NOPE:MLA_QK_DIM] = pe[:, h * MLA_ROPE:(h + 1) * MLA_ROPE]


def _mla_q_prep(zs, cos_q, sin_q, batch, seq, off_qn, off_qr, off_qs):
    tm = _pick(seq, (512, 256, 128))
    ns = seq // tm
    wr = MLA_HEADS * MLA_ROPE
    return pl.pallas_call(
        _mla_q_kernel,
        name="mla_q",
        grid=(batch * ns,),
        in_specs=[
            pl.BlockSpec((tm, MLA_HEADS * MLA_NOPE), lambda i: (i, off_qn // (MLA_HEADS * MLA_NOPE))),
            pl.BlockSpec((tm, wr), lambda i: (i, off_qr // wr)),
            pl.BlockSpec((tm, wr), lambda i: (i, off_qs // wr)),
            pl.BlockSpec((tm, wr), lambda i: (i % ns, 0)),
            pl.BlockSpec((tm, wr), lambda i: (i % ns, 0)),
        ],
        out_specs=pl.BlockSpec((1, MLA_HEADS, tm, MLA_QK_DIM), lambda i: (i // ns, 0, i % ns, 0)),
        out_shape=jax.ShapeDtypeStruct((batch, MLA_HEADS, seq, MLA_QK_DIM), BF16),
        compiler_params=_cparams(1),
    )(zs, zs, zs, cos_q, sin_q)


def _mla_kv_kernel(c_ref, kr_ref, g_ref, w_ref, cos_ref, sin_ref, k_ref, v_ref):
    c = c_ref[...]
    cn = c * lax.rsqrt(jnp.mean(c * c, axis=-1, keepdims=True) + RMS_EPS) * g_ref[...]
    kv = jnp.dot(cn.astype(BF16), w_ref[...], preferred_element_type=F32)
    kr = kr_ref[...]
    pe = (kr[:, 0:MLA_ROPE] * cos_ref[...] + kr[:, MLA_ROPE:2 * MLA_ROPE] * sin_ref[...]).astype(k_ref.dtype)
    for h in range(MLA_HEADS):
        k_ref[0, h, :, 0:MLA_NOPE] = kv[:, h * MLA_NOPE:(h + 1) * MLA_NOPE].astype(k_ref.dtype)
        k_ref[0, h, :, MLA_NOPE:MLA_QK_DIM] = pe
        v0 = MLA_HEADS * MLA_NOPE + h * MLA_V
        v_ref[0, h] = kv[:, v0:v0 + MLA_V].astype(v_ref.dtype)


def _mla_kv_prep(zs, g_ckv, w_kv, cos_k, sin_k, batch, seq, off_c, off_kr):
    tm = _pick(seq, (512, 256, 128))
    ns = seq // tm
    nkv = MLA_HEADS * (MLA_NOPE + MLA_V)
    return pl.pallas_call(
        _mla_kv_kernel,
        name="mla_kv",
        grid=(batch * ns,),
        in_specs=[
            pl.BlockSpec((tm, KV_RANK), lambda i: (i, off_c // KV_RANK)),
            pl.BlockSpec((tm, 2 * MLA_ROPE), lambda i: (i, off_kr // (2 * MLA_ROPE))),
            pl.BlockSpec((1, KV_RANK), lambda i: (0, 0)),
            pl.BlockSpec((KV_RANK, nkv), lambda i: (0, 0)),
            pl.BlockSpec((tm, MLA_ROPE), lambda i: (i % ns, 0)),
            pl.BlockSpec((tm, MLA_ROPE), lambda i: (i % ns, 0)),
        ],
        out_specs=[
            pl.BlockSpec((1, MLA_HEADS, tm, MLA_QK_DIM), lambda i: (i // ns, 0, i % ns, 0)),
            pl.BlockSpec((1, MLA_HEADS, tm, MLA_V), lambda i: (i // ns, 0, i % ns, 0)),
        ],
        out_shape=[jax.ShapeDtypeStruct((batch, MLA_HEADS, seq, MLA_QK_DIM), BF16),
                   jax.ShapeDtypeStruct((batch, MLA_HEADS, seq, MLA_V), BF16)],
        compiler_params=_cparams(1),
    )(zs, zs, g_ckv, w_kv, cos_k, sin_k)


def _mla_flash_kernel(q_ref, k_ref, v_ref, o_ref, s_ref, m_ref, l_ref, acc_ref, *, t):
    qi = pl.program_id(2)
    q = q_ref[0, 0]
    m_ref[...] = jnp.full(m_ref.shape, NEG, F32)
    l_ref[...] = jnp.zeros(l_ref.shape, F32)
    acc_ref[...] = jnp.zeros(acc_ref.shape, F32)

    def scores(j, slot):
        off = pl.multiple_of(j * t, t)
        s_ref[slot] = lax.dot_general(q, k_ref[0, 0, pl.ds(off, t), :], (((1,), (1,)), ((), ())),
                                      preferred_element_type=F32)

    def update(j, slot, diagonal):
        off = pl.multiple_of(j * t, t)
        s = s_ref[slot]
        if diagonal:
            row = lax.broadcasted_iota(jnp.int32, (t, t), 0)
            col = lax.broadcasted_iota(jnp.int32, (t, t), 1)
            s = jnp.where(col <= row, s, NEG)
        m_old = m_ref[...]
        m_new = jnp.maximum(m_old, _row_max(s))
        p = jnp.exp2(s - m_new)
        alpha = jnp.exp2(m_old - m_new)
        l_ref[...] = alpha * l_ref[...] + _row_sum(p)
        acc_ref[...] = alpha * acc_ref[...] + jnp.dot(p.astype(BF16), v_ref[0, 0, pl.ds(off, t), :],
                                                      preferred_element_type=F32)
        m_ref[...] = m_new

    scores(0, 0)
    n_pairs = qi // 2

    def body(p, carry):
        j = 2 * p
        scores(j + 1, 1)
        update(j, 0, False)
        scores(j + 2, 0)
        update(j + 1, 1, False)
        return carry

    lax.fori_loop(0, n_pairs, body, 0)
    j_tail = 2 * n_pairs

    @pl.when(qi - j_tail == 1)
    def _():
        scores(j_tail + 1, 1)
        update(j_tail, 0, False)
        update(j_tail + 1, 1, True)

    @pl.when(qi == j_tail)
    def _():
        update(j_tail, 0, True)

    o_ref[0] = (acc_ref[...] / l_ref[...]).astype(o_ref.dtype)


def _mla_flash(q_cat, k_cat, v, batch, seq):
    t = min(MLA_T, seq)
    n = seq // t
    kern = functools.partial(_mla_flash_kernel, t=t)
    return pl.pallas_call(
        kern,
        name="mla_flash",
        grid=(batch, MLA_HEADS, n),
        in_specs=[
            pl.BlockSpec((1, 1, t, MLA_QK_DIM), lambda b, h, qi: (b, h, qi, 0)),
            pl.BlockSpec((1, 1, seq, MLA_QK_DIM), lambda b, h, qi: (b, h, 0, 0)),
            pl.BlockSpec((1, 1, seq, MLA_V), lambda b, h, qi: (b, h, 0, 0)),
        ],
        out_specs=pl.BlockSpec((1, t, MLA_V), lambda b, h, qi: (b, qi, h)),
        out_shape=jax.ShapeDtypeStruct((batch, seq, MLA_OUT_DIM), BF16),
        scratch_shapes=[pltpu.VMEM((2, t, t), F32), pltpu.VMEM((t, 1), F32), pltpu.VMEM((t, 1), F32),
                        pltpu.VMEM((t, MLA_V), F32)],
        compiler_params=_cparams(3),
    )(q_cat, k_cat, v)


def _merge_kernel(oa_ref, ob_ref, wa_ref, wb_ref, ga_ref, gb_ref, o_ref):
    pa = jnp.dot(oa_ref[...], wa_ref[...], preferred_element_type=F32)
    pb = jnp.dot(ob_ref[...], wb_ref[...], preferred_element_type=F32)
    o_ref[...] = (jax.nn.sigmoid(ga_ref[...]) * pa + jax.nn.sigmoid(gb_ref[...]) * pb).astype(o_ref.dtype)


def _gated_merge(o_a, o_b, w_pa, w_pb, gates):
    m = o_a.shape[0]
    d = w_pa.shape[1]
    tm = _pick(m, (512, 256, 128))
    tn = _pick(d, (1024, 512, 256, 128))
    nn = d // tn
    return pl.pallas_call(
        _merge_kernel,
        name="gated_merge",
        grid=(nn, m // tm),
        in_specs=[
            pl.BlockSpec((tm, o_a.shape[1]), lambda j, i: (i, 0)),
            pl.BlockSpec((tm, o_b.shape[1]), lambda j, i: (i, 0)),
            pl.BlockSpec((w_pa.shape[0], tn), lambda j, i: (0, j)),
            pl.BlockSpec((w_pb.shape[0], tn), lambda j, i: (0, j)),
            pl.BlockSpec((tm, tn), lambda j, i: (i, j)),
            pl.BlockSpec((tm, tn), lambda j, i: (i, j + nn)),
        ],
        out_specs=pl.BlockSpec((tm, tn), lambda j, i: (i, j)),
        out_shape=jax.ShapeDtypeStruct((m, d), BF16),
        compiler_params=_cparams(2),
    )(o_a, o_b, w_pa, w_pb, gates, gates)


def _layer_norm(y, g, b):
    mu = jnp.mean(y, axis=-1, keepdims=True)
    dlt = y - mu
    var = jnp.mean(dlt * dlt, axis=-1, keepdims=True)
    return dlt * lax.rsqrt(var + LN_EPS) * g + b


def _proj_ln_kernel(a_ref, w_ref, x_ref, g_ref, b_ref, o_ref, *, alpha):
    y = alpha * x_ref[...] + jnp.dot(a_ref[...], w_ref[...], preferred_element_type=F32)
    o_ref[...] = _layer_norm(y, g_ref[...], b_ref[...])


def _proj_ln(a, w, x, g, b, alpha):
    m, d = x.shape
    tm = _pick(m, (256, 128))
    return pl.pallas_call(
        functools.partial(_proj_ln_kernel, alpha=alpha),
        name="proj_ln",
        grid=(m // tm,),
        in_specs=[
            pl.BlockSpec((tm, a.shape[1]), lambda i: (i, 0)),
            pl.BlockSpec(w.shape, lambda i: (0, 0)),
            pl.BlockSpec((tm, d), lambda i: (i, 0)),
            pl.BlockSpec((1, d), lambda i: (0, 0)),
            pl.BlockSpec((1, d), lambda i: (0, 0)),
        ],
        out_specs=pl.BlockSpec((tm, d), lambda i: (i, 0)),
        out_shape=jax.ShapeDtypeStruct((m, d), F32),
        compiler_params=_cparams(1),
    )(a, w, x, g, b)


def _router_kernel(h_ref, whi_ref, wlo_ref, b_ref, ids_ref, wts_ref):
    h = h_ref[...]
    h_hi = h.astype(BF16)
    h_lo = (h - h_hi.astype(F32)).astype(BF16)
    logits = (jnp.dot(h_hi, whi_ref[...], preferred_element_type=F32)
              + jnp.dot(h_hi, wlo_ref[...], preferred_element_type=F32)
              + jnp.dot(h_lo, whi_ref[...], preferred_element_type=F32)) + b_ref[...]
    tm = h.shape[0]
    lane = lax.broadcasted_iota(jnp.int32, (tm, LANES), 1)
    lane_f = lane.astype(F32)
    big = float(4 * LANES)
    ninf = -jnp.inf
    gl = jnp.where(lane < N_GROUPS, logits, ninf)
    ge = jnp.exp(gl - jnp.max(gl, axis=1, keepdims=True))
    gprob = ge / jnp.sum(ge, axis=1, keepdims=True)
    gw = jnp.max(gprob, axis=1, keepdims=True)
    gidx = jnp.min(jnp.where(gprob == gw, lane_f, big), axis=1, keepdims=True).astype(jnp.int32)
    in_group = lax.shift_right_arithmetic(lane - N_GROUPS, 3) == gidx
    el = jnp.where(in_group, logits, ninf)
    ee = jnp.exp(el - jnp.max(el, axis=1, keepdims=True))
    ep = jnp.where(in_group, ee / jnp.sum(ee, axis=1, keepdims=True), -1.0)
    p1 = jnp.max(ep, axis=1, keepdims=True)
    i1 = jnp.min(jnp.where(ep == p1, lane_f, big), axis=1, keepdims=True).astype(jnp.int32)
    ep2 = jnp.where(lane == i1, -1.0, ep)
    p2 = jnp.max(ep2, axis=1, keepdims=True)
    i2 = jnp.min(jnp.where(ep2 == p2, lane_f, big), axis=1, keepdims=True).astype(jnp.int32)
    den = p1 + p2
    w1 = gw * (p1 / den)
    w2 = gw * (p2 / den)
    ids_ref[...] = jnp.where(lane == 0, i1 - N_GROUPS, jnp.where(lane == 1, i2 - N_GROUPS, 0))
    wts_ref[...] = jnp.where(lane == 0, w1, jnp.where(lane == 1, w2, 0.0))


def _router(h, w_hi, w_lo, b_r):
    m, d = h.shape
    tm = _pick(m, (512, 256, 128))
    return pl.pallas_call(
        _router_kernel,
        name="router",
        grid=(m // tm,),
        in_specs=[
            pl.BlockSpec((tm, d), lambda i: (i, 0)),
            pl.BlockSpec((d, LANES), lambda i: (0, 0)),
            pl.BlockSpec((d, LANES), lambda i: (0, 0)),
            pl.BlockSpec((1, LANES), lambda i: (0, 0)),
        ],
        out_specs=[pl.BlockSpec((tm, LANES), lambda i: (i, 0)),
                   pl.BlockSpec((tm, LANES), lambda i: (i, 0))],
        out_shape=[jax.ShapeDtypeStruct((m, LANES), jnp.int32),
                   jax.ShapeDtypeStruct((m, LANES), F32)],
        compiler_params=_cparams(1),
    )(h, w_hi, w_lo, b_r)


def _row_copy(src_hbm, src_row, dst_vmem, dst_row, sem):
    return pltpu.make_async_copy(src_hbm.at[pl.ds(src_row, 1)], dst_vmem.at[pl.ds(dst_row, 1)], sem)


def _gather_kernel(tok_ref, h_hbm, o_ref, buf, sem, *, rows):
    i = pl.program_id(0)
    slot = i % 2

    def issue(blk, dst_slot):
        def start(r, c):
            _row_copy(h_hbm, tok_ref[blk * rows + r], buf.at[dst_slot], r, sem.at[dst_slot]).start()
            return c
        lax.fori_loop(0, rows, start, 0, unroll=8)

    @pl.when(i == 0)
    def _():
        issue(0, 0)

    @pl.when(i + 1 < pl.num_programs(0))
    def _():
        issue(i + 1, 1 - slot)

    pltpu.make_async_copy(h_hbm.at[pl.ds(0, rows)], buf.at[slot], sem.at[slot]).wait()
    o_ref[...] = buf[slot].astype(o_ref.dtype)


def _gather_rows(buf_tok, h):
    cap = buf_tok.shape[0]
    d = h.shape[1]
    rows = MOE_BLK
    return pl.pallas_call(
        functools.partial(_gather_kernel, rows=rows),
        name="moe_gather",
        grid_spec=pltpu.PrefetchScalarGridSpec(
            num_scalar_prefetch=1,
            grid=(cap // rows,),
            in_specs=[pl.BlockSpec(memory_space=pl.ANY)],
            out_specs=pl.BlockSpec((rows, d), lambda i, tok: (i, 0)),
            scratch_shapes=[pltpu.VMEM((2, rows, d), F32), pltpu.SemaphoreType.DMA((2,))],
        ),
        out_shape=jax.ShapeDtypeStruct((cap, d), BF16),
        compiler_params=_cparams(1),
    )(buf_tok, h)


def _expert_kernel(be_ref, nu_ref, x_ref, wg_ref, wu_ref, wd_ref, o_ref):
    i = pl.program_id(0)

    @pl.when(i < nu_ref[0])
    def _():
        x = x_ref[...]
        g = jnp.dot(x, wg_ref[...], preferred_element_type=F32)
        u = jnp.dot(x, wu_ref[...], preferred_element_type=F32)
        a = (g * jax.nn.sigmoid(g) * u).astype(BF16)
        o_ref[...] = jnp.dot(a, wd_ref[...], preferred_element_type=F32)

    @pl.when(i >= nu_ref[0])
    def _():
        o_ref[...] = jnp.zeros(o_ref.shape, o_ref.dtype)


def _experts(blk_expert, n_used, xb, wg, wu, wd):
    cap, d = xb.shape
    ff = wg.shape[2]
    return pl.pallas_call(
        _expert_kernel,
        name="moe_experts",
        grid_spec=pltpu.PrefetchScalarGridSpec(
            num_scalar_prefetch=2,
            grid=(cap // MOE_BLK,),
            in_specs=[
                pl.BlockSpec((MOE_BLK, d), lambda i, be, nu: (i, 0)),
                pl.BlockSpec((None, d, ff), lambda i, be, nu: (be[i], 0, 0)),
                pl.BlockSpec((None, d, ff), lambda i, be, nu: (be[i], 0, 0)),
                pl.BlockSpec((None, ff, d), lambda i, be, nu: (be[i], 0, 0)),
            ],
            out_specs=pl.BlockSpec((MOE_BLK, d), lambda i, be, nu: (i, 0)),
        ),
        out_shape=jax.ShapeDtypeStruct((cap, d), F32),
        compiler_params=_cparams(1),
    )(blk_expert, n_used, xb, wg, wu, wd)


def _combine_kernel(dest_ref, y_hbm, h_ref, wts_ref, g_ref, b_ref, o_ref, buf, sem, *, tm, alpha):
    i = pl.program_id(0)
    slot = i % 2

    def issue(tile, dst_slot):
        def start(r, c):
            a = 2 * (tile * tm + r)
            _row_copy(y_hbm, dest_ref[a], buf.at[dst_slot, 0], r, sem.at[dst_slot]).start()
            _row_copy(y_hbm, dest_ref[a + 1], buf.at[dst_slot, 1], r, sem.at[dst_slot]).start()
            return c
        lax.fori_loop(0, tm, start, 0, unroll=8)

    @pl.when(i == 0)
    def _():
        issue(0, 0)

    @pl.when(i + 1 < pl.num_programs(0))
    def _():
        issue(i + 1, 1 - slot)

    for k in range(2):
        pltpu.make_async_copy(y_hbm.at[pl.ds(0, tm)], buf.at[slot, k], sem.at[slot]).wait()
    wts = wts_ref[...]
    f = buf[slot, 0] * wts[:, 0:1] + buf[slot, 1] * wts[:, 1:2]
    o_ref[...] = _layer_norm(alpha * h_ref[...] + f, g_ref[...], b_ref[...])


def _combine_ln(dest, yb, h, wts, g, b, alpha):
    m, d = h.shape
    tm = _pick(m, (256, 128))
    return pl.pallas_call(
        functools.partial(_combine_kernel, tm=tm, alpha=alpha),
        name="moe_combine",
        grid_spec=pltpu.PrefetchScalarGridSpec(
            num_scalar_prefetch=1,
            grid=(m // tm,),
            in_specs=[
                pl.BlockSpec(memory_space=pl.ANY),
                pl.BlockSpec((tm, d), lambda i, dst: (i, 0)),
                pl.BlockSpec((tm, LANES), lambda i, dst: (i, 0)),
                pl.BlockSpec((1, d), lambda i, dst: (0, 0)),
                pl.BlockSpec((1, d), lambda i, dst: (0, 0)),
            ],
            out_specs=pl.BlockSpec((tm, d), lambda i, dst: (i, 0)),
            scratch_shapes=[pltpu.VMEM((2, 2, tm, d), F32), pltpu.SemaphoreType.DMA((2,))],
        ),
        out_shape=jax.ShapeDtypeStruct((m, d), F32),
        compiler_params=_cparams(1),
    )(dest, yb, h, wts, g, b)


def _dispatch_tables(ids, n_tok):
    flat_e = ids[:, :2].reshape(-1)
    n_assign = flat_e.shape[0]
    onehot = (flat_e[:, None] == jnp.arange(N_EXPERTS, dtype=jnp.int32)[None, :]).astype(jnp.int32)
    csum = jnp.cumsum(onehot, axis=0)
    rank = jnp.sum(onehot * csum, axis=1) - 1
    counts = csum[-1]
    padded = (counts + MOE_BLK - 1) // MOE_BLK * MOE_BLK
    pend = jnp.cumsum(padded)
    pstart = pend - padded
    dest = (pstart[flat_e] + rank).astype(jnp.int32)
    cap = -(-n_assign // MOE_BLK) * MOE_BLK + N_EXPERTS * MOE_BLK
    n_blk = cap // MOE_BLK
    buf_tok = jnp.zeros((cap,), jnp.int32).at[dest].set(jnp.arange(n_assign, dtype=jnp.int32) // 2)
    blk_start = jnp.arange(n_blk, dtype=jnp.int32) * MOE_BLK
    blk_expert = jnp.minimum(jnp.sum((pend[None, :] <= blk_start[:, None]).astype(jnp.int32), axis=1),
                             N_EXPERTS - 1).astype(jnp.int32)
    n_used = (pend[-1] // MOE_BLK).astype(jnp.int32).reshape(1)
    return dest, buf_tok, blk_expert, n_used


def _rope_tables(seq):
    half = MLA_ROPE // 2
    freqs = ROPE_THETA ** (-jnp.arange(half, dtype=F32) / half)
    ang = jnp.arange(seq, dtype=F32)[:, None] * freqs
    cos = jnp.cos(ang)
    sin = jnp.sin(ang)
    return jnp.concatenate([cos, cos], -1), jnp.concatenate([sin, sin], -1)


def _swap_halves_neg(w):
    half = w.shape[-1] // 2
    return jnp.concatenate([-w[..., half:], w[..., :half]], axis=-1)


def _layer(x, w_in, w_ukv, g_ckv, w_pa, w_pb, w_o, ln1_g, ln1_b, w_rg, b_rg, w_re, b_re,
           w_gate_e, w_up_e, w_down_e, ln2_g, ln2_b, alpha):
    batch, seq, d = x.shape
    n_tok = batch * seq
    x2 = x.reshape(n_tok, d)
    xb16 = x2.astype(BF16)

    splits = (DSA_Q_DIM, DSA_KV_DIM, DSA_KV_DIM, IDX_Q_DIM, IDX_DIM, IDX_HEADS, MLA_Q_DIM, KV_RANK, MLA_ROPE, 2 * d)
    offs = np.cumsum((0,) + splits)
    w_dq, w_dk, w_dv, w_iq, w_ik, w_iw, w_mq, w_ckv, w_kr, w_gates = [
        w_in[:, offs[j]:offs[j + 1]] for j in range(len(splits))]
    w_mq3 = w_mq.reshape(d, MLA_HEADS, MLA_QK_DIM)
    w_qn = w_mq3[:, :, :MLA_NOPE].reshape(d, MLA_HEADS * MLA_NOPE)
    w_qr = w_mq3[:, :, MLA_NOPE:]
    w_qs = _swap_halves_neg(w_qr)
    wr = MLA_HEADS * MLA_ROPE
    wb = jnp.concatenate([w_iq, w_dk, w_dv], axis=1).astype(BF16)
    off_k = IDX_Q_DIM
    off_v = off_k + DSA_KV_DIM
    pad_i = jnp.zeros((d, LANES - IDX_DIM - IDX_HEADS), F32)
    ws = jnp.concatenate([w_dq, w_qn, w_qr.reshape(d, wr), w_qs.reshape(d, wr), w_ckv, w_ik, w_iw, pad_i,
                          w_kr, _swap_halves_neg(w_kr)], axis=1).astype(BF16)
    off_qn = DSA_Q_DIM
    off_qr = off_qn + MLA_HEADS * MLA_NOPE
    off_qs = off_qr + wr
    off_c = off_qs + wr
    off_i = off_c + KV_RANK
    off_kr = off_i + LANES

    zb = _matmul(xb16, wb, BF16, tn=wb.shape[1])
    zs = _matmul(xb16, ws, F32, tn=ws.shape[1] // 2)
    gates = _matmul(xb16, w_gates.astype(BF16), F32, tn=_pick(2 * d, (2048, 1024, 512, 256)))

    iq_hm = zb[:, :IDX_Q_DIM].reshape(batch, seq, IDX_HEADS, IDX_DIM).transpose(0, 2, 1, 3)
    kidx = zs[:, off_i:off_i + IDX_DIM].astype(BF16).reshape(batch, seq, IDX_DIM)
    w_idx = zs[:, off_i + IDX_DIM:off_i + IDX_DIM + IDX_HEADS].reshape(batch, seq, IDX_HEADS)
    o_a = _dsa_attention(zs, zb, iq_hm, w_idx, kidx, batch, seq, off_k, off_v)

    cos2, sin2 = _rope_tables(seq)
    w_ukv3 = w_ukv.reshape(KV_RANK, MLA_HEADS, MLA_NOPE + MLA_V)
    w_kv = jnp.concatenate([w_ukv3[:, :, :MLA_NOPE].reshape(KV_RANK, MLA_HEADS * MLA_NOPE),
                            w_ukv3[:, :, MLA_NOPE:].reshape(KV_RANK, MLA_HEADS * MLA_V)], axis=1).astype(BF16)
    q_cat = _mla_q_prep(zs, jnp.tile(cos2, (1, MLA_HEADS)), jnp.tile(sin2, (1, MLA_HEADS)),
                        batch, seq, off_qn, off_qr, off_qs)
    k_cat, v_mla = _mla_kv_prep(zs, g_ckv.reshape(1, KV_RANK), w_kv, cos2, sin2, batch, seq, off_c, off_kr)
    o_b = _mla_flash(q_cat, k_cat, v_mla, batch, seq).reshape(n_tok, MLA_OUT_DIM)

    merged = _gated_merge(o_a, o_b, w_pa.astype(BF16), w_pb.astype(BF16), gates)
    h = _proj_ln(merged, w_o.astype(BF16), x2, ln1_g.reshape(1, d), ln1_b.reshape(1, d), alpha)

    pad_r = jnp.zeros((d, LANES - N_GROUPS - N_EXPERTS), F32)
    w_r = jnp.concatenate([w_rg, w_re, pad_r], axis=1)
    w_r_hi = w_r.astype(BF16)
    w_r_lo = (w_r - w_r_hi.astype(F32)).astype(BF16)
    b_r = jnp.concatenate([b_rg, b_re, jnp.zeros((LANES - N_GROUPS - N_EXPERTS,), F32)]).reshape(1, LANES)
    ids, wts = _router(h, w_r_hi, w_r_lo, b_r)
    dest, buf_tok, blk_expert, n_used = _dispatch_tables(ids, n_tok)
    xb = _gather_rows(buf_tok, h)
    yb = _experts(blk_expert, n_used, xb, w_gate_e.astype(BF16), w_up_e.astype(BF16), w_down_e.astype(BF16))
    out = _combine_ln(dest, yb, h, wts, ln2_g.reshape(1, d), ln2_b.reshape(1, d), alpha)
    return out.reshape(batch, seq, d)


def kernel(x, w_in, w_ukv, g_ckv, w_pa, w_pb, w_o, ln1_g, ln1_b, w_rg, b_rg, w_re, b_re,
           w_gate_e, w_up_e, w_down_e, ln2_g, ln2_b):
    depth = w_in.shape[0]
    alpha = (2.0 * depth) ** 0.25
    for l in range(depth):
        x = _layer(x, w_in[l], w_ukv[l], g_ckv[l], w_pa[l], w_pb[l], w_o[l], ln1_g[l], ln1_b[l],
                   w_rg[l], b_rg[l], w_re[l], b_re[l], w_gate_e[l], w_up_e[l], w_down_e[l],
                   ln2_g[l], ln2_b[l], alpha)
    return x
```

```python
import functools

import numpy as np
import jax
import jax.numpy as jnp
from jax import lax
from jax.experimental import pallas as pl
from jax.experimental.pallas import tpu as pltpu

N_DSA_HEADS = 8
DSA_HEAD_DIM = 128
DSA_KV_HEADS = 2
DSA_REP = N_DSA_HEADS // DSA_KV_HEADS
IDX_HEADS = 16
IDX_DIM = 64
TOPK_MAX = 256
MLA_HEADS = 8
MLA_NOPE = 128
MLA_ROPE = 64
MLA_V = 128
KV_RANK = 512
ROPE_THETA = 10000.0
N_GROUPS = 8
EXPERTS_PER_GROUP = 8
N_EXPERTS = N_GROUPS * EXPERTS_PER_GROUP
LN_EPS = 1e-5
RMS_EPS = 1e-6

DSA_Q_DIM = N_DSA_HEADS * DSA_HEAD_DIM
DSA_KV_DIM = DSA_KV_HEADS * DSA_HEAD_DIM
IDX_Q_DIM = IDX_HEADS * IDX_DIM
MLA_QK_DIM = MLA_NOPE + MLA_ROPE
MLA_Q_DIM = MLA_HEADS * MLA_QK_DIM
MLA_OUT_DIM = MLA_HEADS * MLA_V

LANES = 128
VMEM_LIMIT_MB = 56

DSA_TQ = 128
DSA_KC = 512
MLA_T = 512
MOE_BLK = 256
ITEM_SUB = 3
MOE_CHUNKS = 4
EXPERT_VMEM_MB = 58
NEG = -1e30
LOG2E = 1.4426950408889634
INT_MIN = -2 ** 31
FLT_MAX = 3.4028234663852886e38

F32 = jnp.float32
BF16 = jnp.bfloat16


def _cparams(n_axes, vmem_mb=VMEM_LIMIT_MB):
    return pltpu.CompilerParams(dimension_semantics=("arbitrary",) * n_axes,
                                vmem_limit_bytes=vmem_mb * 1024 * 1024)


def _pick(n, prefs):
    for p in prefs:
        if n % p == 0:
            return p
    return n


def _lane_tiles(x):
    return [x[:, j * LANES:(j + 1) * LANES] for j in range(x.shape[1] // LANES)]


def _row_max(x):
    return jnp.max(functools.reduce(jnp.maximum, _lane_tiles(x)), axis=1, keepdims=True)


def _row_sum(x):
    return jnp.sum(functools.reduce(jnp.add, _lane_tiles(x)), axis=1, keepdims=True)


def _mm_kernel(a_ref, w_ref, o_ref):
    o_ref[...] = jnp.dot(a_ref[...], w_ref[...], preferred_element_type=F32).astype(o_ref.dtype)


def _matmul(a, w, out_dtype, tn):
    m, k = a.shape
    n = w.shape[1]
    tm = _pick(m, (512, 256, 128))
    return pl.pallas_call(
        _mm_kernel,
        grid=(n // tn, m // tm),
        in_specs=[pl.BlockSpec((tm, k), lambda j, i: (i, 0)),
                  pl.BlockSpec((k, tn), lambda j, i: (0, j))],
        out_specs=pl.BlockSpec((tm, tn), lambda j, i: (i, j)),
        out_shape=jax.ShapeDtypeStruct((m, n), out_dtype),
        compiler_params=_cparams(2),
    )(a, w)


def _dsa_kernel(q_ref, iq_ref, w_ref, kidx_ref, k_ref, v_ref, o_ref,
                sc_ref, thr_ref, jcut_ref, s_ref, m_ref, l_ref, acc_ref, *, tq, kc, n_sel, seq):
    i = pl.program_id(1)
    nch = (i * tq + tq + kc - 1) // kc
    row = lax.broadcasted_iota(jnp.int32, (tq, kc), 0)
    lane = lax.broadcasted_iota(jnp.int32, (tq, kc), 1)
    qpos = i * tq + row

    iq = iq_ref[0].reshape(IDX_HEADS * tq, IDX_DIM)
    w = w_ref[0] * (IDX_DIM ** -0.5 * IDX_HEADS ** -0.5)

    def score_body(c, carry):
        off = pl.multiple_of(c * kc, kc)
        kx = kidx_ref[0, pl.ds(off, kc), :]
        logits = lax.dot_general(iq, kx, (((1,), (1,)), ((), ())), preferred_element_type=F32)
        sc = jnp.zeros((tq, kc), F32)
        for h in range(IDX_HEADS):
            sc = sc + jnp.maximum(logits[h * tq:(h + 1) * tq], 0.0) * w[:, h:h + 1]
        sc_ref[c] = jnp.where(off + lane <= qpos, sc, -jnp.inf)
        return carry

    lax.fori_loop(0, nch, score_body, 0)

    def key_to_f32(key):
        return pltpu.bitcast(jnp.where(key < 0, key ^ jnp.int32(0x7FFFFFFF), key), F32)

    key_lowest = INT_MIN + 0x00800000
    thr_ref[...] = jnp.full((tq, 1), -FLT_MAX, F32)
    jcut_ref[...] = jnp.full((tq, 1), seq, jnp.int32)

    def count(pred_fn):
        def body(c, acc):
            sch = sc_ref[c]
            for j in range(kc // LANES):
                acc = acc + pred_fn(sch[:, j * LANES:(j + 1) * LANES], c * kc + j * LANES)
            return acc
        acc = lax.fori_loop(0, nch, body, jnp.zeros((tq, LANES), jnp.int32))
        return jnp.sum(acc.astype(F32), axis=1, keepdims=True).astype(jnp.int32)

    lane1 = lax.broadcasted_iota(jnp.int32, (tq, LANES), 1)

    @pl.when(i * tq + tq > n_sel)
    def _():
        def bit_body(b, t):
            cand = t + lax.shift_left(jnp.int32(1), 31 - b)
            cand_b = jnp.broadcast_to(key_to_f32(cand), (tq, LANES))
            cnt = count(lambda sch, _: jnp.where(sch >= cand_b, 1, 0))
            return jnp.where(cnt >= n_sel, cand, t)

        t = lax.fori_loop(0, 32, bit_body, jnp.full((tq, 1), INT_MIN, jnp.int32))
        t_f = key_to_f32(jnp.maximum(t, key_lowest))
        thr_ref[...] = t_f
        t_b = jnp.broadcast_to(t_f, (tq, LANES))
        c_ge = count(lambda sch, _: jnp.where(sch >= t_b, 1, 0))
        c_gt = count(lambda sch, _: jnp.where(sch > t_b, 1, 0))
        excess = c_ge > n_sel
        need = n_sel - c_gt

        @pl.when(jnp.max(jnp.where(excess, 1.0, 0.0)) > 0.0)
        def _():
            nbits = max(1, int(np.ceil(np.log2(seq))))

            def jbit(b, jc):
                cand = jc + lax.shift_left(jnp.int32(1), nbits - 1 - b)
                cand_b = jnp.broadcast_to(cand, (tq, LANES))
                cnt = count(lambda sch, base: jnp.where(
                    sch == t_b, jnp.where(base + lane1 < cand_b, 1, 0), 0))
                return jnp.where(cnt < need, cand, jc)

            jc = lax.fori_loop(0, nbits, jbit, jnp.zeros((tq, 1), jnp.int32))
            jcut_ref[...] = jnp.where(excess, jc, seq)

    qs = (q_ref[...] * (DSA_HEAD_DIM ** -0.5 * LOG2E)).astype(BF16)
    qg = [jnp.concatenate([qs[:, (g * DSA_REP + r) * DSA_HEAD_DIM:(g * DSA_REP + r + 1) * DSA_HEAD_DIM]
                           for r in range(DSA_REP)], axis=0) for g in range(DSA_KV_HEADS)]
    m_ref[...] = jnp.full(m_ref.shape, NEG, F32)
    l_ref[...] = jnp.zeros(l_ref.shape, F32)
    acc_ref[...] = jnp.zeros(acc_ref.shape, F32)
    t = thr_ref[...]
    jc = jcut_ref[...]

    def scores(c, slot):
        off = pl.multiple_of(c * kc, kc)
        for g in range(DSA_KV_HEADS):
            s_ref[slot, g] = lax.dot_general(
                qg[g], k_ref[pl.ds(off, kc), g * DSA_HEAD_DIM:(g + 1) * DSA_HEAD_DIM],
                (((1,), (1,)), ((), ())), preferred_element_type=F32)

    def update(c, slot):
        off = pl.multiple_of(c * kc, kc)
        sc = sc_ref[c]
        kpos = off + lane
        sel = jnp.where(kpos <= jc, jnp.where(sc >= t, 1, 0), jnp.where(sc > t, 1, 0)) > 0
        md = jnp.where(sel, (qpos - kpos).astype(F32), -NEG)
        for g in range(DSA_KV_HEADS):
            s = s_ref[slot, g]
            vg = v_ref[pl.ds(off, kc), g * DSA_HEAD_DIM:(g + 1) * DSA_HEAD_DIM]
            e = jnp.concatenate(
                [s[r * tq:(r + 1) * tq] - md * (LOG2E * 2.0 ** (-8.0 * (g * DSA_REP + r + 1) / N_DSA_HEADS))
                 for r in range(DSA_REP)], axis=0)
            m_old = m_ref[g]
            m_new = jnp.maximum(m_old, _row_max(e))
            p = jnp.exp2(e - m_new)
            alpha = jnp.exp2(m_old - m_new)
            l_ref[g] = alpha * l_ref[g] + _row_sum(p)
            acc_ref[g] = alpha * acc_ref[g] + jnp.dot(p.astype(BF16), vg, preferred_element_type=F32)
            m_ref[g] = m_new

    scores(0, 0)
    n_pairs = (nch - 1) // 2

    def att_body(p, carry):
        c = 2 * p
        scores(c + 1, 1)
        update(c, 0)
        scores(c + 2, 0)
        update(c + 1, 1)
        return carry

    lax.fori_loop(0, n_pairs, att_body, 0)
    c_tail = 2 * n_pairs

    @pl.when(nch - c_tail == 2)
    def _():
        scores(c_tail + 1, 1)
        update(c_tail, 0)
        update(c_tail + 1, 1)

    @pl.when(nch - c_tail == 1)
    def _():
        update(c_tail, 0)

    for g in range(DSA_KV_HEADS):
        o = acc_ref[g] / l_ref[g]
        for r in range(DSA_REP):
            hh = g * DSA_REP + r
            o_ref[:, hh * DSA_HEAD_DIM:(hh + 1) * DSA_HEAD_DIM] = o[r * tq:(r + 1) * tq].astype(o_ref.dtype)


def _dsa_attention(zs, zb, iq_hm, w_idx, kidx, batch, seq, off_k, off_v):
    tq = min(DSA_TQ, seq)
    kc = min(DSA_KC, seq)
    nq = seq // tq
    n_sel = min(TOPK_MAX, seq // 4)
    kern = functools.partial(_dsa_kernel, tq=tq, kc=kc, n_sel=n_sel, seq=seq)
    return pl.pallas_call(
        kern,
        name="dsa",
        grid=(batch, nq),
        in_specs=[
            pl.BlockSpec((tq, DSA_Q_DIM), lambda b, i: (b * nq + i, 0)),
            pl.BlockSpec((1, IDX_HEADS, tq, IDX_DIM), lambda b, i: (b, 0, i, 0)),
            pl.BlockSpec((1, tq, IDX_HEADS), lambda b, i: (b, i, 0)),
            pl.BlockSpec((1, seq, IDX_DIM), lambda b, i: (b, 0, 0)),
            pl.BlockSpec((seq, DSA_KV_DIM), lambda b, i: (b, off_k // DSA_KV_DIM)),
            pl.BlockSpec((seq, DSA_KV_DIM), lambda b, i: (b, off_v // DSA_KV_DIM)),
        ],
        out_specs=pl.BlockSpec((tq, DSA_Q_DIM), lambda b, i: (b * nq + i, 0)),
        out_shape=jax.ShapeDtypeStruct((batch * seq, DSA_Q_DIM), BF16),
        scratch_shapes=[
            pltpu.VMEM((seq // kc, tq, kc), F32),
            pltpu.VMEM((tq, 1), F32),
            pltpu.VMEM((tq, 1), jnp.int32),
            pltpu.VMEM((2, DSA_KV_HEADS, DSA_REP * tq, kc), F32),
            pltpu.VMEM((DSA_KV_HEADS, DSA_REP * tq, 1), F32),
            pltpu.VMEM((DSA_KV_HEADS, DSA_REP * tq, 1), F32),
            pltpu.VMEM((DSA_KV_HEADS, DSA_REP * tq, DSA_HEAD_DIM), F32),
        ],
        compiler_params=_cparams(2),
    )(zs, iq_hm, w_idx, kidx, zb, zb)


def _mla_q_kernel(qn_ref, qr_ref, qs_ref, cos_ref, sin_ref, o_ref):
    c = MLA_QK_DIM ** -0.5 * LOG2E
    pe = ((qr_ref[...] * cos_ref[...] + qs_ref[...] * sin_ref[...]) * c).astype(o_ref.dtype)
    qn = (qn_ref[...] * c).astype(o_ref.dtype)
    for h in range(MLA_HEADS):
        o_ref[0, h, :, 0:MLA_NOPE] = qn[:, h * MLA_NOPE:(h + 1) * MLA_NOPE]
        o_ref[0, h, :, MLA_NOPE:MLA_QK_DIM] = pe[:, h * MLA_ROPE:(h + 1) * MLA_ROPE]


def _mla_q_prep(zs, cos_q, sin_q, batch, seq, off_qn, off_qr, off_qs):
    tm = _pick(seq, (512, 256, 128))
    ns = seq // tm
    wr = MLA_HEADS * MLA_ROPE
    return pl.pallas_call(
        _mla_q_kernel,
        name="mla_q",
        grid=(batch * ns,),
        in_specs=[
            pl.BlockSpec((tm, MLA_HEADS * MLA_NOPE), lambda i: (i, off_qn // (MLA_HEADS * MLA_NOPE))),
            pl.BlockSpec((tm, wr), lambda i: (i, off_qr // wr)),
            pl.BlockSpec((tm, wr), lambda i: (i, off_qs // wr)),
            pl.BlockSpec((tm, wr), lambda i: (i % ns, 0)),
            pl.BlockSpec((tm, wr), lambda i: (i % ns, 0)),
        ],
        out_specs=pl.BlockSpec((1, MLA_HEADS, tm, MLA_QK_DIM), lambda i: (i // ns, 0, i % ns, 0)),
        out_shape=jax.ShapeDtypeStruct((batch, MLA_HEADS, seq, MLA_QK_DIM), BF16),
        compiler_params=_cparams(1),
    )(zs, zs, zs, cos_q, sin_q)


def _mla_kv_kernel(c_ref, kr_ref, g_ref, w_ref, cos_ref, sin_ref, k_ref, v_ref):
    c = c_ref[...]
    cn = c * lax.rsqrt(jnp.mean(c * c, axis=-1, keepdims=True) + RMS_EPS) * g_ref[...]
    kv = jnp.dot(cn.astype(BF16), w_ref[...], preferred_element_type=F32)
    kr = kr_ref[...]
    pe = (kr[:, 0:MLA_ROPE] * cos_ref[...] + kr[:, MLA_ROPE:2 * MLA_ROPE] * sin_ref[...]).astype(k_ref.dtype)
    for h in range(MLA_HEADS):
        k_ref[0, h, :, 0:MLA_NOPE] = kv[:, h * MLA_NOPE:(h + 1) * MLA_NOPE].astype(k_ref.dtype)
        k_ref[0, h, :, MLA_NOPE:MLA_QK_DIM] = pe
        v0 = MLA_HEADS * MLA_NOPE + h * MLA_V
        v_ref[0, h] = kv[:, v0:v0 + MLA_V].astype(v_ref.dtype)


def _mla_kv_prep(zs, g_ckv, w_kv, cos_k, sin_k, batch, seq, off_c, off_kr):
    tm = _pick(seq, (512, 256, 128))
    ns = seq // tm
    nkv = MLA_HEADS * (MLA_NOPE + MLA_V)
    return pl.pallas_call(
        _mla_kv_kernel,
        name="mla_kv",
        grid=(batch * ns,),
        in_specs=[
            pl.BlockSpec((tm, KV_RANK), lambda i: (i, off_c // KV_RANK)),
            pl.BlockSpec((tm, 2 * MLA_ROPE), lambda i: (i, off_kr // (2 * MLA_ROPE))),
            pl.BlockSpec((1, KV_RANK), lambda i: (0, 0)),
            pl.BlockSpec((KV_RANK, nkv), lambda i: (0, 0)),
            pl.BlockSpec((tm, MLA_ROPE), lambda i: (i % ns, 0)),
            pl.BlockSpec((tm, MLA_ROPE), lambda i: (i % ns, 0)),
        ],
        out_specs=[
            pl.BlockSpec((1, MLA_HEADS, tm, MLA_QK_DIM), lambda i: (i // ns, 0, i % ns, 0)),
            pl.BlockSpec((1, MLA_HEADS, tm, MLA_V), lambda i: (i // ns, 0, i % ns, 0)),
        ],
        out_shape=[jax.ShapeDtypeStruct((batch, MLA_HEADS, seq, MLA_QK_DIM), BF16),
                   jax.ShapeDtypeStruct((batch, MLA_HEADS, seq, MLA_V), BF16)],
        compiler_params=_cparams(1),
    )(zs, zs, g_ckv, w_kv, cos_k, sin_k)


def _mla_flash_kernel(q_ref, k_ref, v_ref, o_ref, s_ref, m_ref, l_ref, acc_ref, *, t):
    qi = pl.program_id(2)
    q = q_ref[0, 0]
    m_ref[...] = jnp.full(m_ref.shape, NEG, F32)
    l_ref[...] = jnp.zeros(l_ref.shape, F32)
    acc_ref[...] = jnp.zeros(acc_ref.shape, F32)

    def scores(j, slot):
        off = pl.multiple_of(j * t, t)
        s_ref[slot] = lax.dot_general(q, k_ref[0, 0, pl.ds(off, t), :], (((1,), (1,)), ((), ())),
                                      preferred_element_type=F32)

    def update(j, slot, diagonal):
        off = pl.multiple_of(j * t, t)
        s = s_ref[slot]
        if diagonal:
            row = lax.broadcasted_iota(jnp.int32, (t, t), 0)
            col = lax.broadcasted_iota(jnp.int32, (t, t), 1)
            s = jnp.where(col <= row, s, NEG)
        m_old = m_ref[...]
        m_new = jnp.maximum(m_old, _row_max(s))
        p = jnp.exp2(s - m_new)
        alpha = jnp.exp2(m_old - m_new)
        l_ref[...] = alpha * l_ref[...] + _row_sum(p)
        acc_ref[...] = alpha * acc_ref[...] + jnp.dot(p.astype(BF16), v_ref[0, 0, pl.ds(off, t), :],
                                                      preferred_element_type=F32)
        m_ref[...] = m_new

    scores(0, 0)
    n_pairs = qi // 2

    def body(p, carry):
        j = 2 * p
        scores(j + 1, 1)
        update(j, 0, False)
        scores(j + 2, 0)
        update(j + 1, 1, False)
        return carry

    lax.fori_loop(0, n_pairs, body, 0)
    j_tail = 2 * n_pairs

    @pl.when(qi - j_tail == 1)
    def _():
        scores(j_tail + 1, 1)
        update(j_tail, 0, False)
        update(j_tail + 1, 1, True)

    @pl.when(qi == j_tail)
    def _():
        update(j_tail, 0, True)

    o_ref[0] = (acc_ref[...] / l_ref[...]).astype(o_ref.dtype)


def _mla_flash(q_cat, k_cat, v, batch, seq):
    t = min(MLA_T, seq)
    n = seq // t
    kern = functools.partial(_mla_flash_kernel, t=t)
    return pl.pallas_call(
        kern,
        name="mla_flash",
        grid=(batch, MLA_HEADS, n),
        in_specs=[
            pl.BlockSpec((1, 1, t, MLA_QK_DIM), lambda b, h, qi: (b, h, qi, 0)),
            pl.BlockSpec((1, 1, seq, MLA_QK_DIM), lambda b, h, qi: (b, h, 0, 0)),
            pl.BlockSpec((1, 1, seq, MLA_V), lambda b, h, qi: (b, h, 0, 0)),
        ],
        out_specs=pl.BlockSpec((1, t, MLA_V), lambda b, h, qi: (b, qi, h)),
        out_shape=jax.ShapeDtypeStruct((batch, seq, MLA_OUT_DIM), BF16),
        scratch_shapes=[pltpu.VMEM((2, t, t), F32), pltpu.VMEM((t, 1), F32), pltpu.VMEM((t, 1), F32),
                        pltpu.VMEM((t, MLA_V), F32)],
        compiler_params=_cparams(3),
    )(q_cat, k_cat, v)


def _merge_kernel(oa_ref, ob_ref, wa_ref, wb_ref, ga_ref, gb_ref, o_ref):
    pa = jnp.dot(oa_ref[...], wa_ref[...], preferred_element_type=F32)
    pb = jnp.dot(ob_ref[...], wb_ref[...], preferred_element_type=F32)
    o_ref[...] = (jax.nn.sigmoid(ga_ref[...]) * pa + jax.nn.sigmoid(gb_ref[...]) * pb).astype(o_ref.dtype)


def _gated_merge(o_a, o_b, w_pa, w_pb, gates):
    m = o_a.shape[0]
    d = w_pa.shape[1]
    tm = _pick(m, (512, 256, 128))
    tn = _pick(d, (1024, 512, 256, 128))
    nn = d // tn
    return pl.pallas_call(
        _merge_kernel,
        name="gated_merge",
        grid=(nn, m // tm),
        in_specs=[
            pl.BlockSpec((tm, o_a.shape[1]), lambda j, i: (i, 0)),
            pl.BlockSpec((tm, o_b.shape[1]), lambda j, i: (i, 0)),
            pl.BlockSpec((w_pa.shape[0], tn), lambda j, i: (0, j)),
            pl.BlockSpec((w_pb.shape[0], tn), lambda j, i: (0, j)),
            pl.BlockSpec((tm, tn), lambda j, i: (i, j)),
            pl.BlockSpec((tm, tn), lambda j, i: (i, j + nn)),
        ],
        out_specs=pl.BlockSpec((tm, tn), lambda j, i: (i, j)),
        out_shape=jax.ShapeDtypeStruct((m, d), BF16),
        compiler_params=_cparams(2),
    )(o_a, o_b, w_pa, w_pb, gates, gates)


def _layer_norm(y, g, b):
    mu = jnp.mean(y, axis=-1, keepdims=True)
    dlt = y - mu
    var = jnp.mean(dlt * dlt, axis=-1, keepdims=True)
    return dlt * lax.rsqrt(var + LN_EPS) * g + b


def _proj_ln_kernel(a_ref, w_ref, x_ref, g_ref, b_ref, o_ref, *, alpha):
    y = alpha * x_ref[...] + jnp.dot(a_ref[...], w_ref[...], preferred_element_type=F32)
    o_ref[...] = _layer_norm(y, g_ref[...], b_ref[...])


def _proj_ln(a, w, x, g, b, alpha):
    m, d = x.shape
    tm = _pick(m, (256, 128))
    return pl.pallas_call(
        functools.partial(_proj_ln_kernel, alpha=alpha),
        name="proj_ln",
        grid=(m // tm,),
        in_specs=[
            pl.BlockSpec((tm, a.shape[1]), lambda i: (i, 0)),
            pl.BlockSpec(w.shape, lambda i: (0, 0)),
            pl.BlockSpec((tm, d), lambda i: (i, 0)),
            pl.BlockSpec((1, d), lambda i: (0, 0)),
            pl.BlockSpec((1, d), lambda i: (0, 0)),
        ],
        out_specs=pl.BlockSpec((tm, d), lambda i: (i, 0)),
        out_shape=jax.ShapeDtypeStruct((m, d), F32),
        compiler_params=_cparams(1),
    )(a, w, x, g, b)


def _router_kernel(h_ref, whi_ref, wlo_ref, b_ref, ids_ref, wts_ref):
    h = h_ref[...]
    h_hi = h.astype(BF16)
    h_lo = (h - h_hi.astype(F32)).astype(BF16)
    logits = (jnp.dot(h_hi, whi_ref[...], preferred_element_type=F32)
              + jnp.dot(h_hi, wlo_ref[...], preferred_element_type=F32)
              + jnp.dot(h_lo, whi_ref[...], preferred_element_type=F32)) + b_ref[...]
    tm = h.shape[0]
    lane = lax.broadcasted_iota(jnp.int32, (tm, LANES), 1)
    lane_f = lane.astype(F32)
    big = float(4 * LANES)
    ninf = -jnp.inf
    gl = jnp.where(lane < N_GROUPS, logits, ninf)
    ge = jnp.exp(gl - jnp.max(gl, axis=1, keepdims=True))
    gprob = ge / jnp.sum(ge, axis=1, keepdims=True)
    gw = jnp.max(gprob, axis=1, keepdims=True)
    gidx = jnp.min(jnp.where(gprob == gw, lane_f, big), axis=1, keepdims=True).astype(jnp.int32)
    in_group = lax.shift_right_arithmetic(lane - N_GROUPS, 3) == gidx
    el = jnp.where(in_group, logits, ninf)
    ee = jnp.exp(el - jnp.max(el, axis=1, keepdims=True))
    ep = jnp.where(in_group, ee / jnp.sum(ee, axis=1, keepdims=True), -1.0)
    p1 = jnp.max(ep, axis=1, keepdims=True)
    i1 = jnp.min(jnp.where(ep == p1, lane_f, big), axis=1, keepdims=True).astype(jnp.int32)
    ep2 = jnp.where(lane == i1, -1.0, ep)
    p2 = jnp.max(ep2, axis=1, keepdims=True)
    i2 = jnp.min(jnp.where(ep2 == p2, lane_f, big), axis=1, keepdims=True).astype(jnp.int32)
    den = p1 + p2
    w1 = gw * (p1 / den)
    w2 = gw * (p2 / den)
    ids_ref[...] = jnp.where(lane == 0, i1 - N_GROUPS, jnp.where(lane == 1, i2 - N_GROUPS, 0))
    wts_ref[...] = jnp.where(lane == 0, w1, jnp.where(lane == 1, w2, 0.0))


def _router(h, w_hi, w_lo, b_r):
    m, d = h.shape
    tm = _pick(m, (512, 256, 128))
    return pl.pallas_call(
        _router_kernel,
        name="router",
        grid=(m // tm,),
        in_specs=[
            pl.BlockSpec((tm, d), lambda i: (i, 0)),
            pl.BlockSpec((d, LANES), lambda i: (0, 0)),
            pl.BlockSpec((d, LANES), lambda i: (0, 0)),
            pl.BlockSpec((1, LANES), lambda i: (0, 0)),
        ],
        out_specs=[pl.BlockSpec((tm, LANES), lambda i: (i, 0)),
                   pl.BlockSpec((tm, LANES), lambda i: (i, 0))],
        out_shape=[jax.ShapeDtypeStruct((m, LANES), jnp.int32),
                   jax.ShapeDtypeStruct((m, LANES), F32)],
        compiler_params=_cparams(1),
    )(h, w_hi, w_lo, b_r)


def _row_copy(src_hbm, src_row, dst_vmem, dst_row, sem):
    return pltpu.make_async_copy(src_hbm.at[pl.ds(src_row, 1)], dst_vmem.at[pl.ds(dst_row, 1)], sem)


def _expert_kernel(ie_ref, row0_ref, nsub_ref, skip_ref, nrows_ref, tok_ref,
                   slab_hbm, wg_ref, wu_ref, wd_ref, y_hbm,
                   slab_buf, x_ref, g_ref, u_ref, a_ref, o_buf, zbuf, pend_ref, gsem, osem, zsem, *, d, nck):
    it = pl.program_id(0)
    j = pl.program_id(1)
    nsub = nsub_ref[it]
    spt = d // LANES
    ck = d // nck
    sub_rows = MOE_BLK * spt
    n_fill = (y_hbm.shape[0] - nrows_ref[0]) // MOE_BLK

    def fill_copy(b):
        row = pl.multiple_of(nrows_ref[0] + b * MOE_BLK, MOE_BLK)
        return pltpu.make_async_copy(zbuf, y_hbm.at[pl.ds(row, MOE_BLK)], zsem)

    def issue_gather(item):
        for sb in range(ITEM_SUB):
            @pl.when(sb < nsub_ref[item])
            def _(sb=sb):
                base = row0_ref[item] + sb * MOE_BLK

                def start(r, c):
                    src = pl.multiple_of(tok_ref[base + r] * spt, spt)
                    dst = pl.multiple_of((sb * MOE_BLK + r) * spt, spt)
                    pltpu.make_async_copy(slab_hbm.at[pl.ds(src, spt)], slab_buf.at[pl.ds(dst, spt)],
                                          gsem.at[sb]).start()
                    return c
                lax.fori_loop(0, MOE_BLK, start, 0, unroll=8)

    @pl.when(j == 0)
    def _():
        @pl.when(it == 0)
        def _():
            pend_ref[0] = 0
            pend_ref[1] = 0
            issue_gather(0)
            zbuf[...] = jnp.zeros(zbuf.shape, F32)
            lax.fori_loop(0, n_fill, lambda b, c: (fill_copy(b).start(), c)[1], 0)

        for sb in range(ITEM_SUB):
            @pl.when(sb < nsub)
            def _(sb=sb):
                pltpu.make_async_copy(slab_hbm.at[pl.ds(0, sub_rows)],
                                      slab_buf.at[pl.ds(sb * sub_rows, sub_rows)], gsem.at[sb]).wait()
                for s in range(spt):
                    col = s * LANES
                    x_ref[col // ck, sb * MOE_BLK:(sb + 1) * MOE_BLK, col % ck:col % ck + LANES] = (
                        slab_buf[pl.ds(sb * sub_rows + s, MOE_BLK, stride=spt), :].astype(BF16))

        @pl.when(it + 1 < pl.num_programs(0))
        def _():
            issue_gather(it + 1)

        g_ref[...] = jnp.zeros(g_ref.shape, F32)
        u_ref[...] = jnp.zeros(u_ref.shape, F32)

    for n in range(1, ITEM_SUB + 1):
        rows = n * MOE_BLK

        @pl.when((nsub == n) & (j < nck))
        def _(rows=rows):
            xk = x_ref[j, 0:rows, :]
            g_ref[0:rows] += jnp.dot(xk, wg_ref[...].astype(BF16), preferred_element_type=F32)
            u_ref[0:rows] += jnp.dot(xk, wu_ref[...].astype(BF16), preferred_element_type=F32)

        @pl.when((nsub == n) & (j == nck))
        def _(rows=rows):
            g = g_ref[0:rows]
            a_ref[0:rows] = (g * jax.nn.sigmoid(g) * u_ref[0:rows]).astype(BF16)

    @pl.when(j >= nck)
    def _():
        slot = j % 2
        other = 1 - slot
        col0 = pl.multiple_of((j - nck) * ck, ck)
        for n in range(1, ITEM_SUB + 1):
            rows = n * MOE_BLK

            @pl.when(nsub == n)
            def _(rows=rows):
                o_buf[slot, 0:rows] = jnp.dot(a_ref[0:rows], wd_ref[...].astype(BF16), preferred_element_type=F32)
                pltpu.make_async_copy(o_buf.at[slot, pl.ds(0, rows)],
                                      y_hbm.at[pl.ds(pl.multiple_of(row0_ref[it], MOE_BLK), rows), pl.ds(col0, ck)],
                                      osem.at[slot]).start()

        def drain(s):
            for n in range(1, ITEM_SUB + 1):
                rows = n * MOE_BLK

                @pl.when(pend_ref[s] == n)
                def _(rows=rows):
                    pltpu.make_async_copy(o_buf.at[s, pl.ds(0, rows)],
                                          y_hbm.at[pl.ds(0, rows), pl.ds(0, ck)], osem.at[s]).wait()
            pend_ref[s] = 0

        drain(other)
        pend_ref[slot] = nsub

        @pl.when((it == pl.num_programs(0) - 1) & (j == pl.num_programs(1) - 1))
        def _():
            drain(slot)
            lax.fori_loop(0, n_fill, lambda b, c: (fill_copy(b).wait(), c)[1], 0)


def _experts(tables, buf_tok, slabs, wg, wu, wd, cap):
    item_expert, item_row0, item_nsub, item_skip, n_rows_used = tables
    n_items = item_expert.shape[0]
    n_exp, d, ff = wg.shape
    nck = MOE_CHUNKS
    ck = d // nck
    rows = ITEM_SUB * MOE_BLK
    spt = d // LANES

    def w_in_map(it, j, ie, r0, ns, sk, nr, tok):
        return (ie[it], jnp.where(sk[it] == 1, nck - 1, jnp.minimum(j, nck - 1)), 0)

    def w_out_map(it, j, ie, r0, ns, sk, nr, tok):
        return (ie[it], 0, jnp.where(sk[it] == 1, nck - 1, jnp.maximum(j - nck, 0)))

    return pl.pallas_call(
        functools.partial(_expert_kernel, d=d, nck=nck),
        name="moe_experts",
        grid_spec=pltpu.PrefetchScalarGridSpec(
            num_scalar_prefetch=6,
            grid=(n_items, 2 * nck),
            in_specs=[
                pl.BlockSpec(memory_space=pl.ANY),
                pl.BlockSpec((None, ck, ff), w_in_map),
                pl.BlockSpec((None, ck, ff), w_in_map),
                pl.BlockSpec((None, ff, ck), w_out_map),
            ],
            out_specs=pl.BlockSpec(memory_space=pl.ANY),
            scratch_shapes=[
                pltpu.VMEM((rows * spt, LANES), F32),
                pltpu.VMEM((nck, rows, ck), BF16),
                pltpu.VMEM((rows, ff), F32),
                pltpu.VMEM((rows, ff), F32),
                pltpu.VMEM((rows, ff), BF16),
                pltpu.VMEM((2, rows, ck), F32),
                pltpu.VMEM((MOE_BLK, d), F32),
                pltpu.SMEM((2,), jnp.int32),
                pltpu.SemaphoreType.DMA((ITEM_SUB,)),
                pltpu.SemaphoreType.DMA((2,)),
                pltpu.SemaphoreType.DMA(()),
            ],
        ),
        out_shape=jax.ShapeDtypeStruct((cap, d), F32),
        compiler_params=_cparams(2, vmem_mb=EXPERT_VMEM_MB),
    )(item_expert, item_row0, item_nsub, item_skip, n_rows_used, buf_tok, slabs, wg, wu, wd)


def _combine_kernel(dest_ref, y_hbm, h_ref, wts_ref, g_ref, b_ref, o_ref, buf, sem, *, tm, alpha):
    i = pl.program_id(0)
    slot = i % 2

    def issue(tile, dst_slot):
        def start(r, c):
            a = 2 * (tile * tm + r)
            _row_copy(y_hbm, dest_ref[a], buf.at[dst_slot, 0], r, sem.at[dst_slot]).start()
            _row_copy(y_hbm, dest_ref[a + 1], buf.at[dst_slot, 1], r, sem.at[dst_slot]).start()
            return c
        lax.fori_loop(0, tm, start, 0, unroll=8)

    @pl.when(i == 0)
    def _():
        issue(0, 0)

    @pl.when(i + 1 < pl.num_programs(0))
    def _():
        issue(i + 1, 1 - slot)

    for k in range(2):
        pltpu.make_async_copy(y_hbm.at[pl.ds(0, tm)], buf.at[slot, k], sem.at[slot]).wait()
    wts = wts_ref[...]
    f = buf[slot, 0] * wts[:, 0:1] + buf[slot, 1] * wts[:, 1:2]
    o_ref[...] = _layer_norm(alpha * h_ref[...] + f, g_ref[...], b_ref[...])


def _combine_ln(dest, yb, h, wts, g, b, alpha):
    m, d = h.shape
    tm = _pick(m, (256, 128))
    return pl.pallas_call(
        functools.partial(_combine_kernel, tm=tm, alpha=alpha),
        name="moe_combine",
        grid_spec=pltpu.PrefetchScalarGridSpec(
            num_scalar_prefetch=1,
            grid=(m // tm,),
            in_specs=[
                pl.BlockSpec(memory_space=pl.ANY),
                pl.BlockSpec((tm, d), lambda i, dst: (i, 0)),
                pl.BlockSpec((tm, LANES), lambda i, dst: (i, 0)),
                pl.BlockSpec((1, d), lambda i, dst: (0, 0)),
                pl.BlockSpec((1, d), lambda i, dst: (0, 0)),
            ],
            out_specs=pl.BlockSpec((tm, d), lambda i, dst: (i, 0)),
            scratch_shapes=[pltpu.VMEM((2, 2, tm, d), F32), pltpu.SemaphoreType.DMA((2,))],
        ),
        out_shape=jax.ShapeDtypeStruct((m, d), F32),
        compiler_params=_cparams(1),
    )(dest, yb, h, wts, g, b)


def _dispatch_tables(ids, n_tok):
    flat_e = ids[:, :2].reshape(-1)
    n_assign = flat_e.shape[0]
    onehot = (flat_e[:, None] == jnp.arange(N_EXPERTS, dtype=jnp.int32)[None, :]).astype(jnp.int32)
    csum = jnp.cumsum(onehot, axis=0)
    rank = jnp.sum(onehot * csum, axis=1) - 1
    counts = csum[-1]
    padded = (counts + MOE_BLK - 1) // MOE_BLK * MOE_BLK
    pend = jnp.cumsum(padded)
    pstart = pend - padded
    dest = (pstart[flat_e] + rank).astype(jnp.int32)
    cap = -(-n_assign // MOE_BLK) * MOE_BLK + N_EXPERTS * MOE_BLK
    buf_tok = jnp.zeros((cap,), jnp.int32).at[dest].set(jnp.arange(n_assign, dtype=jnp.int32) // 2)

    item_rows = ITEM_SUB * MOE_BLK
    items_e = (padded + item_rows - 1) // item_rows
    item_end = jnp.cumsum(items_e)
    item_start = item_end - items_e
    n_used = item_end[-1]
    n_items = -(-N_EXPERTS * (ITEM_SUB - 1) // ITEM_SUB) + cap // item_rows + 1
    it = jnp.arange(n_items, dtype=jnp.int32)
    e_it = jnp.minimum(jnp.sum((item_end[None, :] <= it[:, None]).astype(jnp.int32), axis=1), N_EXPERTS - 1)
    k_it = it - item_start[e_it]
    used = it < n_used
    e_last = e_it[jnp.maximum(n_used - 1, 0)]
    item_expert = jnp.where(used, e_it, e_last).astype(jnp.int32)
    item_row0 = jnp.where(used, pstart[e_it] + k_it * item_rows, 0).astype(jnp.int32)
    item_nsub = jnp.where(used, jnp.clip((padded[e_it] - k_it * item_rows) // MOE_BLK, 0, ITEM_SUB), 0)
    item_skip = jnp.where(used, 0, 1).astype(jnp.int32)
    n_rows_used = pend[-1].astype(jnp.int32).reshape(1)
    return dest, buf_tok, (item_expert, item_row0, item_nsub.astype(jnp.int32), item_skip, n_rows_used), cap


def _rope_tables(seq):
    half = MLA_ROPE // 2
    freqs = ROPE_THETA ** (-jnp.arange(half, dtype=F32) / half)
    ang = jnp.arange(seq, dtype=F32)[:, None] * freqs
    cos = jnp.cos(ang)
    sin = jnp.sin(ang)
    return jnp.concatenate([cos, cos], -1), jnp.concatenate([sin, sin], -1)


def _swap_halves_neg(w):
    half = w.shape[-1] // 2
    return jnp.concatenate([-w[..., half:], w[..., :half]], axis=-1)


def _layer(x, w_in, w_ukv, g_ckv, w_pa, w_pb, w_o, ln1_g, ln1_b, w_rg, b_rg, w_re, b_re,
           w_gate_e, w_up_e, w_down_e, ln2_g, ln2_b, alpha):
    batch, seq, d = x.shape
    n_tok = batch * seq
    x2 = x.reshape(n_tok, d)
    xb16 = x2.astype(BF16)

    splits = (DSA_Q_DIM, DSA_KV_DIM, DSA_KV_DIM, IDX_Q_DIM, IDX_DIM, IDX_HEADS, MLA_Q_DIM, KV_RANK, MLA_ROPE, 2 * d)
    offs = np.cumsum((0,) + splits)
    w_dq, w_dk, w_dv, w_iq, w_ik, w_iw, w_mq, w_ckv, w_kr, w_gates = [
        w_in[:, offs[j]:offs[j + 1]] for j in range(len(splits))]
    w_mq3 = w_mq.reshape(d, MLA_HEADS, MLA_QK_DIM)
    w_qn = w_mq3[:, :, :MLA_NOPE].reshape(d, MLA_HEADS * MLA_NOPE)
    w_qr = w_mq3[:, :, MLA_NOPE:]
    w_qs = _swap_halves_neg(w_qr)
    wr = MLA_HEADS * MLA_ROPE
    wb = jnp.concatenate([w_iq, w_dk, w_dv], axis=1).astype(BF16)
    off_k = IDX_Q_DIM
    off_v = off_k + DSA_KV_DIM
    pad_i = jnp.zeros((d, LANES - IDX_DIM - IDX_HEADS), F32)
    ws = jnp.concatenate([w_dq, w_qn, w_qr.reshape(d, wr), w_qs.reshape(d, wr), w_ckv, w_ik, w_iw, pad_i,
                          w_kr, _swap_halves_neg(w_kr)], axis=1).astype(BF16)
    off_qn = DSA_Q_DIM
    off_qr = off_qn + MLA_HEADS * MLA_NOPE
    off_qs = off_qr + wr
    off_c = off_qs + wr
    off_i = off_c + KV_RANK
    off_kr = off_i + LANES

    zb = _matmul(xb16, wb, BF16, tn=wb.shape[1])
    zs = _matmul(xb16, ws, F32, tn=ws.shape[1] // 2)
    gates = _matmul(xb16, w_gates.astype(BF16), F32, tn=_pick(2 * d, (2048, 1024, 512, 256)))

    iq_hm = zb[:, :IDX_Q_DIM].reshape(batch, seq, IDX_HEADS, IDX_DIM).transpose(0, 2, 1, 3)
    kidx = zs[:, off_i:off_i + IDX_DIM].astype(BF16).reshape(batch, seq, IDX_DIM)
    w_idx = zs[:, off_i + IDX_DIM:off_i + IDX_DIM + IDX_HEADS].reshape(batch, seq, IDX_HEADS)
    o_a = _dsa_attention(zs, zb, iq_hm, w_idx, kidx, batch, seq, off_k, off_v)

    cos2, sin2 = _rope_tables(seq)
    w_ukv3 = w_ukv.reshape(KV_RANK, MLA_HEADS, MLA_NOPE + MLA_V)
    w_kv = jnp.concatenate([w_ukv3[:, :, :MLA_NOPE].reshape(KV_RANK, MLA_HEADS * MLA_NOPE),
                            w_ukv3[:, :, MLA_NOPE:].reshape(KV_RANK, MLA_HEADS * MLA_V)], axis=1).astype(BF16)
    q_cat = _mla_q_prep(zs, jnp.tile(cos2, (1, MLA_HEADS)), jnp.tile(sin2, (1, MLA_HEADS)),
                        batch, seq, off_qn, off_qr, off_qs)
    k_cat, v_mla = _mla_kv_prep(zs, g_ckv.reshape(1, KV_RANK), w_kv, cos2, sin2, batch, seq, off_c, off_kr)
    o_b = _mla_flash(q_cat, k_cat, v_mla, batch, seq).reshape(n_tok, MLA_OUT_DIM)

    merged = _gated_merge(o_a, o_b, w_pa.astype(BF16), w_pb.astype(BF16), gates)
    h = _proj_ln(merged, w_o.astype(BF16), x2, ln1_g.reshape(1, d), ln1_b.reshape(1, d), alpha)

    pad_r = jnp.zeros((d, LANES - N_GROUPS - N_EXPERTS), F32)
    w_r = jnp.concatenate([w_rg, w_re, pad_r], axis=1)
    w_r_hi = w_r.astype(BF16)
    w_r_lo = (w_r - w_r_hi.astype(F32)).astype(BF16)
    b_r = jnp.concatenate([b_rg, b_re, jnp.zeros((LANES - N_GROUPS - N_EXPERTS,), F32)]).reshape(1, LANES)
    ids, wts = _router(h, w_r_hi, w_r_lo, b_r)
    dest, buf_tok, item_tables, cap = _dispatch_tables(ids, n_tok)
    slabs = h.reshape(n_tok * (d // LANES), LANES)
    yb = _experts(item_tables, buf_tok, slabs, w_gate_e, w_up_e, w_down_e, cap)
    out = _combine_ln(dest, yb, h, wts, ln2_g.reshape(1, d), ln2_b.reshape(1, d), alpha)
    return out.reshape(batch, seq, d)


def kernel(x, w_in, w_ukv, g_ckv, w_pa, w_pb, w_o, ln1_g, ln1_b, w_rg, b_rg, w_re, b_re,
           w_gate_e, w_up_e, w_down_e, ln2_g, ln2_b):
    depth = w_in.shape[0]
    alpha = (2.0 * depth) ** 0.25
    for l in range(depth):
        x = _layer(x, w_in[l], w_ukv[l], g_ckv[l], w_pa[l], w_pb[l], w_o[l], ln1_g[l], ln1_b[l],
                   w_rg[l], b_rg[l], w_re[l], b_re[l], w_gate_e[l], w_up_e[l], w_down_e[l],
                   ln2_g[l], ln2_b[l], alpha)
    return x
```

```python
import functools

import numpy as np
import jax
import jax.numpy as jnp
from jax import lax
from jax.experimental import pallas as pl
from jax.experimental.pallas import tpu as pltpu

N_DSA_HEADS = 8
DSA_HEAD_DIM = 128
DSA_KV_HEADS = 2
DSA_REP = N_DSA_HEADS // DSA_KV_HEADS
IDX_HEADS = 16
IDX_DIM = 64
TOPK_MAX = 256
MLA_HEADS = 8
MLA_NOPE = 128
MLA_ROPE = 64
MLA_V = 128
KV_RANK = 512
ROPE_THETA = 10000.0
N_GROUPS = 8
EXPERTS_PER_GROUP = 8
N_EXPERTS = N_GROUPS * EXPERTS_PER_GROUP
LN_EPS = 1e-5
RMS_EPS = 1e-6

DSA_Q_DIM = N_DSA_HEADS * DSA_HEAD_DIM
DSA_KV_DIM = DSA_KV_HEADS * DSA_HEAD_DIM
IDX_Q_DIM = IDX_HEADS * IDX_DIM
MLA_QK_DIM = MLA_NOPE + MLA_ROPE
MLA_Q_DIM = MLA_HEADS * MLA_QK_DIM
MLA_OUT_DIM = MLA_HEADS * MLA_V

LANES = 128
VMEM_LIMIT_MB = 56

DSA_TQ = 128
DSA_KC = 512
MLA_T = 512
MOE_BLK = 256
ITEM_SUB = 3
MOE_CHUNKS = 4
EXPERT_VMEM_MB = 58
NEG = -1e30
LOG2E = 1.4426950408889634
INT_MIN = -2 ** 31
FLT_MAX = 3.4028234663852886e38

F32 = jnp.float32
BF16 = jnp.bfloat16


def _cparams(n_axes, vmem_mb=VMEM_LIMIT_MB):
    return pltpu.CompilerParams(dimension_semantics=("arbitrary",) * n_axes,
                                vmem_limit_bytes=vmem_mb * 1024 * 1024)


def _pick(n, prefs):
    for p in prefs:
        if n % p == 0:
            return p
    return n


def _lane_tiles(x):
    return [x[:, j * LANES:(j + 1) * LANES] for j in range(x.shape[1] // LANES)]


def _row_max(x):
    return jnp.max(functools.reduce(jnp.maximum, _lane_tiles(x)), axis=1, keepdims=True)


def _row_sum(x):
    return jnp.sum(functools.reduce(jnp.add, _lane_tiles(x)), axis=1, keepdims=True)


def _mm_kernel(a_ref, w_ref, o_ref):
    o_ref[...] = jnp.dot(a_ref[...], w_ref[...], preferred_element_type=F32).astype(o_ref.dtype)


def _matmul(a, w, out_dtype, tn):
    m, k = a.shape
    n = w.shape[1]
    tm = _pick(m, (512, 256, 128))
    return pl.pallas_call(
        _mm_kernel,
        grid=(n // tn, m // tm),
        in_specs=[pl.BlockSpec((tm, k), lambda j, i: (i, 0)),
                  pl.BlockSpec((k, tn), lambda j, i: (0, j))],
        out_specs=pl.BlockSpec((tm, tn), lambda j, i: (i, j)),
        out_shape=jax.ShapeDtypeStruct((m, n), out_dtype),
        compiler_params=_cparams(2),
    )(a, w)


def _dsa_kernel(q_ref, iq_ref, w_ref, kidx_ref, k_ref, v_ref, o_ref,
                sc_ref, thr_ref, jcut_ref, s_ref, m_ref, l_ref, acc_ref, *, tq, kc, n_sel, seq):
    i = pl.program_id(1)
    nch = (i * tq + tq + kc - 1) // kc
    row = lax.broadcasted_iota(jnp.int32, (tq, kc), 0)
    lane = lax.broadcasted_iota(jnp.int32, (tq, kc), 1)
    qpos = i * tq + row

    iq_all = iq_ref[...]
    iq = jnp.concatenate([iq_all[:, h * IDX_DIM:(h + 1) * IDX_DIM] for h in range(IDX_HEADS)],
                         axis=0)
    w = w_ref[0] * (IDX_DIM ** -0.5 * IDX_HEADS ** -0.5)

    def score_body(c, carry):
        off = pl.multiple_of(c * kc, kc)
        kx = kidx_ref[0, pl.ds(off, kc), :]
        logits = lax.dot_general(iq, kx, (((1,), (1,)), ((), ())), preferred_element_type=F32)
        sc = jnp.zeros((tq, kc), F32)
        for h in range(IDX_HEADS):
            sc = sc + jnp.maximum(logits[h * tq:(h + 1) * tq], 0.0) * w[:, h:h + 1]
        sc_ref[c] = jnp.where(off + lane <= qpos, sc, -jnp.inf)
        return carry

    lax.fori_loop(0, nch, score_body, 0)

    def key_to_f32(key):
        return pltpu.bitcast(jnp.where(key < 0, key ^ jnp.int32(0x7FFFFFFF), key), F32)

    key_lowest = INT_MIN + 0x00800000
    thr_ref[...] = jnp.full((tq, 1), -FLT_MAX, F32)
    jcut_ref[...] = jnp.full((tq, 1), seq, jnp.int32)

    def count(pred_fn):
        def body(c, acc):
            sch = sc_ref[c]
            for j in range(kc // LANES):
                acc = acc + pred_fn(sch[:, j * LANES:(j + 1) * LANES], c * kc + j * LANES)
            return acc
        acc = lax.fori_loop(0, nch, body, jnp.zeros((tq, LANES), jnp.int32))
        return jnp.sum(acc.astype(F32), axis=1, keepdims=True).astype(jnp.int32)

    lane1 = lax.broadcasted_iota(jnp.int32, (tq, LANES), 1)

    @pl.when(i * tq + tq > n_sel)
    def _():
        def bound_body(c, carry):
            tiles = _lane_tiles(sc_ref[c])
            mx = list(carry)
            for jt, tile in enumerate(tiles):
                mx[jt % 2] = jnp.maximum(mx[jt % 2], tile)
            return tuple(mx)

        ninf = jnp.full((tq, LANES), -jnp.inf, F32)
        mx_even, mx_odd = lax.fori_loop(0, nch, bound_body, (ninf, ninf))
        hi = jnp.max(jnp.maximum(mx_even, mx_odd), axis=1, keepdims=True)
        lo = jnp.min(jnp.minimum(mx_even, mx_odd), axis=1, keepdims=True)

        def f32_to_key(x):
            bits = pltpu.bitcast(x, jnp.int32)
            return jnp.where(bits < 0, bits ^ jnp.int32(0x7FFFFFFF), bits)

        k_lo = f32_to_key(lo)
        n_fixed = lax.clz(k_lo ^ f32_to_key(hi))
        fixed_mask = jnp.where(n_fixed == 0, 0,
                               lax.shift_right_arithmetic(jnp.int32(INT_MIN), jnp.maximum(n_fixed - 1, 0)))
        t_start = ((k_lo ^ jnp.int32(INT_MIN)) & fixed_mask) ^ jnp.int32(INT_MIN)
        b_start = jnp.min(n_fixed.astype(F32)).astype(jnp.int32)

        def bit_body(b, t):
            bit = lax.shift_left(jnp.int32(1), 31 - b)
            cand = jnp.where(b >= n_fixed, t + bit, t)
            cand_b = jnp.broadcast_to(key_to_f32(cand), (tq, LANES))
            cnt = count(lambda sch, _: jnp.where(sch >= cand_b, 1, 0))
            return jnp.where(cnt >= n_sel, cand, t)

        t = lax.fori_loop(b_start, 32, bit_body, t_start)
        t_f = key_to_f32(jnp.maximum(t, key_lowest))
        thr_ref[...] = t_f
        t_b = jnp.broadcast_to(t_f, (tq, LANES))
        c_ge = count(lambda sch, _: jnp.where(sch >= t_b, 1, 0))
        c_gt = count(lambda sch, _: jnp.where(sch > t_b, 1, 0))
        excess = c_ge > n_sel
        need = n_sel - c_gt

        @pl.when(jnp.max(jnp.where(excess, 1.0, 0.0)) > 0.0)
        def _():
            nbits = max(1, int(np.ceil(np.log2(seq))))

            def jbit(b, jc):
                cand = jc + lax.shift_left(jnp.int32(1), nbits - 1 - b)
                cand_b = jnp.broadcast_to(cand, (tq, LANES))
                cnt = count(lambda sch, base: jnp.where(
                    sch == t_b, jnp.where(base + lane1 < cand_b, 1, 0), 0))
                return jnp.where(cnt < need, cand, jc)

            jc = lax.fori_loop(0, nbits, jbit, jnp.zeros((tq, 1), jnp.int32))
            jcut_ref[...] = jnp.where(excess, jc, seq)

    qs = (q_ref[...] * (DSA_HEAD_DIM ** -0.5 * LOG2E)).astype(BF16)
    qg = [jnp.concatenate([qs[:, (g * DSA_REP + r) * DSA_HEAD_DIM:(g * DSA_REP + r + 1) * DSA_HEAD_DIM]
                           for r in range(DSA_REP)], axis=0) for g in range(DSA_KV_HEADS)]
    m_ref[...] = jnp.full(m_ref.shape, NEG, F32)
    l_ref[...] = jnp.zeros(l_ref.shape, F32)
    acc_ref[...] = jnp.zeros(acc_ref.shape, F32)
    t = thr_ref[...]
    jc = jcut_ref[...]

    def scores(c, slot):
        off = pl.multiple_of(c * kc, kc)
        for g in range(DSA_KV_HEADS):
            s_ref[slot, g] = lax.dot_general(
                qg[g], k_ref[pl.ds(off, kc), g * DSA_HEAD_DIM:(g + 1) * DSA_HEAD_DIM],
                (((1,), (1,)), ((), ())), preferred_element_type=F32)

    def update(c, slot):
        off = pl.multiple_of(c * kc, kc)
        sc = sc_ref[c]
        kpos = off + lane
        sel = jnp.where(kpos <= jc, jnp.where(sc >= t, 1, 0), jnp.where(sc > t, 1, 0)) > 0
        md = jnp.where(sel, (qpos - kpos).astype(F32), -NEG)
        for g in range(DSA_KV_HEADS):
            s = s_ref[slot, g]
            vg = v_ref[pl.ds(off, kc), g * DSA_HEAD_DIM:(g + 1) * DSA_HEAD_DIM]
            e = jnp.concatenate(
                [s[r * tq:(r + 1) * tq] - md * (LOG2E * 2.0 ** (-8.0 * (g * DSA_REP + r + 1) / N_DSA_HEADS))
                 for r in range(DSA_REP)], axis=0)
            m_old = m_ref[g]
            m_new = jnp.maximum(m_old, _row_max(e))
            p = jnp.exp2(e - m_new)
            alpha = jnp.exp2(m_old - m_new)
            l_ref[g] = alpha * l_ref[g] + _row_sum(p)
            acc_ref[g] = alpha * acc_ref[g] + jnp.dot(p.astype(BF16), vg, preferred_element_type=F32)
            m_ref[g] = m_new

    scores(0, 0)
    n_pairs = (nch - 1) // 2

    def att_body(p, carry):
        c = 2 * p
        scores(c + 1, 1)
        update(c, 0)
        scores(c + 2, 0)
        update(c + 1, 1)
        return carry

    lax.fori_loop(0, n_pairs, att_body, 0)
    c_tail = 2 * n_pairs

    @pl.when(nch - c_tail == 2)
    def _():
        scores(c_tail + 1, 1)
        update(c_tail, 0)
        update(c_tail + 1, 1)

    @pl.when(nch - c_tail == 1)
    def _():
        update(c_tail, 0)

    for g in range(DSA_KV_HEADS):
        o = acc_ref[g] / l_ref[g]
        for r in range(DSA_REP):
            hh = g * DSA_REP + r
            o_ref[:, hh * DSA_HEAD_DIM:(hh + 1) * DSA_HEAD_DIM] = o[r * tq:(r + 1) * tq].astype(o_ref.dtype)


def _dsa_attention(zs, zb, w_idx, kidx, batch, seq, off_k, off_v):
    tq = min(DSA_TQ, seq)
    kc = min(DSA_KC, seq)
    nq = seq // tq
    n_sel = min(TOPK_MAX, seq // 4)
    kern = functools.partial(_dsa_kernel, tq=tq, kc=kc, n_sel=n_sel, seq=seq)
    return pl.pallas_call(
        kern,
        name="dsa",
        grid=(batch, nq),
        in_specs=[
            pl.BlockSpec((tq, DSA_Q_DIM), lambda b, i: (b * nq + i, 0)),
            pl.BlockSpec((tq, IDX_Q_DIM), lambda b, i: (b * nq + i, 0)),
            pl.BlockSpec((1, tq, IDX_HEADS), lambda b, i: (b, i, 0)),
            pl.BlockSpec((1, seq, IDX_DIM), lambda b, i: (b, 0, 0)),
            pl.BlockSpec((seq, DSA_KV_DIM), lambda b, i: (b, off_k // DSA_KV_DIM)),
            pl.BlockSpec((seq, DSA_KV_DIM), lambda b, i: (b, off_v // DSA_KV_DIM)),
        ],
        out_specs=pl.BlockSpec((tq, DSA_Q_DIM), lambda b, i: (b * nq + i, 0)),
        out_shape=jax.ShapeDtypeStruct((batch * seq, DSA_Q_DIM), BF16),
        scratch_shapes=[
            pltpu.VMEM((seq // kc, tq, kc), F32),
            pltpu.VMEM((tq, 1), F32),
            pltpu.VMEM((tq, 1), jnp.int32),
            pltpu.VMEM((2, DSA_KV_HEADS, DSA_REP * tq, kc), F32),
            pltpu.VMEM((DSA_KV_HEADS, DSA_REP * tq, 1), F32),
            pltpu.VMEM((DSA_KV_HEADS, DSA_REP * tq, 1), F32),
            pltpu.VMEM((DSA_KV_HEADS, DSA_REP * tq, DSA_HEAD_DIM), F32),
        ],
        compiler_params=_cparams(2),
    )(zs, zb, w_idx, kidx, zb, zb)


def _mla_q_kernel(qn_ref, qr_ref, qs_ref, cos_ref, sin_ref, o_ref):
    c = MLA_QK_DIM ** -0.5 * LOG2E
    pe = ((qr_ref[...] * cos_ref[...] + qs_ref[...] * sin_ref[...]) * c).astype(o_ref.dtype)
    qn = (qn_ref[...] * c).astype(o_ref.dtype)
    for h in range(MLA_HEADS):
        o_ref[0, h, :, 0:MLA_NOPE] = qn[:, h * MLA_NOPE:(h + 1) * MLA_NOPE]
        o_ref[0, h, :, MLA_NOPE:MLA_QK_DIM] = pe[:, h * MLA_ROPE:(h + 1) * MLA_ROPE]


def _mla_q_prep(zs, cos_q, sin_q, batch, seq, off_qn, off_qr, off_qs):
    tm = _pick(seq, (512, 256, 128))
    ns = seq // tm
    wr = MLA_HEADS * MLA_ROPE
    return pl.pallas_call(
        _mla_q_kernel,
        name="mla_q",
        grid=(batch * ns,),
        in_specs=[
            pl.BlockSpec((tm, MLA_HEADS * MLA_NOPE), lambda i: (i, off_qn // (MLA_HEADS * MLA_NOPE))),
            pl.BlockSpec((tm, wr), lambda i: (i, off_qr // wr)),
            pl.BlockSpec((tm, wr), lambda i: (i, off_qs // wr)),
            pl.BlockSpec((tm, wr), lambda i: (i % ns, 0)),
            pl.BlockSpec((tm, wr), lambda i: (i % ns, 0)),
        ],
        out_specs=pl.BlockSpec((1, MLA_HEADS, tm, MLA_QK_DIM), lambda i: (i // ns, 0, i % ns, 0)),
        out_shape=jax.ShapeDtypeStruct((batch, MLA_HEADS, seq, MLA_QK_DIM), BF16),
        compiler_params=_cparams(1),
    )(zs, zs, zs, cos_q, sin_q)


def _mla_kv_kernel(c_ref, kr_ref, g_ref, w_ref, cos_ref, sin_ref, k_ref, v_ref):
    c = c_ref[...]
    cn = c * lax.rsqrt(jnp.mean(c * c, axis=-1, keepdims=True) + RMS_EPS) * g_ref[...]
    kv = jnp.dot(cn.astype(BF16), w_ref[...], preferred_element_type=F32)
    kr = kr_ref[...]
    pe = (kr[:, 0:MLA_ROPE] * cos_ref[...] + kr[:, MLA_ROPE:2 * MLA_ROPE] * sin_ref[...]).astype(k_ref.dtype)
    for h in range(MLA_HEADS):
        k_ref[0, h, :, 0:MLA_NOPE] = kv[:, h * MLA_NOPE:(h + 1) * MLA_NOPE].astype(k_ref.dtype)
        k_ref[0, h, :, MLA_NOPE:MLA_QK_DIM] = pe
        v0 = MLA_HEADS * MLA_NOPE + h * MLA_V
        v_ref[0, h] = kv[:, v0:v0 + MLA_V].astype(v_ref.dtype)


def _mla_kv_prep(zs, g_ckv, w_kv, cos_k, sin_k, batch, seq, off_c, off_kr):
    tm = _pick(seq, (512, 256, 128))
    ns = seq // tm
    nkv = MLA_HEADS * (MLA_NOPE + MLA_V)
    return pl.pallas_call(
        _mla_kv_kernel,
        name="mla_kv",
        grid=(batch * ns,),
        in_specs=[
            pl.BlockSpec((tm, KV_RANK), lambda i: (i, off_c // KV_RANK)),
            pl.BlockSpec((tm, 2 * MLA_ROPE), lambda i: (i, off_kr // (2 * MLA_ROPE))),
            pl.BlockSpec((1, KV_RANK), lambda i: (0, 0)),
            pl.BlockSpec((KV_RANK, nkv), lambda i: (0, 0)),
            pl.BlockSpec((tm, MLA_ROPE), lambda i: (i % ns, 0)),
            pl.BlockSpec((tm, MLA_ROPE), lambda i: (i % ns, 0)),
        ],
        out_specs=[
            pl.BlockSpec((1, MLA_HEADS, tm, MLA_QK_DIM), lambda i: (i // ns, 0, i % ns, 0)),
            pl.BlockSpec((1, MLA_HEADS, tm, MLA_V), lambda i: (i // ns, 0, i % ns, 0)),
        ],
        out_shape=[jax.ShapeDtypeStruct((batch, MLA_HEADS, seq, MLA_QK_DIM), BF16),
                   jax.ShapeDtypeStruct((batch, MLA_HEADS, seq, MLA_V), BF16)],
        compiler_params=_cparams(1),
    )(zs, zs, g_ckv, w_kv, cos_k, sin_k)


def _mla_flash_kernel(q_ref, k_ref, v_ref, o_ref, s_ref, m_ref, l_ref, acc_ref, *, t):
    qi = pl.program_id(2)
    q = q_ref[0, 0]
    m_ref[...] = jnp.full(m_ref.shape, NEG, F32)
    l_ref[...] = jnp.zeros(l_ref.shape, F32)
    acc_ref[...] = jnp.zeros(acc_ref.shape, F32)

    def scores(j, slot):
        off = pl.multiple_of(j * t, t)
        s_ref[slot] = lax.dot_general(q, k_ref[0, 0, pl.ds(off, t), :], (((1,), (1,)), ((), ())),
                                      preferred_element_type=F32)

    def update(j, slot, diagonal):
        off = pl.multiple_of(j * t, t)
        s = s_ref[slot]
        if diagonal:
            row = lax.broadcasted_iota(jnp.int32, (t, t), 0)
            col = lax.broadcasted_iota(jnp.int32, (t, t), 1)
            s = jnp.where(col <= row, s, NEG)
        m_old = m_ref[...]
        m_new = jnp.maximum(m_old, _row_max(s))
        p = jnp.exp2(s - m_new)
        alpha = jnp.exp2(m_old - m_new)
        l_ref[...] = alpha * l_ref[...] + _row_sum(p)
        acc_ref[...] = alpha * acc_ref[...] + jnp.dot(p.astype(BF16), v_ref[0, 0, pl.ds(off, t), :],
                                                      preferred_element_type=F32)
        m_ref[...] = m_new

    scores(0, 0)
    n_pairs = qi // 2

    def body(p, carry):
        j = 2 * p
        scores(j + 1, 1)
        update(j, 0, False)
        scores(j + 2, 0)
        update(j + 1, 1, False)
        return carry

    lax.fori_loop(0, n_pairs, body, 0)
    j_tail = 2 * n_pairs

    @pl.when(qi - j_tail == 1)
    def _():
        scores(j_tail + 1, 1)
        update(j_tail, 0, False)
        update(j_tail + 1, 1, True)

    @pl.when(qi == j_tail)
    def _():
        update(j_tail, 0, True)

    o_ref[0] = (acc_ref[...] / l_ref[...]).astype(o_ref.dtype)


def _mla_flash(q_cat, k_cat, v, batch, seq):
    t = min(MLA_T, seq)
    n = seq // t
    kern = functools.partial(_mla_flash_kernel, t=t)
    return pl.pallas_call(
        kern,
        name="mla_flash",
        grid=(batch, MLA_HEADS, n),
        in_specs=[
            pl.BlockSpec((1, 1, t, MLA_QK_DIM), lambda b, h, qi: (b, h, qi, 0)),
            pl.BlockSpec((1, 1, seq, MLA_QK_DIM), lambda b, h, qi: (b, h, 0, 0)),
            pl.BlockSpec((1, 1, seq, MLA_V), lambda b, h, qi: (b, h, 0, 0)),
        ],
        out_specs=pl.BlockSpec((1, t, MLA_V), lambda b, h, qi: (b, qi, h)),
        out_shape=jax.ShapeDtypeStruct((batch, seq, MLA_OUT_DIM), BF16),
        scratch_shapes=[pltpu.VMEM((2, t, t), F32), pltpu.VMEM((t, 1), F32), pltpu.VMEM((t, 1), F32),
                        pltpu.VMEM((t, MLA_V), F32)],
        compiler_params=_cparams(3),
    )(q_cat, k_cat, v)


def _merge_kernel(oa_ref, ob_ref, wa_ref, wb_ref, ga_ref, gb_ref, o_ref):
    pa = jnp.dot(oa_ref[...], wa_ref[...], preferred_element_type=F32)
    pb = jnp.dot(ob_ref[...], wb_ref[...], preferred_element_type=F32)
    o_ref[...] = (jax.nn.sigmoid(ga_ref[...]) * pa + jax.nn.sigmoid(gb_ref[...]) * pb).astype(o_ref.dtype)


def _gated_merge(o_a, o_b, w_pa, w_pb, gates):
    m = o_a.shape[0]
    d = w_pa.shape[1]
    tm = _pick(m, (512, 256, 128))
    tn = _pick(d, (1024, 512, 256, 128))
    nn = d // tn
    return pl.pallas_call(
        _merge_kernel,
        name="gated_merge",
        grid=(nn, m // tm),
        in_specs=[
            pl.BlockSpec((tm, o_a.shape[1]), lambda j, i: (i, 0)),
            pl.BlockSpec((tm, o_b.shape[1]), lambda j, i: (i, 0)),
            pl.BlockSpec((w_pa.shape[0], tn), lambda j, i: (0, j)),
            pl.BlockSpec((w_pb.shape[0], tn), lambda j, i: (0, j)),
            pl.BlockSpec((tm, tn), lambda j, i: (i, j)),
            pl.BlockSpec((tm, tn), lambda j, i: (i, j + nn)),
        ],
        out_specs=pl.BlockSpec((tm, tn), lambda j, i: (i, j)),
        out_shape=jax.ShapeDtypeStruct((m, d), BF16),
        compiler_params=_cparams(2),
    )(o_a, o_b, w_pa, w_pb, gates, gates)


def _layer_norm(y, g, b):
    mu = jnp.mean(y, axis=-1, keepdims=True)
    dlt = y - mu
    var = jnp.mean(dlt * dlt, axis=-1, keepdims=True)
    return dlt * lax.rsqrt(var + LN_EPS) * g + b


def _proj_ln_kernel(a_ref, w_ref, x_ref, g_ref, b_ref, o_ref, slab_ref, *, alpha):
    y = alpha * x_ref[...] + jnp.dot(a_ref[...], w_ref[...], preferred_element_type=F32)
    h = _layer_norm(y, g_ref[...], b_ref[...])
    o_ref[...] = h
    tm, d = h.shape
    spt = d // LANES
    for s in range(spt):
        slab_ref[pl.ds(s, tm, stride=spt), :] = h[:, s * LANES:(s + 1) * LANES]


def _proj_ln(a, w, x, g, b, alpha):
    m, d = x.shape
    tm = _pick(m, (256, 128))
    spt = d // LANES
    return pl.pallas_call(
        functools.partial(_proj_ln_kernel, alpha=alpha),
        name="proj_ln",
        grid=(m // tm,),
        in_specs=[
            pl.BlockSpec((tm, a.shape[1]), lambda i: (i, 0)),
            pl.BlockSpec(w.shape, lambda i: (0, 0)),
            pl.BlockSpec((tm, d), lambda i: (i, 0)),
            pl.BlockSpec((1, d), lambda i: (0, 0)),
            pl.BlockSpec((1, d), lambda i: (0, 0)),
        ],
        out_specs=[pl.BlockSpec((tm, d), lambda i: (i, 0)),
                   pl.BlockSpec((tm * spt, LANES), lambda i: (i, 0))],
        out_shape=[jax.ShapeDtypeStruct((m, d), F32),
                   jax.ShapeDtypeStruct((m * spt, LANES), F32)],
        compiler_params=_cparams(1),
    )(a, w, x, g, b)


def _router_kernel(h_ref, whi_ref, wlo_ref, b_ref, ids_ref, wts_ref):
    h = h_ref[...]
    h_hi = h.astype(BF16)
    h_lo = (h - h_hi.astype(F32)).astype(BF16)
    logits = (jnp.dot(h_hi, whi_ref[...], preferred_element_type=F32)
              + jnp.dot(h_hi, wlo_ref[...], preferred_element_type=F32)
              + jnp.dot(h_lo, whi_ref[...], preferred_element_type=F32)) + b_ref[...]
    tm = h.shape[0]
    lane = lax.broadcasted_iota(jnp.int32, (tm, LANES), 1)
    lane_f = lane.astype(F32)
    big = float(4 * LANES)
    ninf = -jnp.inf
    gl = jnp.where(lane < N_GROUPS, logits, ninf)
    ge = jnp.exp(gl - jnp.max(gl, axis=1, keepdims=True))
    gprob = ge / jnp.sum(ge, axis=1, keepdims=True)
    gw = jnp.max(gprob, axis=1, keepdims=True)
    gidx = jnp.min(jnp.where(gprob == gw, lane_f, big), axis=1, keepdims=True).astype(jnp.int32)
    in_group = lax.shift_right_arithmetic(lane - N_GROUPS, 3) == gidx
    el = jnp.where(in_group, logits, ninf)
    ee = jnp.exp(el - jnp.max(el, axis=1, keepdims=True))
    ep = jnp.where(in_group, ee / jnp.sum(ee, axis=1, keepdims=True), -1.0)
    p1 = jnp.max(ep, axis=1, keepdims=True)
    i1 = jnp.min(jnp.where(ep == p1, lane_f, big), axis=1, keepdims=True).astype(jnp.int32)
    ep2 = jnp.where(lane == i1, -1.0, ep)
    p2 = jnp.max(ep2, axis=1, keepdims=True)
    i2 = jnp.min(jnp.where(ep2 == p2, lane_f, big), axis=1, keepdims=True).astype(jnp.int32)
    den = p1 + p2
    w1 = gw * (p1 / den)
    w2 = gw * (p2 / den)
    ids_ref[...] = jnp.where(lane == 0, i1 - N_GROUPS, jnp.where(lane == 1, i2 - N_GROUPS, 0))
    wts_ref[...] = jnp.where(lane == 0, w1, jnp.where(lane == 1, w2, 0.0))


def _router(h, w_hi, w_lo, b_r):
    m, d = h.shape
    tm = _pick(m, (512, 256, 128))
    return pl.pallas_call(
        _router_kernel,
        name="router",
        grid=(m // tm,),
        in_specs=[
            pl.BlockSpec((tm, d), lambda i: (i, 0)),
            pl.BlockSpec((d, LANES), lambda i: (0, 0)),
            pl.BlockSpec((d, LANES), lambda i: (0, 0)),
            pl.BlockSpec((1, LANES), lambda i: (0, 0)),
        ],
        out_specs=[pl.BlockSpec((tm, LANES), lambda i: (i, 0)),
                   pl.BlockSpec((tm, LANES), lambda i: (i, 0))],
        out_shape=[jax.ShapeDtypeStruct((m, LANES), jnp.int32),
                   jax.ShapeDtypeStruct((m, LANES), F32)],
        compiler_params=_cparams(1),
    )(h, w_hi, w_lo, b_r)


def _row_copy(src_hbm, src_row, dst_vmem, dst_row, sem):
    return pltpu.make_async_copy(src_hbm.at[pl.ds(src_row, 1)], dst_vmem.at[pl.ds(dst_row, 1)], sem)


def _expert_kernel(ie_ref, row0_ref, nsub_ref, skip_ref, nrows_ref, tok_ref,
                   slab_hbm, wg_ref, wu_ref, wd_ref, y_hbm,
                   slab_buf, x_ref, g_ref, u_ref, a_ref, o_buf, zbuf, pend_ref, gsem, osem, zsem, *, d, nck):
    it = pl.program_id(0)
    j = pl.program_id(1)
    nsub = nsub_ref[it]
    spt = d // LANES
    ck = d // nck
    sub_rows = MOE_BLK * spt
    n_fill = (y_hbm.shape[0] - nrows_ref[0]) // MOE_BLK

    def fill_copy(b):
        row = pl.multiple_of(nrows_ref[0] + b * MOE_BLK, MOE_BLK)
        return pltpu.make_async_copy(zbuf, y_hbm.at[pl.ds(row, MOE_BLK)], zsem)

    def issue_gather(item):
        for sb in range(ITEM_SUB):
            @pl.when(sb < nsub_ref[item])
            def _(sb=sb):
                base = row0_ref[item] + sb * MOE_BLK

                def start(r, c):
                    src = pl.multiple_of(tok_ref[base + r] * spt, spt)
                    dst = pl.multiple_of((sb * MOE_BLK + r) * spt, spt)
                    pltpu.make_async_copy(slab_hbm.at[pl.ds(src, spt)], slab_buf.at[pl.ds(dst, spt)],
                                          gsem.at[sb]).start()
                    return c
                lax.fori_loop(0, MOE_BLK, start, 0, unroll=8)

    @pl.when(j == 0)
    def _():
        @pl.when(it == 0)
        def _():
            pend_ref[0] = 0
            pend_ref[1] = 0
            issue_gather(0)
            zbuf[...] = jnp.zeros(zbuf.shape, F32)
            lax.fori_loop(0, n_fill, lambda b, c: (fill_copy(b).start(), c)[1], 0)

        for sb in range(ITEM_SUB):
            @pl.when(sb < nsub)
            def _(sb=sb):
                pltpu.make_async_copy(slab_hbm.at[pl.ds(0, sub_rows)],
                                      slab_buf.at[pl.ds(sb * sub_rows, sub_rows)], gsem.at[sb]).wait()
                for s in range(spt):
                    col = s * LANES
                    x_ref[col // ck, sb * MOE_BLK:(sb + 1) * MOE_BLK, col % ck:col % ck + LANES] = (
                        slab_buf[pl.ds(sb * sub_rows + s, MOE_BLK, stride=spt), :].astype(BF16))

        @pl.when(it + 1 < pl.num_programs(0))
        def _():
            issue_gather(it + 1)

    for n in range(1, ITEM_SUB + 1):
        rows = n * MOE_BLK

        @pl.when((nsub == n) & (j == 0))
        def _(rows=rows):
            xk = x_ref[0, 0:rows, :]
            g_ref[0:rows] = jnp.dot(xk, wg_ref[...].astype(BF16), preferred_element_type=F32)
            u_ref[0:rows] = jnp.dot(xk, wu_ref[...].astype(BF16), preferred_element_type=F32)

        @pl.when((nsub == n) & (j > 0) & (j < nck))
        def _(rows=rows):
            xk = x_ref[j, 0:rows, :]
            g_ref[0:rows] += jnp.dot(xk, wg_ref[...].astype(BF16), preferred_element_type=F32)
            u_ref[0:rows] += jnp.dot(xk, wu_ref[...].astype(BF16), preferred_element_type=F32)

        @pl.when((nsub == n) & (j == nck))
        def _(rows=rows):
            g = g_ref[0:rows]
            a_ref[0:rows] = (g * jax.nn.sigmoid(g) * u_ref[0:rows]).astype(BF16)

    @pl.when(j >= nck)
    def _():
        slot = j % 2
        other = 1 - slot
        col0 = pl.multiple_of((j - nck) * ck, ck)
        for n in range(1, ITEM_SUB + 1):
            rows = n * MOE_BLK

            @pl.when(nsub == n)
            def _(rows=rows):
                o_buf[slot, 0:rows] = jnp.dot(a_ref[0:rows], wd_ref[...].astype(BF16), preferred_element_type=F32)
                pltpu.make_async_copy(o_buf.at[slot, pl.ds(0, rows)],
                                      y_hbm.at[pl.ds(pl.multiple_of(row0_ref[it], MOE_BLK), rows), pl.ds(col0, ck)],
                                      osem.at[slot]).start()

        def drain(s):
            for n in range(1, ITEM_SUB + 1):
                rows = n * MOE_BLK

                @pl.when(pend_ref[s] == n)
                def _(rows=rows):
                    pltpu.make_async_copy(o_buf.at[s, pl.ds(0, rows)],
                                          y_hbm.at[pl.ds(0, rows), pl.ds(0, ck)], osem.at[s]).wait()
            pend_ref[s] = 0

        drain(other)
        pend_ref[slot] = nsub

        @pl.when((it == pl.num_programs(0) - 1) & (j == pl.num_programs(1) - 1))
        def _():
            drain(slot)
            lax.fori_loop(0, n_fill, lambda b, c: (fill_copy(b).wait(), c)[1], 0)


def _experts(tables, buf_tok, slabs, wg, wu, wd, cap):
    item_expert, item_row0, item_nsub, item_skip, n_rows_used = tables
    n_items = item_expert.shape[0]
    n_exp, d, ff = wg.shape
    nck = MOE_CHUNKS
    ck = d // nck
    rows = ITEM_SUB * MOE_BLK
    spt = d // LANES

    def w_in_map(it, j, ie, r0, ns, sk, nr, tok):
        return (ie[it], jnp.where(sk[it] == 1, nck - 1, jnp.minimum(j, nck - 1)), 0)

    def w_out_map(it, j, ie, r0, ns, sk, nr, tok):
        return (ie[it], 0, jnp.where(sk[it] == 1, nck - 1, jnp.maximum(j - nck, 0)))

    return pl.pallas_call(
        functools.partial(_expert_kernel, d=d, nck=nck),
        name="moe_experts",
        grid_spec=pltpu.PrefetchScalarGridSpec(
            num_scalar_prefetch=6,
            grid=(n_items, 2 * nck),
            in_specs=[
                pl.BlockSpec(memory_space=pl.ANY),
                pl.BlockSpec((None, ck, ff), w_in_map),
                pl.BlockSpec((None, ck, ff), w_in_map),
                pl.BlockSpec((None, ff, ck), w_out_map),
            ],
            out_specs=pl.BlockSpec(memory_space=pl.ANY),
            scratch_shapes=[
                pltpu.VMEM((rows * spt, LANES), F32),
                pltpu.VMEM((nck, rows, ck), BF16),
                pltpu.VMEM((rows, ff), F32),
                pltpu.VMEM((rows, ff), F32),
                pltpu.VMEM((rows, ff), BF16),
                pltpu.VMEM((2, rows, ck), F32),
                pltpu.VMEM((MOE_BLK, d), F32),
                pltpu.SMEM((2,), jnp.int32),
                pltpu.SemaphoreType.DMA((ITEM_SUB,)),
                pltpu.SemaphoreType.DMA((2,)),
                pltpu.SemaphoreType.DMA(()),
            ],
        ),
        out_shape=jax.ShapeDtypeStruct((cap, d), F32),
        compiler_params=_cparams(2, vmem_mb=EXPERT_VMEM_MB),
    )(item_expert, item_row0, item_nsub, item_skip, n_rows_used, buf_tok, slabs, wg, wu, wd)


def _combine_kernel(dest_ref, y_hbm, h_ref, wts_ref, g_ref, b_ref, o_ref, buf, sem, *, tm, alpha):
    i = pl.program_id(0)
    slot = i % 2

    def issue(tile, dst_slot):
        def start(r, c):
            a = 2 * (tile * tm + r)
            _row_copy(y_hbm, dest_ref[a], buf.at[dst_slot, 0], r, sem.at[dst_slot]).start()
            _row_copy(y_hbm, dest_ref[a + 1], buf.at[dst_slot, 1], r, sem.at[dst_slot]).start()
            return c
        lax.fori_loop(0, tm, start, 0, unroll=8)

    @pl.when(i == 0)
    def _():
        issue(0, 0)

    @pl.when(i + 1 < pl.num_programs(0))
    def _():
        issue(i + 1, 1 - slot)

    for k in range(2):
        pltpu.make_async_copy(y_hbm.at[pl.ds(0, tm)], buf.at[slot, k], sem.at[slot]).wait()
    wts = wts_ref[...]
    f = buf[slot, 0] * wts[:, 0:1] + buf[slot, 1] * wts[:, 1:2]
    o_ref[...] = _layer_norm(alpha * h_ref[...] + f, g_ref[...], b_ref[...])


def _combine_ln(dest, yb, h, wts, g, b, alpha):
    m, d = h.shape
    tm = _pick(m, (256, 128))
    return pl.pallas_call(
        functools.partial(_combine_kernel, tm=tm, alpha=alpha),
        name="moe_combine",
        grid_spec=pltpu.PrefetchScalarGridSpec(
            num_scalar_prefetch=1,
            grid=(m // tm,),
            in_specs=[
                pl.BlockSpec(memory_space=pl.ANY),
                pl.BlockSpec((tm, d), lambda i, dst: (i, 0)),
                pl.BlockSpec((tm, LANES), lambda i, dst: (i, 0)),
                pl.BlockSpec((1, d), lambda i, dst: (0, 0)),
                pl.BlockSpec((1, d), lambda i, dst: (0, 0)),
            ],
            out_specs=pl.BlockSpec((tm, d), lambda i, dst: (i, 0)),
            scratch_shapes=[pltpu.VMEM((2, 2, tm, d), F32), pltpu.SemaphoreType.DMA((2,))],
        ),
        out_shape=jax.ShapeDtypeStruct((m, d), F32),
        compiler_params=_cparams(1),
    )(dest, yb, h, wts, g, b)


def _dispatch_tables(ids, n_tok):
    flat_e = ids[:, :2].reshape(-1)
    n_assign = flat_e.shape[0]
    onehot = (flat_e[:, None] == jnp.arange(N_EXPERTS, dtype=jnp.int32)[None, :]).astype(jnp.int32)
    csum = jnp.cumsum(onehot, axis=0)
    rank = jnp.sum(onehot * csum, axis=1) - 1
    counts = csum[-1]
    padded = (counts + MOE_BLK - 1) // MOE_BLK * MOE_BLK
    pend = jnp.cumsum(padded)
    pstart = pend - padded
    dest = (pstart[flat_e] + rank).astype(jnp.int32)
    cap = -(-n_assign // MOE_BLK) * MOE_BLK + N_EXPERTS * MOE_BLK
    buf_tok = jnp.zeros((cap,), jnp.int32).at[dest].set(jnp.arange(n_assign, dtype=jnp.int32) // 2)

    item_rows = ITEM_SUB * MOE_BLK
    items_e = (padded + item_rows - 1) // item_rows
    item_end = jnp.cumsum(items_e)
    item_start = item_end - items_e
    n_used = item_end[-1]
    n_items = -(-N_EXPERTS * (ITEM_SUB - 1) // ITEM_SUB) + cap // item_rows + 1
    it = jnp.arange(n_items, dtype=jnp.int32)
    e_it = jnp.minimum(jnp.sum((item_end[None, :] <= it[:, None]).astype(jnp.int32), axis=1), N_EXPERTS - 1)
    k_it = it - item_start[e_it]
    used = it < n_used
    e_last = e_it[jnp.maximum(n_used - 1, 0)]
    item_expert = jnp.where(used, e_it, e_last).astype(jnp.int32)
    item_row0 = jnp.where(used, pstart[e_it] + k_it * item_rows, 0).astype(jnp.int32)
    item_nsub = jnp.where(used, jnp.clip((padded[e_it] - k_it * item_rows) // MOE_BLK, 0, ITEM_SUB), 0)
    item_skip = jnp.where(used, 0, 1).astype(jnp.int32)
    n_rows_used = pend[-1].astype(jnp.int32).reshape(1)
    return dest, buf_tok, (item_expert, item_row0, item_nsub.astype(jnp.int32), item_skip, n_rows_used), cap


def _rope_tables(seq):
    half = MLA_ROPE // 2
    freqs = ROPE_THETA ** (-jnp.arange(half, dtype=F32) / half)
    ang = jnp.arange(seq, dtype=F32)[:, None] * freqs
    cos = jnp.cos(ang)
    sin = jnp.sin(ang)
    return jnp.concatenate([cos, cos], -1), jnp.concatenate([sin, sin], -1)


def _swap_halves_neg(w):
    half = w.shape[-1] // 2
    return jnp.concatenate([-w[..., half:], w[..., :half]], axis=-1)


def _layer(x, w_in, w_ukv, g_ckv, w_pa, w_pb, w_o, ln1_g, ln1_b, w_rg, b_rg, w_re, b_re,
           w_gate_e, w_up_e, w_down_e, ln2_g, ln2_b, alpha):
    batch, seq, d = x.shape
    n_tok = batch * seq
    x2 = x.reshape(n_tok, d)
    xb16 = x2.astype(BF16)

    splits = (DSA_Q_DIM, DSA_KV_DIM, DSA_KV_DIM, IDX_Q_DIM, IDX_DIM, IDX_HEADS, MLA_Q_DIM, KV_RANK, MLA_ROPE, 2 * d)
    offs = np.cumsum((0,) + splits)
    w_dq, w_dk, w_dv, w_iq, w_ik, w_iw, w_mq, w_ckv, w_kr, w_gates = [
        w_in[:, offs[j]:offs[j + 1]] for j in range(len(splits))]
    w_mq3 = w_mq.reshape(d, MLA_HEADS, MLA_QK_DIM)
    w_qn = w_mq3[:, :, :MLA_NOPE].reshape(d, MLA_HEADS * MLA_NOPE)
    w_qr = w_mq3[:, :, MLA_NOPE:]
    w_qs = _swap_halves_neg(w_qr)
    wr = MLA_HEADS * MLA_ROPE
    wb = jnp.concatenate([w_iq, w_dk, w_dv], axis=1).astype(BF16)
    off_k = IDX_Q_DIM
    off_v = off_k + DSA_KV_DIM
    pad_i = jnp.zeros((d, LANES - IDX_DIM - IDX_HEADS), F32)
    ws = jnp.concatenate([w_dq, w_qn, w_qr.reshape(d, wr), w_qs.reshape(d, wr), w_ckv, w_ik, w_iw, pad_i,
                          w_kr, _swap_halves_neg(w_kr)], axis=1).astype(BF16)
    off_qn = DSA_Q_DIM
    off_qr = off_qn + MLA_HEADS * MLA_NOPE
    off_qs = off_qr + wr
    off_c = off_qs + wr
    off_i = off_c + KV_RANK
    off_kr = off_i + LANES

    zb = _matmul(xb16, wb, BF16, tn=wb.shape[1])
    zs = _matmul(xb16, ws, F32, tn=ws.shape[1] // 2)
    gates = _matmul(xb16, w_gates.astype(BF16), F32, tn=_pick(2 * d, (2048, 1024, 512, 256)))

    kidx = zs[:, off_i:off_i + IDX_DIM].astype(BF16).reshape(batch, seq, IDX_DIM)
    w_idx = zs[:, off_i + IDX_DIM:off_i + IDX_DIM + IDX_HEADS].reshape(batch, seq, IDX_HEADS)
    o_a = _dsa_attention(zs, zb, w_idx, kidx, batch, seq, off_k, off_v)

    cos2, sin2 = _rope_tables(seq)
    w_ukv3 = w_ukv.reshape(KV_RANK, MLA_HEADS, MLA_NOPE + MLA_V)
    w_kv = jnp.concatenate([w_ukv3[:, :, :MLA_NOPE].reshape(KV_RANK, MLA_HEADS * MLA_NOPE),
                            w_ukv3[:, :, MLA_NOPE:].reshape(KV_RANK, MLA_HEADS * MLA_V)], axis=1).astype(BF16)
    q_cat = _mla_q_prep(zs, jnp.tile(cos2, (1, MLA_HEADS)), jnp.tile(sin2, (1, MLA_HEADS)),
                        batch, seq, off_qn, off_qr, off_qs)
    k_cat, v_mla = _mla_kv_prep(zs, g_ckv.reshape(1, KV_RANK), w_kv, cos2, sin2, batch, seq, off_c, off_kr)
    o_b = _mla_flash(q_cat, k_cat, v_mla, batch, seq).reshape(n_tok, MLA_OUT_DIM)

    merged = _gated_merge(o_a, o_b, w_pa.astype(BF16), w_pb.astype(BF16), gates)
    h, slabs = _proj_ln(merged, w_o.astype(BF16), x2, ln1_g.reshape(1, d), ln1_b.reshape(1, d), alpha)

    pad_r = jnp.zeros((d, LANES - N_GROUPS - N_EXPERTS), F32)
    w_r = jnp.concatenate([w_rg, w_re, pad_r], axis=1)
    w_r_hi = w_r.astype(BF16)
    w_r_lo = (w_r - w_r_hi.astype(F32)).astype(BF16)
    b_r = jnp.concatenate([b_rg, b_re, jnp.zeros((LANES - N_GROUPS - N_EXPERTS,), F32)]).reshape(1, LANES)
    ids, wts = _router(h, w_r_hi, w_r_lo, b_r)
    dest, buf_tok, item_tables, cap = _dispatch_tables(ids, n_tok)
    yb = _experts(item_tables, buf_tok, slabs, w_gate_e, w_up_e, w_down_e, cap)
    out = _combine_ln(dest, yb, h, wts, ln2_g.reshape(1, d), ln2_b.reshape(1, d), alpha)
    return out.reshape(batch, seq, d)


def kernel(x, w_in, w_ukv, g_ckv, w_pa, w_pb, w_o, ln1_g, ln1_b, w_rg, b_rg, w_re, b_re,
           w_gate_e, w_up_e, w_down_e, ln2_g, ln2_b):
    depth = w_in.shape[0]
    alpha = (2.0 * depth) ** 0.25
    for l in range(depth):
        x = _layer(x, w_in[l], w_ukv[l], g_ckv[l], w_pa[l], w_pb[l], w_o[l], ln1_g[l], ln1_b[l],
                   w_rg[l], b_rg[l], w_re[l], b_re[l], w_gate_e[l], w_up_e[l], w_down_e[l],
                   ln2_g[l], ln2_b[l], alpha)
    return x
```

```python
import functools

import numpy as np
import jax
import jax.numpy as jnp
from jax import lax
from jax.experimental import pallas as pl
from jax.experimental.pallas import tpu as pltpu

N_DSA_HEADS = 8
DSA_HEAD_DIM = 128
DSA_KV_HEADS = 2
DSA_REP = N_DSA_HEADS // DSA_KV_HEADS
IDX_HEADS = 16
IDX_DIM = 64
TOPK_MAX = 256
MLA_HEADS = 8
MLA_NOPE = 128
MLA_ROPE = 64
MLA_V = 128
KV_RANK = 512
ROPE_THETA = 10000.0
N_GROUPS = 8
EXPERTS_PER_GROUP = 8
N_EXPERTS = N_GROUPS * EXPERTS_PER_GROUP
LN_EPS = 1e-5
RMS_EPS = 1e-6

DSA_Q_DIM = N_DSA_HEADS * DSA_HEAD_DIM
DSA_KV_DIM = DSA_KV_HEADS * DSA_HEAD_DIM
IDX_Q_DIM = IDX_HEADS * IDX_DIM
MLA_QK_DIM = MLA_NOPE + MLA_ROPE
MLA_Q_DIM = MLA_HEADS * MLA_QK_DIM
MLA_OUT_DIM = MLA_HEADS * MLA_V

LANES = 128
VMEM_LIMIT_MB = 56

DSA_TQ = 128
DSA_KC = 512
MLA_T = 512
MOE_BLK = 256
ITEM_SUB = 3
MOE_CHUNKS = 4
EXPERT_VMEM_MB = 58
NEG = -1e30
LOG2E = 1.4426950408889634
INT_MIN = -2 ** 31
FLT_MAX = 3.4028234663852886e38

F32 = jnp.float32
BF16 = jnp.bfloat16


def _cparams(n_axes, vmem_mb=VMEM_LIMIT_MB):
    return pltpu.CompilerParams(dimension_semantics=("arbitrary",) * n_axes,
                                vmem_limit_bytes=vmem_mb * 1024 * 1024)


def _pick(n, prefs):
    for p in prefs:
        if n % p == 0:
            return p
    return n


def _lane_tiles(x):
    return [x[:, j * LANES:(j + 1) * LANES] for j in range(x.shape[1] // LANES)]


def _row_max(x):
    return jnp.max(functools.reduce(jnp.maximum, _lane_tiles(x)), axis=1, keepdims=True)


def _row_sum(x):
    return jnp.sum(functools.reduce(jnp.add, _lane_tiles(x)), axis=1, keepdims=True)


def _mm_kernel(a_ref, w_ref, o_ref):
    o_ref[...] = jnp.dot(a_ref[...], w_ref[...], preferred_element_type=F32).astype(o_ref.dtype)


def _matmul(a, w, out_dtype, tn):
    m, k = a.shape
    n = w.shape[1]
    tm = _pick(m, (512, 256, 128))
    return pl.pallas_call(
        _mm_kernel,
        grid=(n // tn, m // tm),
        in_specs=[pl.BlockSpec((tm, k), lambda j, i: (i, 0)),
                  pl.BlockSpec((k, tn), lambda j, i: (0, j))],
        out_specs=pl.BlockSpec((tm, tn), lambda j, i: (i, j)),
        out_shape=jax.ShapeDtypeStruct((m, n), out_dtype),
        compiler_params=_cparams(2),
    )(a, w)


def _mm_cast_kernel(a_ref, w_ref, o_ref, a16_ref):
    a16 = a_ref[...].astype(BF16)
    a16_ref[...] = a16
    o_ref[...] = jnp.dot(a16, w_ref[...], preferred_element_type=F32).astype(o_ref.dtype)


def _matmul_cast(a, w, out_dtype):
    m, k = a.shape
    n = w.shape[1]
    tm = _pick(m, (512, 256, 128))
    return pl.pallas_call(
        _mm_cast_kernel,
        grid=(m // tm,),
        in_specs=[pl.BlockSpec((tm, k), lambda i: (i, 0)),
                  pl.BlockSpec((k, n), lambda i: (0, 0))],
        out_specs=[pl.BlockSpec((tm, n), lambda i: (i, 0)),
                   pl.BlockSpec((tm, k), lambda i: (i, 0))],
        out_shape=[jax.ShapeDtypeStruct((m, n), out_dtype), jax.ShapeDtypeStruct((m, k), BF16)],
        compiler_params=_cparams(1),
    )(a, w)


def _dsa_kernel(q_ref, iq_ref, w_ref, kidx_ref, k_ref, v_ref, o_ref,
                sc_ref, thr_ref, jcut_ref, s_ref, m_ref, l_ref, acc_ref, *, tq, kc, n_sel, seq):
    i = pl.program_id(1)
    nch = (i * tq + tq + kc - 1) // kc
    row = lax.broadcasted_iota(jnp.int32, (tq, kc), 0)
    lane = lax.broadcasted_iota(jnp.int32, (tq, kc), 1)
    qpos = i * tq + row

    iq_all = iq_ref[...]
    iq = jnp.concatenate([iq_all[:, h * IDX_DIM:(h + 1) * IDX_DIM] for h in range(IDX_HEADS)],
                         axis=0)
    w = w_ref[0] * (IDX_DIM ** -0.5 * IDX_HEADS ** -0.5)

    def score_body(c, carry):
        off = pl.multiple_of(c * kc, kc)
        kx = kidx_ref[0, pl.ds(off, kc), :]
        logits = lax.dot_general(iq, kx, (((1,), (1,)), ((), ())), preferred_element_type=F32)
        sc = jnp.zeros((tq, kc), F32)
        for h in range(IDX_HEADS):
            sc = sc + jnp.maximum(logits[h * tq:(h + 1) * tq], 0.0) * w[:, h:h + 1]
        sc_ref[c] = jnp.where(off + lane <= qpos, sc, -jnp.inf)
        return carry

    lax.fori_loop(0, nch, score_body, 0)

    def key_to_f32(key):
        return pltpu.bitcast(jnp.where(key < 0, key ^ jnp.int32(0x7FFFFFFF), key), F32)

    key_lowest = INT_MIN + 0x00800000
    thr_ref[...] = jnp.full((tq, 1), -FLT_MAX, F32)
    jcut_ref[...] = jnp.full((tq, 1), seq, jnp.int32)

    def count(pred_fn):
        def body(c, acc):
            sch = sc_ref[c]
            for j in range(kc // LANES):
                acc = acc + pred_fn(sch[:, j * LANES:(j + 1) * LANES], c * kc + j * LANES)
            return acc
        acc = lax.fori_loop(0, nch, body, jnp.zeros((tq, LANES), jnp.int32))
        return jnp.sum(acc.astype(F32), axis=1, keepdims=True).astype(jnp.int32)

    lane1 = lax.broadcasted_iota(jnp.int32, (tq, LANES), 1)

    @pl.when(i * tq + tq > n_sel)
    def _():
        def bit_body(b, t):
            cand = t + lax.shift_left(jnp.int32(1), 31 - b)
            cand_b = jnp.broadcast_to(key_to_f32(cand), (tq, LANES))
            cnt = count(lambda sch, _: jnp.where(sch >= cand_b, 1, 0))
            return jnp.where(cnt >= n_sel, cand, t)

        t = lax.fori_loop(0, 32, bit_body, jnp.full((tq, 1), INT_MIN, jnp.int32))
        t_f = key_to_f32(jnp.maximum(t, key_lowest))
        thr_ref[...] = t_f
        t_b = jnp.broadcast_to(t_f, (tq, LANES))
        c_ge = count(lambda sch, _: jnp.where(sch >= t_b, 1, 0))
        c_gt = count(lambda sch, _: jnp.where(sch > t_b, 1, 0))
        excess = c_ge > n_sel
        need = n_sel - c_gt

        @pl.when(jnp.max(jnp.where(excess, 1.0, 0.0)) > 0.0)
        def _():
            nbits = max(1, int(np.ceil(np.log2(seq))))

            def jbit(b, jc):
                cand = jc + lax.shift_left(jnp.int32(1), nbits - 1 - b)
                cand_b = jnp.broadcast_to(cand, (tq, LANES))
                cnt = count(lambda sch, base: jnp.where(
                    sch == t_b, jnp.where(base + lane1 < cand_b, 1, 0), 0))
                return jnp.where(cnt < need, cand, jc)

            jc = lax.fori_loop(0, nbits, jbit, jnp.zeros((tq, 1), jnp.int32))
            jcut_ref[...] = jnp.where(excess, jc, seq)

    qs = (q_ref[...] * (DSA_HEAD_DIM ** -0.5 * LOG2E)).astype(BF16)
    qg = [jnp.concatenate([qs[:, (g * DSA_REP + r) * DSA_HEAD_DIM:(g * DSA_REP + r + 1) * DSA_HEAD_DIM]
                           for r in range(DSA_REP)], axis=0) for g in range(DSA_KV_HEADS)]
    m_ref[...] = jnp.full(m_ref.shape, NEG, F32)
    l_ref[...] = jnp.zeros(l_ref.shape, F32)
    acc_ref[...] = jnp.zeros(acc_ref.shape, F32)
    t = thr_ref[...]
    jc = jcut_ref[...]

    def scores(c, slot):
        off = pl.multiple_of(c * kc, kc)
        for g in range(DSA_KV_HEADS):
            s_ref[slot, g] = lax.dot_general(
                qg[g], k_ref[pl.ds(off, kc), g * DSA_HEAD_DIM:(g + 1) * DSA_HEAD_DIM],
                (((1,), (1,)), ((), ())), preferred_element_type=F32)

    def update(c, slot):
        off = pl.multiple_of(c * kc, kc)
        sc = sc_ref[c]
        kpos = off + lane
        sel = jnp.where(kpos <= jc, jnp.where(sc >= t, 1, 0), jnp.where(sc > t, 1, 0)) > 0
        md = jnp.where(sel, (qpos - kpos).astype(F32), -NEG)
        for g in range(DSA_KV_HEADS):
            s = s_ref[slot, g]
            vg = v_ref[pl.ds(off, kc), g * DSA_HEAD_DIM:(g + 1) * DSA_HEAD_DIM]
            e = jnp.concatenate(
                [s[r * tq:(r + 1) * tq] - md * (LOG2E * 2.0 ** (-8.0 * (g * DSA_REP + r + 1) / N_DSA_HEADS))
                 for r in range(DSA_REP)], axis=0)
            m_old = m_ref[g]
            m_new = jnp.maximum(m_old, _row_max(e))
            p = jnp.exp2(e - m_new)
            alpha = jnp.exp2(m_old - m_new)
            l_ref[g] = alpha * l_ref[g] + _row_sum(p)
            acc_ref[g] = alpha * acc_ref[g] + jnp.dot(p.astype(BF16), vg, preferred_element_type=F32)
            m_ref[g] = m_new

    scores(0, 0)
    n_pairs = (nch - 1) // 2

    def att_body(p, carry):
        c = 2 * p
        scores(c + 1, 1)
        update(c, 0)
        scores(c + 2, 0)
        update(c + 1, 1)
        return carry

    lax.fori_loop(0, n_pairs, att_body, 0)
    c_tail = 2 * n_pairs

    @pl.when(nch - c_tail == 2)
    def _():
        scores(c_tail + 1, 1)
        update(c_tail, 0)
        update(c_tail + 1, 1)

    @pl.when(nch - c_tail == 1)
    def _():
        update(c_tail, 0)

    for g in range(DSA_KV_HEADS):
        o = acc_ref[g] / l_ref[g]
        for r in range(DSA_REP):
            hh = g * DSA_REP + r
            o_ref[:, hh * DSA_HEAD_DIM:(hh + 1) * DSA_HEAD_DIM] = o[r * tq:(r + 1) * tq].astype(o_ref.dtype)


def _dsa_attention(zs, zb, w_idx, kidx, batch, seq, off_k, off_v):
    tq = min(DSA_TQ, seq)
    kc = min(DSA_KC, seq)
    nq = seq // tq
    n_sel = min(TOPK_MAX, seq // 4)
    kern = functools.partial(_dsa_kernel, tq=tq, kc=kc, n_sel=n_sel, seq=seq)
    return pl.pallas_call(
        kern,
        name="dsa",
        grid=(batch, nq),
        in_specs=[
            pl.BlockSpec((tq, DSA_Q_DIM), lambda b, i: (b * nq + i, 0)),
            pl.BlockSpec((tq, IDX_Q_DIM), lambda b, i: (b * nq + i, 0)),
            pl.BlockSpec((1, tq, IDX_HEADS), lambda b, i: (b, i, 0)),
            pl.BlockSpec((1, seq, IDX_DIM), lambda b, i: (b, 0, 0)),
            pl.BlockSpec((seq, DSA_KV_DIM), lambda b, i: (b, off_k // DSA_KV_DIM)),
            pl.BlockSpec((seq, DSA_KV_DIM), lambda b, i: (b, off_v // DSA_KV_DIM)),
        ],
        out_specs=pl.BlockSpec((tq, DSA_Q_DIM), lambda b, i: (b * nq + i, 0)),
        out_shape=jax.ShapeDtypeStruct((batch * seq, DSA_Q_DIM), BF16),
        scratch_shapes=[
            pltpu.VMEM((seq // kc, tq, kc), F32),
            pltpu.VMEM((tq, 1), F32),
            pltpu.VMEM((tq, 1), jnp.int32),
            pltpu.VMEM((2, DSA_KV_HEADS, DSA_REP * tq, kc), F32),
            pltpu.VMEM((DSA_KV_HEADS, DSA_REP * tq, 1), F32),
            pltpu.VMEM((DSA_KV_HEADS, DSA_REP * tq, 1), F32),
            pltpu.VMEM((DSA_KV_HEADS, DSA_REP * tq, DSA_HEAD_DIM), F32),
        ],
        compiler_params=_cparams(2),
    )(zs, zb, w_idx, kidx, zb, zb)


def _mla_q_kernel(qn_ref, qr_ref, qs_ref, cos_ref, sin_ref, o_ref):
    c = MLA_QK_DIM ** -0.5 * LOG2E
    pe = ((qr_ref[...] * cos_ref[...] + qs_ref[...] * sin_ref[...]) * c).astype(o_ref.dtype)
    qn = (qn_ref[...] * c).astype(o_ref.dtype)
    for h in range(MLA_HEADS):
        o_ref[0, h, :, 0:MLA_NOPE] = qn[:, h * MLA_NOPE:(h + 1) * MLA_NOPE]
        o_ref[0, h, :, MLA_NOPE:MLA_QK_DIM] = pe[:, h * MLA_ROPE:(h + 1) * MLA_ROPE]


def _mla_q_prep(zs, cos_q, sin_q, batch, seq, off_qn, off_qr, off_qs):
    tm = _pick(seq, (512, 256, 128))
    ns = seq // tm
    wr = MLA_HEADS * MLA_ROPE
    return pl.pallas_call(
        _mla_q_kernel,
        name="mla_q",
        grid=(batch * ns,),
        in_specs=[
            pl.BlockSpec((tm, MLA_HEADS * MLA_NOPE), lambda i: (i, off_qn // (MLA_HEADS * MLA_NOPE))),
            pl.BlockSpec((tm, wr), lambda i: (i, off_qr // wr)),
            pl.BlockSpec((tm, wr), lambda i: (i, off_qs // wr)),
            pl.BlockSpec((tm, wr), lambda i: (i % ns, 0)),
            pl.BlockSpec((tm, wr), lambda i: (i % ns, 0)),
        ],
        out_specs=pl.BlockSpec((1, MLA_HEADS, tm, MLA_QK_DIM), lambda i: (i // ns, 0, i % ns, 0)),
        out_shape=jax.ShapeDtypeStruct((batch, MLA_HEADS, seq, MLA_QK_DIM), BF16),
        compiler_params=_cparams(1),
    )(zs, zs, zs, cos_q, sin_q)


def _mla_kv_kernel(c_ref, kr_ref, g_ref, w_ref, cos_ref, sin_ref, k_ref, v_ref):
    c = c_ref[...]
    cn = c * lax.rsqrt(jnp.mean(c * c, axis=-1, keepdims=True) + RMS_EPS) * g_ref[...]
    kv = jnp.dot(cn.astype(BF16), w_ref[...], preferred_element_type=F32)
    kr = kr_ref[...]
    pe = (kr[:, 0:MLA_ROPE] * cos_ref[...] + kr[:, MLA_ROPE:2 * MLA_ROPE] * sin_ref[...]).astype(k_ref.dtype)
    for h in range(MLA_HEADS):
        k_ref[0, h, :, 0:MLA_NOPE] = kv[:, h * MLA_NOPE:(h + 1) * MLA_NOPE].astype(k_ref.dtype)
        k_ref[0, h, :, MLA_NOPE:MLA_QK_DIM] = pe
        v0 = MLA_HEADS * MLA_NOPE + h * MLA_V
        v_ref[0, h] = kv[:, v0:v0 + MLA_V].astype(v_ref.dtype)


def _mla_kv_prep(zs, g_ckv, w_kv, cos_k, sin_k, batch, seq, off_c, off_kr):
    tm = _pick(seq, (512, 256, 128))
    ns = seq // tm
    nkv = MLA_HEADS * (MLA_NOPE + MLA_V)
    return pl.pallas_call(
        _mla_kv_kernel,
        name="mla_kv",
        grid=(batch * ns,),
        in_specs=[
            pl.BlockSpec((tm, KV_RANK), lambda i: (i, off_c // KV_RANK)),
            pl.BlockSpec((tm, 2 * MLA_ROPE), lambda i: (i, off_kr // (2 * MLA_ROPE))),
            pl.BlockSpec((1, KV_RANK), lambda i: (0, 0)),
            pl.BlockSpec((KV_RANK, nkv), lambda i: (0, 0)),
            pl.BlockSpec((tm, MLA_ROPE), lambda i: (i % ns, 0)),
            pl.BlockSpec((tm, MLA_ROPE), lambda i: (i % ns, 0)),
        ],
        out_specs=[
            pl.BlockSpec((1, MLA_HEADS, tm, MLA_QK_DIM), lambda i: (i // ns, 0, i % ns, 0)),
            pl.BlockSpec((1, MLA_HEADS, tm, MLA_V), lambda i: (i // ns, 0, i % ns, 0)),
        ],
        out_shape=[jax.ShapeDtypeStruct((batch, MLA_HEADS, seq, MLA_QK_DIM), BF16),
                   jax.ShapeDtypeStruct((batch, MLA_HEADS, seq, MLA_V), BF16)],
        compiler_params=_cparams(1),
    )(zs, zs, g_ckv, w_kv, cos_k, sin_k)


def _mla_flash_kernel(q_ref, k_ref, v_ref, o_ref, s_ref, m_ref, l_ref, acc_ref, *, t):
    qi = pl.program_id(2)
    q = q_ref[0, 0]
    m_ref[...] = jnp.full(m_ref.shape, NEG, F32)
    l_ref[...] = jnp.zeros(l_ref.shape, F32)
    acc_ref[...] = jnp.zeros(acc_ref.shape, F32)

    def scores(j, slot):
        off = pl.multiple_of(j * t, t)
        s_ref[slot] = lax.dot_general(q, k_ref[0, 0, pl.ds(off, t), :], (((1,), (1,)), ((), ())),
                                      preferred_element_type=F32)

    def update(j, slot, diagonal):
        off = pl.multiple_of(j * t, t)
        s = s_ref[slot]
        if diagonal:
            row = lax.broadcasted_iota(jnp.int32, (t, t), 0)
            col = lax.broadcasted_iota(jnp.int32, (t, t), 1)
            s = jnp.where(col <= row, s, NEG)
        m_old = m_ref[...]
        m_new = jnp.maximum(m_old, _row_max(s))
        p = jnp.exp2(s - m_new)
        alpha = jnp.exp2(m_old - m_new)
        l_ref[...] = alpha * l_ref[...] + _row_sum(p)
        acc_ref[...] = alpha * acc_ref[...] + jnp.dot(p.astype(BF16), v_ref[0, 0, pl.ds(off, t), :],
                                                      preferred_element_type=F32)
        m_ref[...] = m_new

    scores(0, 0)
    n_pairs = qi // 2

    def body(p, carry):
        j = 2 * p
        scores(j + 1, 1)
        update(j, 0, False)
        scores(j + 2, 0)
        update(j + 1, 1, False)
        return carry

    lax.fori_loop(0, n_pairs, body, 0)
    j_tail = 2 * n_pairs

    @pl.when(qi - j_tail == 1)
    def _():
        scores(j_tail + 1, 1)
        update(j_tail, 0, False)
        update(j_tail + 1, 1, True)

    @pl.when(qi == j_tail)
    def _():
        update(j_tail, 0, True)

    o_ref[0] = (acc_ref[...] / l_ref[...]).astype(o_ref.dtype)


def _mla_flash(q_cat, k_cat, v, batch, seq):
    t = min(MLA_T, seq)
    n = seq // t
    kern = functools.partial(_mla_flash_kernel, t=t)
    return pl.pallas_call(
        kern,
        name="mla_flash",
        grid=(batch, MLA_HEADS, n),
        in_specs=[
            pl.BlockSpec((1, 1, t, MLA_QK_DIM), lambda b, h, qi: (b, h, qi, 0)),
            pl.BlockSpec((1, 1, seq, MLA_QK_DIM), lambda b, h, qi: (b, h, 0, 0)),
            pl.BlockSpec((1, 1, seq, MLA_V), lambda b, h, qi: (b, h, 0, 0)),
        ],
        out_specs=pl.BlockSpec((1, t, MLA_V), lambda b, h, qi: (b, qi, h)),
        out_shape=jax.ShapeDtypeStruct((batch, seq, MLA_OUT_DIM), BF16),
        scratch_shapes=[pltpu.VMEM((2, t, t), F32), pltpu.VMEM((t, 1), F32), pltpu.VMEM((t, 1), F32),
                        pltpu.VMEM((t, MLA_V), F32)],
        compiler_params=_cparams(3),
    )(q_cat, k_cat, v)


def _merge_kernel(oa_ref, ob_ref, wa_ref, wb_ref, ga_ref, gb_ref, o_ref):
    pa = jnp.dot(oa_ref[...], wa_ref[...], preferred_element_type=F32)
    pb = jnp.dot(ob_ref[...], wb_ref[...], preferred_element_type=F32)
    o_ref[...] = (jax.nn.sigmoid(ga_ref[...]) * pa + jax.nn.sigmoid(gb_ref[...]) * pb).astype(o_ref.dtype)


def _gated_merge(o_a, o_b, w_pa, w_pb, gates):
    m = o_a.shape[0]
    d = w_pa.shape[1]
    tm = _pick(m, (512, 256, 128))
    tn = _pick(d, (1024, 512, 256, 128))
    nn = d // tn
    return pl.pallas_call(
        _merge_kernel,
        name="gated_merge",
        grid=(nn, m // tm),
        in_specs=[
            pl.BlockSpec((tm, o_a.shape[1]), lambda j, i: (i, 0)),
            pl.BlockSpec((tm, o_b.shape[1]), lambda j, i: (i, 0)),
            pl.BlockSpec((w_pa.shape[0], tn), lambda j, i: (0, j)),
            pl.BlockSpec((w_pb.shape[0], tn), lambda j, i: (0, j)),
            pl.BlockSpec((tm, tn), lambda j, i: (i, j)),
            pl.BlockSpec((tm, tn), lambda j, i: (i, j + nn)),
        ],
        out_specs=pl.BlockSpec((tm, tn), lambda j, i: (i, j)),
        out_shape=jax.ShapeDtypeStruct((m, d), BF16),
        compiler_params=_cparams(2),
    )(o_a, o_b, w_pa, w_pb, gates, gates)


def _layer_norm(y, g, b):
    mu = jnp.mean(y, axis=-1, keepdims=True)
    dlt = y - mu
    var = jnp.mean(dlt * dlt, axis=-1, keepdims=True)
    return dlt * lax.rsqrt(var + LN_EPS) * g + b


def _proj_ln_kernel(a_ref, w_ref, x_ref, g_ref, b_ref, o_ref, slab_ref, *, alpha):
    y = alpha * x_ref[...] + jnp.dot(a_ref[...], w_ref[...], preferred_element_type=F32)
    h = _layer_norm(y, g_ref[...], b_ref[...])
    o_ref[...] = h
    tm, d = h.shape
    spt = d // LANES
    for s in range(spt):
        slab_ref[pl.ds(s, tm, stride=spt), :] = h[:, s * LANES:(s + 1) * LANES]


def _proj_ln(a, w, x, g, b, alpha):
    m, d = x.shape
    tm = _pick(m, (256, 128))
    spt = d // LANES
    return pl.pallas_call(
        functools.partial(_proj_ln_kernel, alpha=alpha),
        name="proj_ln",
        grid=(m // tm,),
        in_specs=[
            pl.BlockSpec((tm, a.shape[1]), lambda i: (i, 0)),
            pl.BlockSpec(w.shape, lambda i: (0, 0)),
            pl.BlockSpec((tm, d), lambda i: (i, 0)),
            pl.BlockSpec((1, d), lambda i: (0, 0)),
            pl.BlockSpec((1, d), lambda i: (0, 0)),
        ],
        out_specs=[pl.BlockSpec((tm, d), lambda i: (i, 0)),
                   pl.BlockSpec((tm * spt, LANES), lambda i: (i, 0))],
        out_shape=[jax.ShapeDtypeStruct((m, d), F32),
                   jax.ShapeDtypeStruct((m * spt, LANES), F32)],
        compiler_params=_cparams(1),
    )(a, w, x, g, b)


def _router_kernel(h_ref, whi_ref, wlo_ref, b_ref, ids_ref, wts_ref):
    h = h_ref[...]
    h_hi = h.astype(BF16)
    h_lo = (h - h_hi.astype(F32)).astype(BF16)
    logits = (jnp.dot(h_hi, whi_ref[...], preferred_element_type=F32)
              + jnp.dot(h_hi, wlo_ref[...], preferred_element_type=F32)
              + jnp.dot(h_lo, whi_ref[...], preferred_element_type=F32)) + b_ref[...]
    tm = h.shape[0]
    lane = lax.broadcasted_iota(jnp.int32, (tm, LANES), 1)
    lane_f = lane.astype(F32)
    big = float(4 * LANES)
    ninf = -jnp.inf
    gl = jnp.where(lane < N_GROUPS, logits, ninf)
    ge = jnp.exp(gl - jnp.max(gl, axis=1, keepdims=True))
    gprob = ge / jnp.sum(ge, axis=1, keepdims=True)
    gw = jnp.max(gprob, axis=1, keepdims=True)
    gidx = jnp.min(jnp.where(gprob == gw, lane_f, big), axis=1, keepdims=True).astype(jnp.int32)
    in_group = lax.shift_right_arithmetic(lane - N_GROUPS, 3) == gidx
    el = jnp.where(in_group, logits, ninf)
    ee = jnp.exp(el - jnp.max(el, axis=1, keepdims=True))
    ep = jnp.where(in_group, ee / jnp.sum(ee, axis=1, keepdims=True), -1.0)
    p1 = jnp.max(ep, axis=1, keepdims=True)
    i1 = jnp.min(jnp.where(ep == p1, lane_f, big), axis=1, keepdims=True).astype(jnp.int32)
    ep2 = jnp.where(lane == i1, -1.0, ep)
    p2 = jnp.max(ep2, axis=1, keepdims=True)
    i2 = jnp.min(jnp.where(ep2 == p2, lane_f, big), axis=1, keepdims=True).astype(jnp.int32)
    den = p1 + p2
    w1 = gw * (p1 / den)
    w2 = gw * (p2 / den)
    ids_ref[...] = jnp.where(lane == 0, i1 - N_GROUPS, jnp.where(lane == 1, i2 - N_GROUPS, 0))
    wts_ref[...] = jnp.where(lane == 0, w1, jnp.where(lane == 1, w2, 0.0))


def _router(h, w_hi, w_lo, b_r):
    m, d = h.shape
    tm = _pick(m, (512, 256, 128))
    return pl.pallas_call(
        _router_kernel,
        name="router",
        grid=(m // tm,),
        in_specs=[
            pl.BlockSpec((tm, d), lambda i: (i, 0)),
            pl.BlockSpec((d, LANES), lambda i: (0, 0)),
            pl.BlockSpec((d, LANES), lambda i: (0, 0)),
            pl.BlockSpec((1, LANES), lambda i: (0, 0)),
        ],
        out_specs=[pl.BlockSpec((tm, LANES), lambda i: (i, 0)),
                   pl.BlockSpec((tm, LANES), lambda i: (i, 0))],
        out_shape=[jax.ShapeDtypeStruct((m, LANES), jnp.int32),
                   jax.ShapeDtypeStruct((m, LANES), F32)],
        compiler_params=_cparams(1),
    )(h, w_hi, w_lo, b_r)


def _row_copy(src_hbm, src_row, dst_vmem, dst_row, sem):
    return pltpu.make_async_copy(src_hbm.at[pl.ds(src_row, 1)], dst_vmem.at[pl.ds(dst_row, 1)], sem)


def _expert_kernel(ie_ref, row0_ref, nsub_ref, skip_ref, nrows_ref, tok_ref,
                   slab_hbm, wg_ref, wu_ref, wd_ref, y_hbm,
                   slab_buf, x_ref, g_ref, u_ref, a_ref, o_buf, zbuf, pend_ref, gsem, osem, zsem, *, d, nck):
    it = pl.program_id(0)
    j = pl.program_id(1)
    nsub = nsub_ref[it]
    spt = d // LANES
    ck = d // nck
    sub_rows = MOE_BLK * spt
    n_fill = (y_hbm.shape[0] - nrows_ref[0]) // MOE_BLK

    def fill_copy(b):
        row = pl.multiple_of(nrows_ref[0] + b * MOE_BLK, MOE_BLK)
        return pltpu.make_async_copy(zbuf, y_hbm.at[pl.ds(row, MOE_BLK)], zsem)

    def issue_gather(item):
        for sb in range(ITEM_SUB):
            @pl.when(sb < nsub_ref[item])
            def _(sb=sb):
                base = row0_ref[item] + sb * MOE_BLK

                def start(r, c):
                    src = pl.multiple_of(tok_ref[base + r] * spt, spt)
                    dst = pl.multiple_of((sb * MOE_BLK + r) * spt, spt)
                    pltpu.make_async_copy(slab_hbm.at[pl.ds(src, spt)], slab_buf.at[pl.ds(dst, spt)],
                                          gsem.at[sb]).start()
                    return c
                lax.fori_loop(0, MOE_BLK, start, 0, unroll=8)

    def unpack(item):
        for sb in range(ITEM_SUB):
            @pl.when(sb < nsub_ref[item])
            def _(sb=sb):
                pltpu.make_async_copy(slab_hbm.at[pl.ds(0, sub_rows)],
                                      slab_buf.at[pl.ds(sb * sub_rows, sub_rows)], gsem.at[sb]).wait()
                for s in range(spt):
                    col = s * LANES
                    x_ref[col // ck, sb * MOE_BLK:(sb + 1) * MOE_BLK, col % ck:col % ck + LANES] = (
                        slab_buf[pl.ds(sb * sub_rows + s, MOE_BLK, stride=spt), :].astype(BF16))

    @pl.when((it == 0) & (j == 0))
    def _():
        pend_ref[0] = 0
        pend_ref[1] = 0
        issue_gather(0)
        zbuf[...] = jnp.zeros(zbuf.shape, F32)
        lax.fori_loop(0, n_fill, lambda b, c: (fill_copy(b).start(), c)[1], 0)
        unpack(0)

        @pl.when(1 < pl.num_programs(0))
        def _():
            issue_gather(1)

    for n in range(1, ITEM_SUB + 1):
        rows = n * MOE_BLK

        @pl.when((nsub == n) & (j == 0))
        def _(rows=rows):
            xk = x_ref[0, 0:rows, :]
            g_ref[0:rows] = jnp.dot(xk, wg_ref[...].astype(BF16), preferred_element_type=F32)
            u_ref[0:rows] = jnp.dot(xk, wu_ref[...].astype(BF16), preferred_element_type=F32)

        @pl.when((nsub == n) & (j > 0) & (j < nck))
        def _(rows=rows):
            xk = x_ref[j, 0:rows, :]
            g_ref[0:rows] += jnp.dot(xk, wg_ref[...].astype(BF16), preferred_element_type=F32)
            u_ref[0:rows] += jnp.dot(xk, wu_ref[...].astype(BF16), preferred_element_type=F32)

        @pl.when((nsub == n) & (j == nck))
        def _(rows=rows):
            g = g_ref[0:rows]
            a_ref[0:rows] = (g * jax.nn.sigmoid(g) * u_ref[0:rows]).astype(BF16)

    @pl.when(j >= nck)
    def _():
        slot = j % 2
        other = 1 - slot
        col0 = pl.multiple_of((j - nck) * ck, ck)
        for n in range(1, ITEM_SUB + 1):
            rows = n * MOE_BLK

            @pl.when(nsub == n)
            def _(rows=rows):
                o_buf[slot, 0:rows] = jnp.dot(a_ref[0:rows], wd_ref[...].astype(BF16), preferred_element_type=F32)
                pltpu.make_async_copy(o_buf.at[slot, pl.ds(0, rows)],
                                      y_hbm.at[pl.ds(pl.multiple_of(row0_ref[it], MOE_BLK), rows), pl.ds(col0, ck)],
                                      osem.at[slot]).start()

        def drain(s):
            for n in range(1, ITEM_SUB + 1):
                rows = n * MOE_BLK

                @pl.when(pend_ref[s] == n)
                def _(rows=rows):
                    pltpu.make_async_copy(o_buf.at[s, pl.ds(0, rows)],
                                          y_hbm.at[pl.ds(0, rows), pl.ds(0, ck)], osem.at[s]).wait()
            pend_ref[s] = 0

        drain(other)
        pend_ref[slot] = nsub

        @pl.when(j == pl.num_programs(1) - 1)
        def _():
            @pl.when(it + 1 < pl.num_programs(0))
            def _():
                unpack(it + 1)

            @pl.when(it + 2 < pl.num_programs(0))
            def _():
                issue_gather(it + 2)

        @pl.when((it == pl.num_programs(0) - 1) & (j == pl.num_programs(1) - 1))
        def _():
            drain(slot)
            lax.fori_loop(0, n_fill, lambda b, c: (fill_copy(b).wait(), c)[1], 0)


def _experts(tables, buf_tok, slabs, wg, wu, wd, cap):
    item_expert, item_row0, item_nsub, item_skip, n_rows_used = tables
    n_items = item_expert.shape[0]
    n_exp, d, ff = wg.shape
    nck = MOE_CHUNKS
    ck = d // nck
    rows = ITEM_SUB * MOE_BLK
    spt = d // LANES

    def w_in_map(it, j, ie, r0, ns, sk, nr, tok):
        return (ie[it], jnp.where(sk[it] == 1, nck - 1, jnp.minimum(j, nck - 1)), 0)

    def w_out_map(it, j, ie, r0, ns, sk, nr, tok):
        hold = (j < nck) | (sk[it] == 1)
        prev = jnp.maximum(it - 1, 0)
        e = jnp.where((j < nck) & (sk[it] == 0), ie[prev], ie[it])
        return (e, 0, jnp.where(hold, nck - 1, j - nck))

    return pl.pallas_call(
        functools.partial(_expert_kernel, d=d, nck=nck),
        name="moe_experts",
        grid_spec=pltpu.PrefetchScalarGridSpec(
            num_scalar_prefetch=6,
            grid=(n_items, 2 * nck),
            in_specs=[
                pl.BlockSpec(memory_space=pl.ANY),
                pl.BlockSpec((None, ck, ff), w_in_map),
                pl.BlockSpec((None, ck, ff), w_in_map),
                pl.BlockSpec((None, ff, ck), w_out_map),
            ],
            out_specs=pl.BlockSpec(memory_space=pl.ANY),
            scratch_shapes=[
                pltpu.VMEM((rows * spt, LANES), F32),
                pltpu.VMEM((nck, rows, ck), BF16),
                pltpu.VMEM((rows, ff), F32),
                pltpu.VMEM((rows, ff), F32),
                pltpu.VMEM((rows, ff), BF16),
                pltpu.VMEM((2, rows, ck), F32),
                pltpu.VMEM((MOE_BLK, d), F32),
                pltpu.SMEM((2,), jnp.int32),
                pltpu.SemaphoreType.DMA((ITEM_SUB,)),
                pltpu.SemaphoreType.DMA((2,)),
                pltpu.SemaphoreType.DMA(()),
            ],
        ),
        out_shape=jax.ShapeDtypeStruct((cap, d), F32),
        compiler_params=_cparams(2, vmem_mb=EXPERT_VMEM_MB),
    )(item_expert, item_row0, item_nsub, item_skip, n_rows_used, buf_tok, slabs, wg, wu, wd)


def _combine_kernel(dest_ref, y_hbm, h_ref, wts_ref, g_ref, b_ref, o_ref, buf, sem, *, tm, alpha):
    i = pl.program_id(0)
    slot = i % 2

    def issue(tile, dst_slot):
        def start(r, c):
            a = 2 * (tile * tm + r)
            _row_copy(y_hbm, dest_ref[a], buf.at[dst_slot, 0], r, sem.at[dst_slot]).start()
            _row_copy(y_hbm, dest_ref[a + 1], buf.at[dst_slot, 1], r, sem.at[dst_slot]).start()
            return c
        lax.fori_loop(0, tm, start, 0, unroll=8)

    @pl.when(i == 0)
    def _():
        issue(0, 0)

    @pl.when(i + 1 < pl.num_programs(0))
    def _():
        issue(i + 1, 1 - slot)

    for k in range(2):
        pltpu.make_async_copy(y_hbm.at[pl.ds(0, tm)], buf.at[slot, k], sem.at[slot]).wait()
    wts = wts_ref[...]
    f = buf[slot, 0] * wts[:, 0:1] + buf[slot, 1] * wts[:, 1:2]
    o_ref[...] = _layer_norm(alpha * h_ref[...] + f, g_ref[...], b_ref[...])


def _combine_ln(dest, yb, h, wts, g, b, alpha):
    m, d = h.shape
    tm = _pick(m, (256, 128))
    return pl.pallas_call(
        functools.partial(_combine_kernel, tm=tm, alpha=alpha),
        name="moe_combine",
        grid_spec=pltpu.PrefetchScalarGridSpec(
            num_scalar_prefetch=1,
            grid=(m // tm,),
            in_specs=[
                pl.BlockSpec(memory_space=pl.ANY),
                pl.BlockSpec((tm, d), lambda i, dst: (i, 0)),
                pl.BlockSpec((tm, LANES), lambda i, dst: (i, 0)),
                pl.BlockSpec((1, d), lambda i, dst: (0, 0)),
                pl.BlockSpec((1, d), lambda i, dst: (0, 0)),
            ],
            out_specs=pl.BlockSpec((tm, d), lambda i, dst: (i, 0)),
            scratch_shapes=[pltpu.VMEM((2, 2, tm, d), F32), pltpu.SemaphoreType.DMA((2,))],
        ),
        out_shape=jax.ShapeDtypeStruct((m, d), F32),
        compiler_params=_cparams(1),
    )(dest, yb, h, wts, g, b)


def _dispatch_tables(ids, n_tok):
    flat_e = ids[:, :2].reshape(-1)
    n_assign = flat_e.shape[0]
    onehot = (flat_e[:, None] == jnp.arange(N_EXPERTS, dtype=jnp.int32)[None, :]).astype(jnp.int32)
    csum = jnp.cumsum(onehot, axis=0)
    rank = jnp.sum(onehot * csum, axis=1) - 1
    counts = csum[-1]
    padded = (counts + MOE_BLK - 1) // MOE_BLK * MOE_BLK
    pend = jnp.cumsum(padded)
    pstart = pend - padded
    dest = (pstart[flat_e] + rank).astype(jnp.int32)
    cap = -(-n_assign // MOE_BLK) * MOE_BLK + N_EXPERTS * MOE_BLK
    buf_tok = jnp.zeros((cap,), jnp.int32).at[dest].set(jnp.arange(n_assign, dtype=jnp.int32) // 2)

    item_rows = ITEM_SUB * MOE_BLK
    items_e = (padded + item_rows - 1) // item_rows
    item_end = jnp.cumsum(items_e)
    item_start = item_end - items_e
    n_used = item_end[-1]
    n_items = -(-N_EXPERTS * (ITEM_SUB - 1) // ITEM_SUB) + cap // item_rows + 1
    it = jnp.arange(n_items, dtype=jnp.int32)
    e_it = jnp.minimum(jnp.sum((item_end[None, :] <= it[:, None]).astype(jnp.int32), axis=1), N_EXPERTS - 1)
    k_it = it - item_start[e_it]
    used = it < n_used
    e_last = e_it[jnp.maximum(n_used - 1, 0)]
    item_expert = jnp.where(used, e_it, e_last).astype(jnp.int32)
    item_row0 = jnp.where(used, pstart[e_it] + k_it * item_rows, 0).astype(jnp.int32)
    item_nsub = jnp.where(used, jnp.clip((padded[e_it] - k_it * item_rows) // MOE_BLK, 0, ITEM_SUB), 0)
    item_skip = jnp.where(used, 0, 1).astype(jnp.int32)
    n_rows_used = pend[-1].astype(jnp.int32).reshape(1)
    return dest, buf_tok, (item_expert, item_row0, item_nsub.astype(jnp.int32), item_skip, n_rows_used), cap


def _rope_tables(seq):
    half = MLA_ROPE // 2
    freqs = ROPE_THETA ** (-jnp.arange(half, dtype=F32) / half)
    ang = jnp.arange(seq, dtype=F32)[:, None] * freqs
    cos = jnp.cos(ang)
    sin = jnp.sin(ang)
    return jnp.concatenate([cos, cos], -1), jnp.concatenate([sin, sin], -1)


def _swap_halves_neg(w):
    half = w.shape[-1] // 2
    return jnp.concatenate([-w[..., half:], w[..., :half]], axis=-1)


def _layer(x, w_in, w_ukv, g_ckv, w_pa, w_pb, w_o, ln1_g, ln1_b, w_rg, b_rg, w_re, b_re,
           w_gate_e, w_up_e, w_down_e, ln2_g, ln2_b, alpha):
    batch, seq, d = x.shape
    n_tok = batch * seq
    x2 = x.reshape(n_tok, d)

    splits = (DSA_Q_DIM, DSA_KV_DIM, DSA_KV_DIM, IDX_Q_DIM, IDX_DIM, IDX_HEADS, MLA_Q_DIM, KV_RANK, MLA_ROPE, 2 * d)
    offs = np.cumsum((0,) + splits)
    w_dq, w_dk, w_dv, w_iq, w_ik, w_iw, w_mq, w_ckv, w_kr, w_gates = [
        w_in[:, offs[j]:offs[j + 1]] for j in range(len(splits))]
    w_mq3 = w_mq.reshape(d, MLA_HEADS, MLA_QK_DIM)
    w_qn = w_mq3[:, :, :MLA_NOPE].reshape(d, MLA_HEADS * MLA_NOPE)
    w_qr = w_mq3[:, :, MLA_NOPE:]
    w_qs = _swap_halves_neg(w_qr)
    wr = MLA_HEADS * MLA_ROPE
    wb = jnp.concatenate([w_iq, w_dk, w_dv], axis=1).astype(BF16)
    off_k = IDX_Q_DIM
    off_v = off_k + DSA_KV_DIM
    pad_i = jnp.zeros((d, LANES - IDX_DIM - IDX_HEADS), F32)
    ws = jnp.concatenate([w_dq, w_qn, w_qr.reshape(d, wr), w_qs.reshape(d, wr), w_ckv, w_ik, w_iw, pad_i,
                          w_kr, _swap_halves_neg(w_kr)], axis=1).astype(BF16)
    off_qn = DSA_Q_DIM
    off_qr = off_qn + MLA_HEADS * MLA_NOPE
    off_qs = off_qr + wr
    off_c = off_qs + wr
    off_i = off_c + KV_RANK
    off_kr = off_i + LANES

    zb, xb16 = _matmul_cast(x2, wb, BF16)
    zs = _matmul(xb16, ws, F32, tn=ws.shape[1] // 2)
    gates = _matmul(xb16, w_gates.astype(BF16), F32, tn=_pick(2 * d, (2048, 1024, 512, 256)))

    kidx = zs[:, off_i:off_i + IDX_DIM].astype(BF16).reshape(batch, seq, IDX_DIM)
    w_idx = zs[:, off_i + IDX_DIM:off_i + IDX_DIM + IDX_HEADS].reshape(batch, seq, IDX_HEADS)
    o_a = _dsa_attention(zs, zb, w_idx, kidx, batch, seq, off_k, off_v)

    cos2, sin2 = _rope_tables(seq)
    w_ukv3 = w_ukv.reshape(KV_RANK, MLA_HEADS, MLA_NOPE + MLA_V)
    w_kv = jnp.concatenate([w_ukv3[:, :, :MLA_NOPE].reshape(KV_RANK, MLA_HEADS * MLA_NOPE),
                            w_ukv3[:, :, MLA_NOPE:].reshape(KV_RANK, MLA_HEADS * MLA_V)], axis=1).astype(BF16)
    q_cat = _mla_q_prep(zs, jnp.tile(cos2, (1, MLA_HEADS)), jnp.tile(sin2, (1, MLA_HEADS)),
                        batch, seq, off_qn, off_qr, off_qs)
    k_cat, v_mla = _mla_kv_prep(zs, g_ckv.reshape(1, KV_RANK), w_kv, cos2, sin2, batch, seq, off_c, off_kr)
    o_b = _mla_flash(q_cat, k_cat, v_mla, batch, seq).reshape(n_tok, MLA_OUT_DIM)

    merged = _gated_merge(o_a, o_b, w_pa.astype(BF16), w_pb.astype(BF16), gates)
    h, slabs = _proj_ln(merged, w_o.astype(BF16), x2, ln1_g.reshape(1, d), ln1_b.reshape(1, d), alpha)

    pad_r = jnp.zeros((d, LANES - N_GROUPS - N_EXPERTS), F32)
    w_r = jnp.concatenate([w_rg, w_re, pad_r], axis=1)
    w_r_hi = w_r.astype(BF16)
    w_r_lo = (w_r - w_r_hi.astype(F32)).astype(BF16)
    b_r = jnp.concatenate([b_rg, b_re, jnp.zeros((LANES - N_GROUPS - N_EXPERTS,), F32)]).reshape(1, LANES)
    ids, wts = _router(h, w_r_hi, w_r_lo, b_r)
    dest, buf_tok, item_tables, cap = _dispatch_tables(ids, n_tok)
    yb = _experts(item_tables, buf_tok, slabs, w_gate_e, w_up_e, w_down_e, cap)
    out = _combine_ln(dest, yb, h, wts, ln2_g.reshape(1, d), ln2_b.reshape(1, d), alpha)
    return out.reshape(batch, seq, d)


def kernel(x, w_in, w_ukv, g_ckv, w_pa, w_pb, w_o, ln1_g, ln1_b, w_rg, b_rg, w_re, b_re,
           w_gate_e, w_up_e, w_down_e, ln2_g, ln2_b):
    depth = w_in.shape[0]
    alpha = (2.0 * depth) ** 0.25
    for l in range(depth):
        x = _layer(x, w_in[l], w_ukv[l], g_ckv[l], w_pa[l], w_pb[l], w_o[l], ln1_g[l], ln1_b[l],
                   w_rg[l], b_rg[l], w_re[l], b_re[l], w_gate_e[l], w_up_e[l], w_down_e[l],
                   ln2_g[l], ln2_b[l], alpha)
    return x
```

```python
import functools

import numpy as np
import jax
import jax.numpy as jnp
from jax import lax
from jax.experimental import pallas as pl
from jax.experimental.pallas import tpu as pltpu

N_DSA_HEADS = 8
DSA_HEAD_DIM = 128
DSA_KV_HEADS = 2
DSA_REP = N_DSA_HEADS // DSA_KV_HEADS
IDX_HEADS = 16
IDX_DIM = 64
TOPK_MAX = 256
MLA_HEADS = 8
MLA_NOPE = 128
MLA_ROPE = 64
MLA_V = 128
KV_RANK = 512
ROPE_THETA = 10000.0
N_GROUPS = 8
EXPERTS_PER_GROUP = 8
N_EXPERTS = N_GROUPS * EXPERTS_PER_GROUP
LN_EPS = 1e-5
RMS_EPS = 1e-6

DSA_Q_DIM = N_DSA_HEADS * DSA_HEAD_DIM
DSA_KV_DIM = DSA_KV_HEADS * DSA_HEAD_DIM
IDX_Q_DIM = IDX_HEADS * IDX_DIM
MLA_QK_DIM = MLA_NOPE + MLA_ROPE
MLA_Q_DIM = MLA_HEADS * MLA_QK_DIM
MLA_OUT_DIM = MLA_HEADS * MLA_V

LANES = 128
VMEM_LIMIT_MB = 56

DSA_TQ = 128
DSA_KC = 512
MLA_T = 512
MOE_BLK = 256
ITEM_SUB = 3
MOE_CHUNKS = 4
W_SLOTS = 3
EXPERT_VMEM_MB = 58
NEG = -1e30
LOG2E = 1.4426950408889634
INT_MIN = -2 ** 31
FLT_MAX = 3.4028234663852886e38

F32 = jnp.float32
BF16 = jnp.bfloat16


def _cparams(n_axes, vmem_mb=VMEM_LIMIT_MB):
    return pltpu.CompilerParams(dimension_semantics=("arbitrary",) * n_axes,
                                vmem_limit_bytes=vmem_mb * 1024 * 1024)


def _pick(n, prefs):
    for p in prefs:
        if n % p == 0:
            return p
    return n


def _lane_tiles(x):
    return [x[:, j * LANES:(j + 1) * LANES] for j in range(x.shape[1] // LANES)]


def _row_max(x):
    return jnp.max(functools.reduce(jnp.maximum, _lane_tiles(x)), axis=1, keepdims=True)


def _row_sum(x):
    return jnp.sum(functools.reduce(jnp.add, _lane_tiles(x)), axis=1, keepdims=True)


def _mm_kernel(a_ref, w_ref, o_ref):
    o_ref[...] = jnp.dot(a_ref[...], w_ref[...], preferred_element_type=F32).astype(o_ref.dtype)


def _matmul(a, w, out_dtype, tn):
    m, k = a.shape
    n = w.shape[1]
    tm = _pick(m, (512, 256, 128))
    return pl.pallas_call(
        _mm_kernel,
        grid=(n // tn, m // tm),
        in_specs=[pl.BlockSpec((tm, k), lambda j, i: (i, 0)),
                  pl.BlockSpec((k, tn), lambda j, i: (0, j))],
        out_specs=pl.BlockSpec((tm, tn), lambda j, i: (i, j)),
        out_shape=jax.ShapeDtypeStruct((m, n), out_dtype),
        compiler_params=_cparams(2),
    )(a, w)


def _mm_cast_kernel(a_ref, w_ref, o_ref, a16_ref):
    a16 = a_ref[...].astype(BF16)
    a16_ref[...] = a16
    o_ref[...] = jnp.dot(a16, w_ref[...], preferred_element_type=F32).astype(o_ref.dtype)


def _matmul_cast(a, w, out_dtype):
    m, k = a.shape
    n = w.shape[1]
    tm = _pick(m, (512, 256, 128))
    return pl.pallas_call(
        _mm_cast_kernel,
        grid=(m // tm,),
        in_specs=[pl.BlockSpec((tm, k), lambda i: (i, 0)),
                  pl.BlockSpec((k, n), lambda i: (0, 0))],
        out_specs=[pl.BlockSpec((tm, n), lambda i: (i, 0)),
                   pl.BlockSpec((tm, k), lambda i: (i, 0))],
        out_shape=[jax.ShapeDtypeStruct((m, n), out_dtype), jax.ShapeDtypeStruct((m, k), BF16)],
        compiler_params=_cparams(1),
    )(a, w)


def _dsa_kernel(q_ref, iq_ref, w_ref, kidx_ref, k_ref, v_ref, o_ref,
                sc_ref, thr_ref, jcut_ref, s_ref, m_ref, l_ref, acc_ref, *, tq, kc, n_sel, seq):
    i = pl.program_id(1)
    nch = (i * tq + tq + kc - 1) // kc
    row = lax.broadcasted_iota(jnp.int32, (tq, kc), 0)
    lane = lax.broadcasted_iota(jnp.int32, (tq, kc), 1)
    qpos = i * tq + row

    iq_all = iq_ref[...]
    iq = jnp.concatenate([iq_all[:, h * IDX_DIM:(h + 1) * IDX_DIM] for h in range(IDX_HEADS)],
                         axis=0)
    w = w_ref[0] * (IDX_DIM ** -0.5 * IDX_HEADS ** -0.5)

    def score_body(c, carry):
        off = pl.multiple_of(c * kc, kc)
        kx = kidx_ref[0, pl.ds(off, kc), :]
        logits = lax.dot_general(iq, kx, (((1,), (1,)), ((), ())), preferred_element_type=F32)
        sc = jnp.zeros((tq, kc), F32)
        for h in range(IDX_HEADS):
            sc = sc + jnp.maximum(logits[h * tq:(h + 1) * tq], 0.0) * w[:, h:h + 1]
        sc_ref[c] = jnp.where(off + lane <= qpos, sc, -jnp.inf)
        return carry

    lax.fori_loop(0, nch, score_body, 0)

    def key_to_f32(key):
        return pltpu.bitcast(jnp.where(key < 0, key ^ jnp.int32(0x7FFFFFFF), key), F32)

    key_lowest = INT_MIN + 0x00800000
    thr_ref[...] = jnp.full((tq, 1), -FLT_MAX, F32)
    jcut_ref[...] = jnp.full((tq, 1), seq, jnp.int32)

    def count(pred_fn):
        def body(c, acc):
            sch = sc_ref[c]
            for j in range(kc // LANES):
                acc = acc + pred_fn(sch[:, j * LANES:(j + 1) * LANES], c * kc + j * LANES)
            return acc
        acc = lax.fori_loop(0, nch, body, jnp.zeros((tq, LANES), jnp.int32))
        return jnp.sum(acc.astype(F32), axis=1, keepdims=True).astype(jnp.int32)

    lane1 = lax.broadcasted_iota(jnp.int32, (tq, LANES), 1)

    @pl.when(i * tq + tq > n_sel)
    def _():
        def bit_body(b, t):
            cand = t + lax.shift_left(jnp.int32(1), 31 - b)
            cand_b = jnp.broadcast_to(key_to_f32(cand), (tq, LANES))
            cnt = count(lambda sch, _: jnp.where(sch >= cand_b, 1, 0))
            return jnp.where(cnt >= n_sel, cand, t)

        t = lax.fori_loop(0, 32, bit_body, jnp.full((tq, 1), INT_MIN, jnp.int32))
        t_f = key_to_f32(jnp.maximum(t, key_lowest))
        thr_ref[...] = t_f
        t_b = jnp.broadcast_to(t_f, (tq, LANES))
        c_ge = count(lambda sch, _: jnp.where(sch >= t_b, 1, 0))
        c_gt = count(lambda sch, _: jnp.where(sch > t_b, 1, 0))
        excess = c_ge > n_sel
        need = n_sel - c_gt

        @pl.when(jnp.max(jnp.where(excess, 1.0, 0.0)) > 0.0)
        def _():
            nbits = max(1, int(np.ceil(np.log2(seq))))

            def jbit(b, jc):
                cand = jc + lax.shift_left(jnp.int32(1), nbits - 1 - b)
                cand_b = jnp.broadcast_to(cand, (tq, LANES))
                cnt = count(lambda sch, base: jnp.where(
                    sch == t_b, jnp.where(base + lane1 < cand_b, 1, 0), 0))
                return jnp.where(cnt < need, cand, jc)

            jc = lax.fori_loop(0, nbits, jbit, jnp.zeros((tq, 1), jnp.int32))
            jcut_ref[...] = jnp.where(excess, jc, seq)

    qs = (q_ref[...] * (DSA_HEAD_DIM ** -0.5 * LOG2E)).astype(BF16)
    qg = [jnp.concatenate([qs[:, (g * DSA_REP + r) * DSA_HEAD_DIM:(g * DSA_REP + r + 1) * DSA_HEAD_DIM]
                           for r in range(DSA_REP)], axis=0) for g in range(DSA_KV_HEADS)]
    m_ref[...] = jnp.full(m_ref.shape, NEG, F32)
    l_ref[...] = jnp.zeros(l_ref.shape, F32)
    acc_ref[...] = jnp.zeros(acc_ref.shape, F32)
    t = thr_ref[...]
    jc = jcut_ref[...]

    def scores(c, slot):
        off = pl.multiple_of(c * kc, kc)
        for g in range(DSA_KV_HEADS):
            s_ref[slot, g] = lax.dot_general(
                qg[g], k_ref[pl.ds(off, kc), g * DSA_HEAD_DIM:(g + 1) * DSA_HEAD_DIM],
                (((1,), (1,)), ((), ())), preferred_element_type=F32)

    def update(c, slot):
        off = pl.multiple_of(c * kc, kc)
        sc = sc_ref[c]
        kpos = off + lane
        sel = jnp.where(kpos <= jc, jnp.where(sc >= t, 1, 0), jnp.where(sc > t, 1, 0)) > 0
        md = jnp.where(sel, (qpos - kpos).astype(F32), -NEG)
        for g in range(DSA_KV_HEADS):
            s = s_ref[slot, g]
            vg = v_ref[pl.ds(off, kc), g * DSA_HEAD_DIM:(g + 1) * DSA_HEAD_DIM]
            e = jnp.concatenate(
                [s[r * tq:(r + 1) * tq] - md * (LOG2E * 2.0 ** (-8.0 * (g * DSA_REP + r + 1) / N_DSA_HEADS))
                 for r in range(DSA_REP)], axis=0)
            m_old = m_ref[g]
            m_new = jnp.maximum(m_old, _row_max(e))
            p = jnp.exp2(e - m_new)
            alpha = jnp.exp2(m_old - m_new)
            l_ref[g] = alpha * l_ref[g] + _row_sum(p)
            acc_ref[g] = alpha * acc_ref[g] + jnp.dot(p.astype(BF16), vg, preferred_element_type=F32)
            m_ref[g] = m_new

    scores(0, 0)
    n_pairs = (nch - 1) // 2

    def att_body(p, carry):
        c = 2 * p
        scores(c + 1, 1)
        update(c, 0)
        scores(c + 2, 0)
        update(c + 1, 1)
        return carry

    lax.fori_loop(0, n_pairs, att_body, 0)
    c_tail = 2 * n_pairs

    @pl.when(nch - c_tail == 2)
    def _():
        scores(c_tail + 1, 1)
        update(c_tail, 0)
        update(c_tail + 1, 1)

    @pl.when(nch - c_tail == 1)
    def _():
        update(c_tail, 0)

    for g in range(DSA_KV_HEADS):
        o = acc_ref[g] / l_ref[g]
        for r in range(DSA_REP):
            hh = g * DSA_REP + r
            o_ref[:, hh * DSA_HEAD_DIM:(hh + 1) * DSA_HEAD_DIM] = o[r * tq:(r + 1) * tq].astype(o_ref.dtype)


def _dsa_attention(zs, zb, w_idx, kidx, batch, seq, off_k, off_v):
    tq = min(DSA_TQ, seq)
    kc = min(DSA_KC, seq)
    nq = seq // tq
    n_sel = min(TOPK_MAX, seq // 4)
    kern = functools.partial(_dsa_kernel, tq=tq, kc=kc, n_sel=n_sel, seq=seq)
    return pl.pallas_call(
        kern,
        name="dsa",
        grid=(batch, nq),
        in_specs=[
            pl.BlockSpec((tq, DSA_Q_DIM), lambda b, i: (b * nq + i, 0)),
            pl.BlockSpec((tq, IDX_Q_DIM), lambda b, i: (b * nq + i, 0)),
            pl.BlockSpec((1, tq, IDX_HEADS), lambda b, i: (b, i, 0)),
            pl.BlockSpec((1, seq, IDX_DIM), lambda b, i: (b, 0, 0)),
            pl.BlockSpec((seq, DSA_KV_DIM), lambda b, i: (b, off_k // DSA_KV_DIM)),
            pl.BlockSpec((seq, DSA_KV_DIM), lambda b, i: (b, off_v // DSA_KV_DIM)),
        ],
        out_specs=pl.BlockSpec((tq, DSA_Q_DIM), lambda b, i: (b * nq + i, 0)),
        out_shape=jax.ShapeDtypeStruct((batch * seq, DSA_Q_DIM), BF16),
        scratch_shapes=[
            pltpu.VMEM((seq // kc, tq, kc), F32),
            pltpu.VMEM((tq, 1), F32),
            pltpu.VMEM((tq, 1), jnp.int32),
            pltpu.VMEM((2, DSA_KV_HEADS, DSA_REP * tq, kc), F32),
            pltpu.VMEM((DSA_KV_HEADS, DSA_REP * tq, 1), F32),
            pltpu.VMEM((DSA_KV_HEADS, DSA_REP * tq, 1), F32),
            pltpu.VMEM((DSA_KV_HEADS, DSA_REP * tq, DSA_HEAD_DIM), F32),
        ],
        compiler_params=_cparams(2),
    )(zs, zb, w_idx, kidx, zb, zb)


def _mla_q_kernel(qn_ref, qr_ref, qs_ref, cos_ref, sin_ref, o_ref):
    c = MLA_QK_DIM ** -0.5 * LOG2E
    pe = ((qr_ref[...] * cos_ref[...] + qs_ref[...] * sin_ref[...]) * c).astype(o_ref.dtype)
    qn = (qn_ref[...] * c).astype(o_ref.dtype)
    for h in range(MLA_HEADS):
        o_ref[0, h, :, 0:MLA_NOPE] = qn[:, h * MLA_NOPE:(h + 1) * MLA_NOPE]
        o_ref[0, h, :, MLA_NOPE:MLA_QK_DIM] = pe[:, h * MLA_ROPE:(h + 1) * MLA_ROPE]


def _mla_q_prep(zs, cos_q, sin_q, batch, seq, off_qn, off_qr, off_qs):
    tm = _pick(seq, (512, 256, 128))
    ns = seq // tm
    wr = MLA_HEADS * MLA_ROPE
    return pl.pallas_call(
        _mla_q_kernel,
        name="mla_q",
        grid=(batch * ns,),
        in_specs=[
            pl.BlockSpec((tm, MLA_HEADS * MLA_NOPE), lambda i: (i, off_qn // (MLA_HEADS * MLA_NOPE))),
            pl.BlockSpec((tm, wr), lambda i: (i, off_qr // wr)),
            pl.BlockSpec((tm, wr), lambda i: (i, off_qs // wr)),
            pl.BlockSpec((tm, wr), lambda i: (i % ns, 0)),
            pl.BlockSpec((tm, wr), lambda i: (i % ns, 0)),
        ],
        out_specs=pl.BlockSpec((1, MLA_HEADS, tm, MLA_QK_DIM), lambda i: (i // ns, 0, i % ns, 0)),
        out_shape=jax.ShapeDtypeStruct((batch, MLA_HEADS, seq, MLA_QK_DIM), BF16),
        compiler_params=_cparams(1),
    )(zs, zs, zs, cos_q, sin_q)


def _mla_kv_kernel(c_ref, kr_ref, g_ref, w_ref, cos_ref, sin_ref, k_ref, v_ref):
    c = c_ref[...]
    cn = c * lax.rsqrt(jnp.mean(c * c, axis=-1, keepdims=True) + RMS_EPS) * g_ref[...]
    kv = jnp.dot(cn.astype(BF16), w_ref[...], preferred_element_type=F32)
    kr = kr_ref[...]
    pe = (kr[:, 0:MLA_ROPE] * cos_ref[...] + kr[:, MLA_ROPE:2 * MLA_ROPE] * sin_ref[...]).astype(k_ref.dtype)
    for h in range(MLA_HEADS):
        k_ref[0, h, :, 0:MLA_NOPE] = kv[:, h * MLA_NOPE:(h + 1) * MLA_NOPE].astype(k_ref.dtype)
        k_ref[0, h, :, MLA_NOPE:MLA_QK_DIM] = pe
        v0 = MLA_HEADS * MLA_NOPE + h * MLA_V
        v_ref[0, h] = kv[:, v0:v0 + MLA_V].astype(v_ref.dtype)


def _mla_kv_prep(zs, g_ckv, w_kv, cos_k, sin_k, batch, seq, off_c, off_kr):
    tm = _pick(seq, (512, 256, 128))
    ns = seq // tm
    nkv = MLA_HEADS * (MLA_NOPE + MLA_V)
    return pl.pallas_call(
        _mla_kv_kernel,
        name="mla_kv",
        grid=(batch * ns,),
        in_specs=[
            pl.BlockSpec((tm, KV_RANK), lambda i: (i, off_c // KV_RANK)),
            pl.BlockSpec((tm, 2 * MLA_ROPE), lambda i: (i, off_kr // (2 * MLA_ROPE))),
            pl.BlockSpec((1, KV_RANK), lambda i: (0, 0)),
            pl.BlockSpec((KV_RANK, nkv), lambda i: (0, 0)),
            pl.BlockSpec((tm, MLA_ROPE), lambda i: (i % ns, 0)),
            pl.BlockSpec((tm, MLA_ROPE), lambda i: (i % ns, 0)),
        ],
        out_specs=[
            pl.BlockSpec((1, MLA_HEADS, tm, MLA_QK_DIM), lambda i: (i // ns, 0, i % ns, 0)),
            pl.BlockSpec((1, MLA_HEADS, tm, MLA_V), lambda i: (i // ns, 0, i % ns, 0)),
        ],
        out_shape=[jax.ShapeDtypeStruct((batch, MLA_HEADS, seq, MLA_QK_DIM), BF16),
                   jax.ShapeDtypeStruct((batch, MLA_HEADS, seq, MLA_V), BF16)],
        compiler_params=_cparams(1),
    )(zs, zs, g_ckv, w_kv, cos_k, sin_k)


def _mla_flash_kernel(q_ref, k_ref, v_ref, o_ref, s_ref, m_ref, l_ref, acc_ref, *, t):
    qi = pl.program_id(2)
    q = q_ref[0, 0]
    m_ref[...] = jnp.full(m_ref.shape, NEG, F32)
    l_ref[...] = jnp.zeros(l_ref.shape, F32)
    acc_ref[...] = jnp.zeros(acc_ref.shape, F32)

    def scores(j, slot):
        off = pl.multiple_of(j * t, t)
        s_ref[slot] = lax.dot_general(q, k_ref[0, 0, pl.ds(off, t), :], (((1,), (1,)), ((), ())),
                                      preferred_element_type=F32)

    def update(j, slot, diagonal):
        off = pl.multiple_of(j * t, t)
        s = s_ref[slot]
        if diagonal:
            row = lax.broadcasted_iota(jnp.int32, (t, t), 0)
            col = lax.broadcasted_iota(jnp.int32, (t, t), 1)
            s = jnp.where(col <= row, s, NEG)
        m_old = m_ref[...]
        m_new = jnp.maximum(m_old, _row_max(s))
        p = jnp.exp2(s - m_new)
        alpha = jnp.exp2(m_old - m_new)
        l_ref[...] = alpha * l_ref[...] + _row_sum(p)
        acc_ref[...] = alpha * acc_ref[...] + jnp.dot(p.astype(BF16), v_ref[0, 0, pl.ds(off, t), :],
                                                      preferred_element_type=F32)
        m_ref[...] = m_new

    scores(0, 0)
    n_pairs = qi // 2

    def body(p, carry):
        j = 2 * p
        scores(j + 1, 1)
        update(j, 0, False)
        scores(j + 2, 0)
        update(j + 1, 1, False)
        return carry

    lax.fori_loop(0, n_pairs, body, 0)
    j_tail = 2 * n_pairs

    @pl.when(qi - j_tail == 1)
    def _():
        scores(j_tail + 1, 1)
        update(j_tail, 0, False)
        update(j_tail + 1, 1, True)

    @pl.when(qi == j_tail)
    def _():
        update(j_tail, 0, True)

    o_ref[0] = (acc_ref[...] / l_ref[...]).astype(o_ref.dtype)


def _mla_flash(q_cat, k_cat, v, batch, seq):
    t = min(MLA_T, seq)
    n = seq // t
    kern = functools.partial(_mla_flash_kernel, t=t)
    return pl.pallas_call(
        kern,
        name="mla_flash",
        grid=(batch, MLA_HEADS, n),
        in_specs=[
            pl.BlockSpec((1, 1, t, MLA_QK_DIM), lambda b, h, qi: (b, h, qi, 0)),
            pl.BlockSpec((1, 1, seq, MLA_QK_DIM), lambda b, h, qi: (b, h, 0, 0)),
            pl.BlockSpec((1, 1, seq, MLA_V), lambda b, h, qi: (b, h, 0, 0)),
        ],
        out_specs=pl.BlockSpec((1, t, MLA_V), lambda b, h, qi: (b, qi, h)),
        out_shape=jax.ShapeDtypeStruct((batch, seq, MLA_OUT_DIM), BF16),
        scratch_shapes=[pltpu.VMEM((2, t, t), F32), pltpu.VMEM((t, 1), F32), pltpu.VMEM((t, 1), F32),
                        pltpu.VMEM((t, MLA_V), F32)],
        compiler_params=_cparams(3),
    )(q_cat, k_cat, v)


def _merge_kernel(oa_ref, ob_ref, wa_ref, wb_ref, ga_ref, gb_ref, o_ref):
    pa = jnp.dot(oa_ref[...], wa_ref[...], preferred_element_type=F32)
    pb = jnp.dot(ob_ref[...], wb_ref[...], preferred_element_type=F32)
    o_ref[...] = (jax.nn.sigmoid(ga_ref[...]) * pa + jax.nn.sigmoid(gb_ref[...]) * pb).astype(o_ref.dtype)


def _gated_merge(o_a, o_b, w_pa, w_pb, gates):
    m = o_a.shape[0]
    d = w_pa.shape[1]
    tm = _pick(m, (512, 256, 128))
    tn = _pick(d, (1024, 512, 256, 128))
    nn = d // tn
    return pl.pallas_call(
        _merge_kernel,
        name="gated_merge",
        grid=(nn, m // tm),
        in_specs=[
            pl.BlockSpec((tm, o_a.shape[1]), lambda j, i: (i, 0)),
            pl.BlockSpec((tm, o_b.shape[1]), lambda j, i: (i, 0)),
            pl.BlockSpec((w_pa.shape[0], tn), lambda j, i: (0, j)),
            pl.BlockSpec((w_pb.shape[0], tn), lambda j, i: (0, j)),
            pl.BlockSpec((tm, tn), lambda j, i: (i, j)),
            pl.BlockSpec((tm, tn), lambda j, i: (i, j + nn)),
        ],
        out_specs=pl.BlockSpec((tm, tn), lambda j, i: (i, j)),
        out_shape=jax.ShapeDtypeStruct((m, d), BF16),
        compiler_params=_cparams(2),
    )(o_a, o_b, w_pa, w_pb, gates, gates)


def _layer_norm(y, g, b):
    mu = jnp.mean(y, axis=-1, keepdims=True)
    dlt = y - mu
    var = jnp.mean(dlt * dlt, axis=-1, keepdims=True)
    return dlt * lax.rsqrt(var + LN_EPS) * g + b


def _proj_ln_kernel(a_ref, w_ref, x_ref, g_ref, b_ref, o_ref, slab_ref, *, alpha):
    y = alpha * x_ref[...] + jnp.dot(a_ref[...], w_ref[...], preferred_element_type=F32)
    h = _layer_norm(y, g_ref[...], b_ref[...])
    o_ref[...] = h
    tm, d = h.shape
    spt = d // LANES
    for s in range(spt):
        slab_ref[pl.ds(s, tm, stride=spt), :] = h[:, s * LANES:(s + 1) * LANES]


def _proj_ln(a, w, x, g, b, alpha):
    m, d = x.shape
    tm = _pick(m, (256, 128))
    spt = d // LANES
    return pl.pallas_call(
        functools.partial(_proj_ln_kernel, alpha=alpha),
        name="proj_ln",
        grid=(m // tm,),
        in_specs=[
            pl.BlockSpec((tm, a.shape[1]), lambda i: (i, 0)),
            pl.BlockSpec(w.shape, lambda i: (0, 0)),
            pl.BlockSpec((tm, d), lambda i: (i, 0)),
            pl.BlockSpec((1, d), lambda i: (0, 0)),
            pl.BlockSpec((1, d), lambda i: (0, 0)),
        ],
        out_specs=[pl.BlockSpec((tm, d), lambda i: (i, 0)),
                   pl.BlockSpec((tm * spt, LANES), lambda i: (i, 0))],
        out_shape=[jax.ShapeDtypeStruct((m, d), F32),
                   jax.ShapeDtypeStruct((m * spt, LANES), F32)],
        compiler_params=_cparams(1),
    )(a, w, x, g, b)


def _router_kernel(h_ref, whi_ref, wlo_ref, b_ref, ids_ref, wts_ref):
    h = h_ref[...]
    h_hi = h.astype(BF16)
    h_lo = (h - h_hi.astype(F32)).astype(BF16)
    logits = (jnp.dot(h_hi, whi_ref[...], preferred_element_type=F32)
              + jnp.dot(h_hi, wlo_ref[...], preferred_element_type=F32)
              + jnp.dot(h_lo, whi_ref[...], preferred_element_type=F32)) + b_ref[...]
    tm = h.shape[0]
    lane = lax.broadcasted_iota(jnp.int32, (tm, LANES), 1)
    lane_f = lane.astype(F32)
    big = float(4 * LANES)
    ninf = -jnp.inf
    gl = jnp.where(lane < N_GROUPS, logits, ninf)
    ge = jnp.exp(gl - jnp.max(gl, axis=1, keepdims=True))
    gprob = ge / jnp.sum(ge, axis=1, keepdims=True)
    gw = jnp.max(gprob, axis=1, keepdims=True)
    gidx = jnp.min(jnp.where(gprob == gw, lane_f, big), axis=1, keepdims=True).astype(jnp.int32)
    in_group = lax.shift_right_arithmetic(lane - N_GROUPS, 3) == gidx
    el = jnp.where(in_group, logits, ninf)
    ee = jnp.exp(el - jnp.max(el, axis=1, keepdims=True))
    ep = jnp.where(in_group, ee / jnp.sum(ee, axis=1, keepdims=True), -1.0)
    p1 = jnp.max(ep, axis=1, keepdims=True)
    i1 = jnp.min(jnp.where(ep == p1, lane_f, big), axis=1, keepdims=True).astype(jnp.int32)
    ep2 = jnp.where(lane == i1, -1.0, ep)
    p2 = jnp.max(ep2, axis=1, keepdims=True)
    i2 = jnp.min(jnp.where(ep2 == p2, lane_f, big), axis=1, keepdims=True).astype(jnp.int32)
    den = p1 + p2
    w1 = gw * (p1 / den)
    w2 = gw * (p2 / den)
    ids_ref[...] = jnp.where(lane == 0, i1 - N_GROUPS, jnp.where(lane == 1, i2 - N_GROUPS, 0))
    wts_ref[...] = jnp.where(lane == 0, w1, jnp.where(lane == 1, w2, 0.0))


def _router(h, w_hi, w_lo, b_r):
    m, d = h.shape
    tm = _pick(m, (512, 256, 128))
    return pl.pallas_call(
        _router_kernel,
        name="router",
        grid=(m // tm,),
        in_specs=[
            pl.BlockSpec((tm, d), lambda i: (i, 0)),
            pl.BlockSpec((d, LANES), lambda i: (0, 0)),
            pl.BlockSpec((d, LANES), lambda i: (0, 0)),
            pl.BlockSpec((1, LANES), lambda i: (0, 0)),
        ],
        out_specs=[pl.BlockSpec((tm, LANES), lambda i: (i, 0)),
                   pl.BlockSpec((tm, LANES), lambda i: (i, 0))],
        out_shape=[jax.ShapeDtypeStruct((m, LANES), jnp.int32),
                   jax.ShapeDtypeStruct((m, LANES), F32)],
        compiler_params=_cparams(1),
    )(h, w_hi, w_lo, b_r)


def _row_copy(src_hbm, src_row, dst_vmem, dst_row, sem):
    return pltpu.make_async_copy(src_hbm.at[pl.ds(src_row, 1)], dst_vmem.at[pl.ds(dst_row, 1)], sem)


def _expert_kernel(ie_ref, row0_ref, nsub_ref, skip_ref, nrows_ref, tok_ref,
                   slab_hbm, wg_hbm, wu_hbm, wd_hbm, y_hbm,
                   slab_buf, x_ref, g_ref, u_ref, a_ref, o_buf, zbuf, wgu_buf, wd_buf, pend_ref,
                   gsem, osem, zsem, wsem, *, d, nck):
    it = pl.program_id(0)
    j = pl.program_id(1)
    n_items = pl.num_programs(0)
    nj = pl.num_programs(1)
    nsub = nsub_ref[it]
    spt = d // LANES
    ck = d // nck
    sub_rows = MOE_BLK * spt
    n_fill = (y_hbm.shape[0] - nrows_ref[0]) // MOE_BLK
    step = it * nj + j
    wslot = step % W_SLOTS

    def weight_copies(item, jj, slot, fn):
        e = ie_ref[item]

        @pl.when(jj < nck)
        def _():
            k0 = pl.multiple_of(jj * ck, ck)
            fn(pltpu.make_async_copy(wg_hbm.at[e, pl.ds(k0, ck), :], wgu_buf.at[slot, 0], wsem.at[slot]))
            fn(pltpu.make_async_copy(wu_hbm.at[e, pl.ds(k0, ck), :], wgu_buf.at[slot, 1], wsem.at[slot]))

        @pl.when(jj >= nck)
        def _():
            c0 = pl.multiple_of((jj - nck) * ck, ck)
            fn(pltpu.make_async_copy(wd_hbm.at[e, :, pl.ds(c0, ck)], wd_buf.at[slot], wsem.at[slot]))

    def fetch_weights(s):
        item = s // nj

        @pl.when(item < n_items)
        def _():
            @pl.when(skip_ref[item] == 0)
            def _():
                weight_copies(item, s % nj, s % W_SLOTS, lambda cp: cp.start())

    def fill_copy(b):
        row = pl.multiple_of(nrows_ref[0] + b * MOE_BLK, MOE_BLK)
        return pltpu.make_async_copy(zbuf, y_hbm.at[pl.ds(row, MOE_BLK)], zsem)

    def issue_gather(item):
        for sb in range(ITEM_SUB):
            @pl.when(sb < nsub_ref[item])
            def _(sb=sb):
                base = row0_ref[item] + sb * MOE_BLK

                def start(r, c):
                    src = pl.multiple_of(tok_ref[base + r] * spt, spt)
                    dst = pl.multiple_of((sb * MOE_BLK + r) * spt, spt)
                    pltpu.make_async_copy(slab_hbm.at[pl.ds(src, spt)], slab_buf.at[pl.ds(dst, spt)],
                                          gsem.at[sb]).start()
                    return c
                lax.fori_loop(0, MOE_BLK, start, 0, unroll=8)

    def unpack(item):
        for sb in range(ITEM_SUB):
            @pl.when(sb < nsub_ref[item])
            def _(sb=sb):
                pltpu.make_async_copy(slab_hbm.at[pl.ds(0, sub_rows)],
                                      slab_buf.at[pl.ds(sb * sub_rows, sub_rows)], gsem.at[sb]).wait()
                for s in range(spt):
                    col = s * LANES
                    x_ref[col // ck, sb * MOE_BLK:(sb + 1) * MOE_BLK, col % ck:col % ck + LANES] = (
                        slab_buf[pl.ds(sb * sub_rows + s, MOE_BLK, stride=spt), :].astype(BF16))

    @pl.when((it == 0) & (j == 0))
    def _():
        pend_ref[0] = 0
        pend_ref[1] = 0
        fetch_weights(0)
        fetch_weights(1)
        issue_gather(0)
        zbuf[...] = jnp.zeros(zbuf.shape, F32)
        lax.fori_loop(0, n_fill, lambda b, c: (fill_copy(b).start(), c)[1], 0)
        unpack(0)

        @pl.when(1 < pl.num_programs(0))
        def _():
            issue_gather(1)

    fetch_weights(step + 2)

    @pl.when(nsub > 0)
    def _():
        weight_copies(it, j, wslot, lambda cp: cp.wait())

    for n in range(1, ITEM_SUB + 1):
        rows = n * MOE_BLK

        @pl.when((nsub == n) & (j == 0))
        def _(rows=rows):
            xk = x_ref[0, 0:rows, :]
            g_ref[0:rows] = jnp.dot(xk, wgu_buf[wslot, 0].astype(BF16), preferred_element_type=F32)
            u_ref[0:rows] = jnp.dot(xk, wgu_buf[wslot, 1].astype(BF16), preferred_element_type=F32)

        @pl.when((nsub == n) & (j > 0) & (j < nck))
        def _(rows=rows):
            xk = x_ref[j, 0:rows, :]
            g_ref[0:rows] += jnp.dot(xk, wgu_buf[wslot, 0].astype(BF16), preferred_element_type=F32)
            u_ref[0:rows] += jnp.dot(xk, wgu_buf[wslot, 1].astype(BF16), preferred_element_type=F32)

        @pl.when((nsub == n) & (j == nck))
        def _(rows=rows):
            g = g_ref[0:rows]
            a_ref[0:rows] = (g * jax.nn.sigmoid(g) * u_ref[0:rows]).astype(BF16)

    @pl.when(j >= nck)
    def _():
        slot = j % 2
        other = 1 - slot
        col0 = pl.multiple_of((j - nck) * ck, ck)
        for n in range(1, ITEM_SUB + 1):
            rows = n * MOE_BLK

            @pl.when(nsub == n)
            def _(rows=rows):
                o_buf[slot, 0:rows] = jnp.dot(a_ref[0:rows], wd_buf[wslot].astype(BF16),
                                              preferred_element_type=F32)
                pltpu.make_async_copy(o_buf.at[slot, pl.ds(0, rows)],
                                      y_hbm.at[pl.ds(pl.multiple_of(row0_ref[it], MOE_BLK), rows), pl.ds(col0, ck)],
                                      osem.at[slot]).start()

        def drain(s):
            for n in range(1, ITEM_SUB + 1):
                rows = n * MOE_BLK

                @pl.when(pend_ref[s] == n)
                def _(rows=rows):
                    pltpu.make_async_copy(o_buf.at[s, pl.ds(0, rows)],
                                          y_hbm.at[pl.ds(0, rows), pl.ds(0, ck)], osem.at[s]).wait()
            pend_ref[s] = 0

        drain(other)
        pend_ref[slot] = nsub

        @pl.when(j == pl.num_programs(1) - 1)
        def _():
            @pl.when(it + 1 < pl.num_programs(0))
            def _():
                unpack(it + 1)

            @pl.when(it + 2 < pl.num_programs(0))
            def _():
                issue_gather(it + 2)

        @pl.when((it == pl.num_programs(0) - 1) & (j == pl.num_programs(1) - 1))
        def _():
            drain(slot)
            lax.fori_loop(0, n_fill, lambda b, c: (fill_copy(b).wait(), c)[1], 0)


def _experts(tables, buf_tok, slabs, wg, wu, wd, cap):
    item_expert, item_row0, item_nsub, item_skip, n_rows_used = tables
    n_items = item_expert.shape[0]
    n_exp, d, ff = wg.shape
    nck = MOE_CHUNKS
    ck = d // nck
    rows = ITEM_SUB * MOE_BLK
    spt = d // LANES

    return pl.pallas_call(
        functools.partial(_expert_kernel, d=d, nck=nck),
        name="moe_experts",
        grid_spec=pltpu.PrefetchScalarGridSpec(
            num_scalar_prefetch=6,
            grid=(n_items, 2 * nck),
            in_specs=[pl.BlockSpec(memory_space=pl.ANY)] * 4,
            out_specs=pl.BlockSpec(memory_space=pl.ANY),
            scratch_shapes=[
                pltpu.VMEM((rows * spt, LANES), F32),
                pltpu.VMEM((nck, rows, ck), BF16),
                pltpu.VMEM((rows, ff), F32),
                pltpu.VMEM((rows, ff), F32),
                pltpu.VMEM((rows, ff), BF16),
                pltpu.VMEM((2, rows, ck), F32),
                pltpu.VMEM((MOE_BLK, d), F32),
                pltpu.VMEM((W_SLOTS, 2, ck, ff), F32),
                pltpu.VMEM((W_SLOTS, ff, ck), F32),
                pltpu.SMEM((2,), jnp.int32),
                pltpu.SemaphoreType.DMA((ITEM_SUB,)),
                pltpu.SemaphoreType.DMA((2,)),
                pltpu.SemaphoreType.DMA(()),
                pltpu.SemaphoreType.DMA((W_SLOTS,)),
            ],
        ),
        out_shape=jax.ShapeDtypeStruct((cap, d), F32),
        compiler_params=_cparams(2, vmem_mb=EXPERT_VMEM_MB),
    )(item_expert, item_row0, item_nsub, item_skip, n_rows_used, buf_tok, slabs, wg, wu, wd)


def _combine_kernel(dest_ref, y_hbm, h_ref, wts_ref, g_ref, b_ref, o_ref, buf, sem, *, tm, alpha):
    i = pl.program_id(0)
    slot = i % 2

    def issue(tile, dst_slot):
        def start(r, c):
            a = 2 * (tile * tm + r)
            _row_copy(y_hbm, dest_ref[a], buf.at[dst_slot, 0], r, sem.at[dst_slot]).start()
            _row_copy(y_hbm, dest_ref[a + 1], buf.at[dst_slot, 1], r, sem.at[dst_slot]).start()
            return c
        lax.fori_loop(0, tm, start, 0, unroll=8)

    @pl.when(i == 0)
    def _():
        issue(0, 0)

    @pl.when(i + 1 < pl.num_programs(0))
    def _():
        issue(i + 1, 1 - slot)

    for k in range(2):
        pltpu.make_async_copy(y_hbm.at[pl.ds(0, tm)], buf.at[slot, k], sem.at[slot]).wait()
    wts = wts_ref[...]
    f = buf[slot, 0] * wts[:, 0:1] + buf[slot, 1] * wts[:, 1:2]
    o_ref[...] = _layer_norm(alpha * h_ref[...] + f, g_ref[...], b_ref[...])


def _combine_ln(dest, yb, h, wts, g, b, alpha):
    m, d = h.shape
    tm = _pick(m, (256, 128))
    return pl.pallas_call(
        functools.partial(_combine_kernel, tm=tm, alpha=alpha),
        name="moe_combine",
        grid_spec=pltpu.PrefetchScalarGridSpec(
            num_scalar_prefetch=1,
            grid=(m // tm,),
            in_specs=[
                pl.BlockSpec(memory_space=pl.ANY),
                pl.BlockSpec((tm, d), lambda i, dst: (i, 0)),
                pl.BlockSpec((tm, LANES), lambda i, dst: (i, 0)),
                pl.BlockSpec((1, d), lambda i, dst: (0, 0)),
                pl.BlockSpec((1, d), lambda i, dst: (0, 0)),
            ],
            out_specs=pl.BlockSpec((tm, d), lambda i, dst: (i, 0)),
            scratch_shapes=[pltpu.VMEM((2, 2, tm, d), F32), pltpu.SemaphoreType.DMA((2,))],
        ),
        out_shape=jax.ShapeDtypeStruct((m, d), F32),
        compiler_params=_cparams(1),
    )(dest, yb, h, wts, g, b)


def _dispatch_tables(ids, n_tok):
    flat_e = ids[:, :2].reshape(-1)
    n_assign = flat_e.shape[0]
    onehot = (flat_e[:, None] == jnp.arange(N_EXPERTS, dtype=jnp.int32)[None, :]).astype(jnp.int32)
    csum = jnp.cumsum(onehot, axis=0)
    rank = jnp.sum(onehot * csum, axis=1) - 1
    counts = csum[-1]
    padded = (counts + MOE_BLK - 1) // MOE_BLK * MOE_BLK
    pend = jnp.cumsum(padded)
    pstart = pend - padded
    dest = (pstart[flat_e] + rank).astype(jnp.int32)
    cap = -(-n_assign // MOE_BLK) * MOE_BLK + N_EXPERTS * MOE_BLK
    buf_tok = jnp.zeros((cap,), jnp.int32).at[dest].set(jnp.arange(n_assign, dtype=jnp.int32) // 2)

    item_rows = ITEM_SUB * MOE_BLK
    items_e = (padded + item_rows - 1) // item_rows
    item_end = jnp.cumsum(items_e)
    item_start = item_end - items_e
    n_used = item_end[-1]
    n_items = -(-N_EXPERTS * (ITEM_SUB - 1) // ITEM_SUB) + cap // item_rows + 1
    it = jnp.arange(n_items, dtype=jnp.int32)
    e_it = jnp.minimum(jnp.sum((item_end[None, :] <= it[:, None]).astype(jnp.int32), axis=1), N_EXPERTS - 1)
    k_it = it - item_start[e_it]
    used = it < n_used
    e_last = e_it[jnp.maximum(n_used - 1, 0)]
    item_expert = jnp.where(used, e_it, e_last).astype(jnp.int32)
    item_row0 = jnp.where(used, pstart[e_it] + k_it * item_rows, 0).astype(jnp.int32)
    item_nsub = jnp.where(used, jnp.clip((padded[e_it] - k_it * item_rows) // MOE_BLK, 0, ITEM_SUB), 0)
    item_skip = jnp.where(used, 0, 1).astype(jnp.int32)
    n_rows_used = pend[-1].astype(jnp.int32).reshape(1)
    return dest, buf_tok, (item_expert, item_row0, item_nsub.astype(jnp.int32), item_skip, n_rows_used), cap


def _rope_tables(seq):
    half = MLA_ROPE // 2
    freqs = ROPE_THETA ** (-jnp.arange(half, dtype=F32) / half)
    ang = jnp.arange(seq, dtype=F32)[:, None] * freqs
    cos = jnp.cos(ang)
    sin = jnp.sin(ang)
    return jnp.concatenate([cos, cos], -1), jnp.concatenate([sin, sin], -1)


def _swap_halves_neg(w):
    half = w.shape[-1] // 2
    return jnp.concatenate([-w[..., half:], w[..., :half]], axis=-1)


def _layer(x, w_in, w_ukv, g_ckv, w_pa, w_pb, w_o, ln1_g, ln1_b, w_rg, b_rg, w_re, b_re,
           w_gate_e, w_up_e, w_down_e, ln2_g, ln2_b, alpha):
    batch, seq, d = x.shape
    n_tok = batch * seq
    x2 = x.reshape(n_tok, d)

    splits = (DSA_Q_DIM, DSA_KV_DIM, DSA_KV_DIM, IDX_Q_DIM, IDX_DIM, IDX_HEADS, MLA_Q_DIM, KV_RANK, MLA_ROPE, 2 * d)
    offs = np.cumsum((0,) + splits)
    w_dq, w_dk, w_dv, w_iq, w_ik, w_iw, w_mq, w_ckv, w_kr, w_gates = [
        w_in[:, offs[j]:offs[j + 1]] for j in range(len(splits))]
    w_mq3 = w_mq.reshape(d, MLA_HEADS, MLA_QK_DIM)
    w_qn = w_mq3[:, :, :MLA_NOPE].reshape(d, MLA_HEADS * MLA_NOPE)
    w_qr = w_mq3[:, :, MLA_NOPE:]
    w_qs = _swap_halves_neg(w_qr)
    wr = MLA_HEADS * MLA_ROPE
    wb = jnp.concatenate([w_iq, w_dk, w_dv], axis=1).astype(BF16)
    off_k = IDX_Q_DIM
    off_v = off_k + DSA_KV_DIM
    pad_i = jnp.zeros((d, LANES - IDX_DIM - IDX_HEADS), F32)
    ws = jnp.concatenate([w_dq, w_qn, w_qr.reshape(d, wr), w_qs.reshape(d, wr), w_ckv, w_ik, w_iw, pad_i,
                          w_kr, _swap_halves_neg(w_kr)], axis=1).astype(BF16)
    off_qn = DSA_Q_DIM
    off_qr = off_qn + MLA_HEADS * MLA_NOPE
    off_qs = off_qr + wr
    off_c = off_qs + wr
    off_i = off_c + KV_RANK
    off_kr = off_i + LANES

    zb, xb16 = _matmul_cast(x2, wb, BF16)
    zs = _matmul(xb16, ws, F32, tn=ws.shape[1] // 2)
    gates = _matmul(xb16, w_gates.astype(BF16), F32, tn=_pick(2 * d, (2048, 1024, 512, 256)))

    kidx = zs[:, off_i:off_i + IDX_DIM].astype(BF16).reshape(batch, seq, IDX_DIM)
    w_idx = zs[:, off_i + IDX_DIM:off_i + IDX_DIM + IDX_HEADS].reshape(batch, seq, IDX_HEADS)
    o_a = _dsa_attention(zs, zb, w_idx, kidx, batch, seq, off_k, off_v)

    cos2, sin2 = _rope_tables(seq)
    w_ukv3 = w_ukv.reshape(KV_RANK, MLA_HEADS, MLA_NOPE + MLA_V)
    w_kv = jnp.concatenate([w_ukv3[:, :, :MLA_NOPE].reshape(KV_RANK, MLA_HEADS * MLA_NOPE),
                            w_ukv3[:, :, MLA_NOPE:].reshape(KV_RANK, MLA_HEADS * MLA_V)], axis=1).astype(BF16)
    q_cat = _mla_q_prep(zs, jnp.tile(cos2, (1, MLA_HEADS)), jnp.tile(sin2, (1, MLA_HEADS)),
                        batch, seq, off_qn, off_qr, off_qs)
    k_cat, v_mla = _mla_kv_prep(zs, g_ckv.reshape(1, KV_RANK), w_kv, cos2, sin2, batch, seq, off_c, off_kr)
    o_b = _mla_flash(q_cat, k_cat, v_mla, batch, seq).reshape(n_tok, MLA_OUT_DIM)

    merged = _gated_merge(o_a, o_b, w_pa.astype(BF16), w_pb.astype(BF16), gates)
    h, slabs = _proj_ln(merged, w_o.astype(BF16), x2, ln1_g.reshape(1, d), ln1_b.reshape(1, d), alpha)

    pad_r = jnp.zeros((d, LANES - N_GROUPS - N_EXPERTS), F32)
    w_r = jnp.concatenate([w_rg, w_re, pad_r], axis=1)
    w_r_hi = w_r.astype(BF16)
    w_r_lo = (w_r - w_r_hi.astype(F32)).astype(BF16)
    b_r = jnp.concatenate([b_rg, b_re, jnp.zeros((LANES - N_GROUPS - N_EXPERTS,), F32)]).reshape(1, LANES)
    ids, wts = _router(h, w_r_hi, w_r_lo, b_r)
    dest, buf_tok, item_tables, cap = _dispatch_tables(ids, n_tok)
    yb = _experts(item_tables, buf_tok, slabs, w_gate_e, w_up_e, w_down_e, cap)
    out = _combine_ln(dest, yb, h, wts, ln2_g.reshape(1, d), ln2_b.reshape(1, d), alpha)
    return out.reshape(batch, seq, d)


def kernel(x, w_in, w_ukv, g_ckv, w_pa, w_pb, w_o, ln1_g, ln1_b, w_rg, b_rg, w_re, b_re,
           w_gate_e, w_up_e, w_down_e, ln2_g, ln2_b):
    depth = w_in.shape[0]
    alpha = (2.0 * depth) ** 0.25
    for l in range(depth):
        x = _layer(x, w_in[l], w_ukv[l], g_ckv[l], w_pa[l], w_pb[l], w_o[l], ln1_g[l], ln1_b[l],
                   w_rg[l], b_rg[l], w_re[l], b_re[l], w_gate_e[l], w_up_e[l], w_down_e[l],
                   ln2_g[l], ln2_b[l], alpha)
    return x
```

```python
import functools

import numpy as np
import jax
import jax.numpy as jnp
from jax import lax
from jax.experimental import pallas as pl
from jax.experimental.pallas import tpu as pltpu

N_DSA_HEADS = 8
DSA_HEAD_DIM = 128
DSA_KV_HEADS = 2
DSA_REP = N_DSA_HEADS // DSA_KV_HEADS
IDX_HEADS = 16
IDX_DIM = 64
TOPK_MAX = 256
MLA_HEADS = 8
MLA_NOPE = 128
MLA_ROPE = 64
MLA_V = 128
KV_RANK = 512
ROPE_THETA = 10000.0
N_GROUPS = 8
EXPERTS_PER_GROUP = 8
N_EXPERTS = N_GROUPS * EXPERTS_PER_GROUP
LN_EPS = 1e-5
RMS_EPS = 1e-6

DSA_Q_DIM = N_DSA_HEADS * DSA_HEAD_DIM
DSA_KV_DIM = DSA_KV_HEADS * DSA_HEAD_DIM
IDX_Q_DIM = IDX_HEADS * IDX_DIM
MLA_QK_DIM = MLA_NOPE + MLA_ROPE
MLA_Q_DIM = MLA_HEADS * MLA_QK_DIM
MLA_OUT_DIM = MLA_HEADS * MLA_V

LANES = 128
VMEM_LIMIT_MB = 56

DSA_TQ = 128
DSA_KC = 512
MLA_T = 512
MOE_BLK = 256
ITEM_SUB = 3
MOE_CHUNKS = 4
W_SLOTS = 3
EXPERT_VMEM_MB = 58
NEG = -1e30
LOG2E = 1.4426950408889634
INT_MIN = -2 ** 31
FLT_MAX = 3.4028234663852886e38

F32 = jnp.float32
BF16 = jnp.bfloat16


def _cparams(n_axes, vmem_mb=VMEM_LIMIT_MB):
    return pltpu.CompilerParams(dimension_semantics=("arbitrary",) * n_axes,
                                vmem_limit_bytes=vmem_mb * 1024 * 1024)


def _pick(n, prefs):
    for p in prefs:
        if n % p == 0:
            return p
    return n


def _lane_tiles(x):
    return [x[:, j * LANES:(j + 1) * LANES] for j in range(x.shape[1] // LANES)]


def _row_max(x):
    return jnp.max(functools.reduce(jnp.maximum, _lane_tiles(x)), axis=1, keepdims=True)


def _row_sum(x):
    return jnp.sum(functools.reduce(jnp.add, _lane_tiles(x)), axis=1, keepdims=True)


def _mm_kernel(a_ref, w_ref, o_ref):
    o_ref[...] = jnp.dot(a_ref[...], w_ref[...], preferred_element_type=F32).astype(o_ref.dtype)


def _matmul(a, w, out_dtype, tn):
    m, k = a.shape
    n = w.shape[1]
    tm = _pick(m, (512, 256, 128))
    return pl.pallas_call(
        _mm_kernel,
        grid=(n // tn, m // tm),
        in_specs=[pl.BlockSpec((tm, k), lambda j, i: (i, 0)),
                  pl.BlockSpec((k, tn), lambda j, i: (0, j))],
        out_specs=pl.BlockSpec((tm, tn), lambda j, i: (i, j)),
        out_shape=jax.ShapeDtypeStruct((m, n), out_dtype),
        compiler_params=_cparams(2),
    )(a, w)


def _mm_cast_kernel(a_ref, w_ref, o_ref, a16_ref):
    a16 = a_ref[...].astype(BF16)
    a16_ref[...] = a16
    o_ref[...] = jnp.dot(a16, w_ref[...], preferred_element_type=F32).astype(o_ref.dtype)


def _matmul_cast(a, w, out_dtype):
    m, k = a.shape
    n = w.shape[1]
    tm = _pick(m, (512, 256, 128))
    return pl.pallas_call(
        _mm_cast_kernel,
        grid=(m // tm,),
        in_specs=[pl.BlockSpec((tm, k), lambda i: (i, 0)),
                  pl.BlockSpec((k, n), lambda i: (0, 0))],
        out_specs=[pl.BlockSpec((tm, n), lambda i: (i, 0)),
                   pl.BlockSpec((tm, k), lambda i: (i, 0))],
        out_shape=[jax.ShapeDtypeStruct((m, n), out_dtype), jax.ShapeDtypeStruct((m, k), BF16)],
        compiler_params=_cparams(1),
    )(a, w)


def _dsa_kernel(q_ref, iq_ref, w_ref, kidx_ref, k_ref, v_ref, o_ref,
                sc_ref, thr_ref, jcut_ref, s_ref, m_ref, l_ref, acc_ref, *, tq, kc, n_sel, seq):
    i = pl.program_id(1)
    nch = (i * tq + tq + kc - 1) // kc
    row = lax.broadcasted_iota(jnp.int32, (tq, kc), 0)
    lane = lax.broadcasted_iota(jnp.int32, (tq, kc), 1)
    qpos = i * tq + row

    iq_all = iq_ref[...]
    iq = jnp.concatenate([iq_all[:, h * IDX_DIM:(h + 1) * IDX_DIM] for h in range(IDX_HEADS)],
                         axis=0)
    w = w_ref[:, IDX_DIM:IDX_DIM + IDX_HEADS] * (IDX_DIM ** -0.5 * IDX_HEADS ** -0.5)

    def score_body(c, carry):
        off = pl.multiple_of(c * kc, kc)
        kx = kidx_ref[pl.ds(off, kc), 0:IDX_DIM].astype(BF16)
        logits = lax.dot_general(iq, kx, (((1,), (1,)), ((), ())), preferred_element_type=F32)
        sc = jnp.zeros((tq, kc), F32)
        for h in range(IDX_HEADS):
            sc = sc + jnp.maximum(logits[h * tq:(h + 1) * tq], 0.0) * w[:, h:h + 1]
        sc_ref[c] = jnp.where(off + lane <= qpos, sc, -jnp.inf)
        return carry

    lax.fori_loop(0, nch, score_body, 0)

    def key_to_f32(key):
        return pltpu.bitcast(jnp.where(key < 0, key ^ jnp.int32(0x7FFFFFFF), key), F32)

    key_lowest = INT_MIN + 0x00800000
    thr_ref[...] = jnp.full((tq, 1), -FLT_MAX, F32)
    jcut_ref[...] = jnp.full((tq, 1), seq, jnp.int32)

    def count(pred_fn):
        def body(c, acc):
            sch = sc_ref[c]
            for j in range(kc // LANES):
                acc = acc + pred_fn(sch[:, j * LANES:(j + 1) * LANES], c * kc + j * LANES)
            return acc
        acc = lax.fori_loop(0, nch, body, jnp.zeros((tq, LANES), jnp.int32))
        return jnp.sum(acc.astype(F32), axis=1, keepdims=True).astype(jnp.int32)

    lane1 = lax.broadcasted_iota(jnp.int32, (tq, LANES), 1)

    @pl.when(i * tq + tq > n_sel)
    def _():
        def bit_body(b, t):
            cand = t + lax.shift_left(jnp.int32(1), 31 - b)
            cand_b = jnp.broadcast_to(key_to_f32(cand), (tq, LANES))
            cnt = count(lambda sch, _: jnp.where(sch >= cand_b, 1, 0))
            return jnp.where(cnt >= n_sel, cand, t)

        t = lax.fori_loop(0, 32, bit_body, jnp.full((tq, 1), INT_MIN, jnp.int32))
        t_f = key_to_f32(jnp.maximum(t, key_lowest))
        thr_ref[...] = t_f
        t_b = jnp.broadcast_to(t_f, (tq, LANES))
        c_ge = count(lambda sch, _: jnp.where(sch >= t_b, 1, 0))
        c_gt = count(lambda sch, _: jnp.where(sch > t_b, 1, 0))
        excess = c_ge > n_sel
        need = n_sel - c_gt

        @pl.when(jnp.max(jnp.where(excess, 1.0, 0.0)) > 0.0)
        def _():
            nbits = max(1, int(np.ceil(np.log2(seq))))

            def jbit(b, jc):
                cand = jc + lax.shift_left(jnp.int32(1), nbits - 1 - b)
                cand_b = jnp.broadcast_to(cand, (tq, LANES))
                cnt = count(lambda sch, base: jnp.where(
                    sch == t_b, jnp.where(base + lane1 < cand_b, 1, 0), 0))
                return jnp.where(cnt < need, cand, jc)

            jc = lax.fori_loop(0, nbits, jbit, jnp.zeros((tq, 1), jnp.int32))
            jcut_ref[...] = jnp.where(excess, jc, seq)

    qs = (q_ref[...] * (DSA_HEAD_DIM ** -0.5 * LOG2E)).astype(BF16)
    qg = [jnp.concatenate([qs[:, (g * DSA_REP + r) * DSA_HEAD_DIM:(g * DSA_REP + r + 1) * DSA_HEAD_DIM]
                           for r in range(DSA_REP)], axis=0) for g in range(DSA_KV_HEADS)]
    m_ref[...] = jnp.full(m_ref.shape, NEG, F32)
    l_ref[...] = jnp.zeros(l_ref.shape, F32)
    acc_ref[...] = jnp.zeros(acc_ref.shape, F32)
    t = thr_ref[...]
    jc = jcut_ref[...]

    def scores(c, slot):
        off = pl.multiple_of(c * kc, kc)
        for g in range(DSA_KV_HEADS):
            s_ref[slot, g] = lax.dot_general(
                qg[g], k_ref[pl.ds(off, kc), g * DSA_HEAD_DIM:(g + 1) * DSA_HEAD_DIM],
                (((1,), (1,)), ((), ())), preferred_element_type=F32)

    def update(c, slot):
        off = pl.multiple_of(c * kc, kc)
        sc = sc_ref[c]
        kpos = off + lane
        sel = jnp.where(kpos <= jc, jnp.where(sc >= t, 1, 0), jnp.where(sc > t, 1, 0)) > 0
        md = jnp.where(sel, (qpos - kpos).astype(F32), -NEG)
        for g in range(DSA_KV_HEADS):
            s = s_ref[slot, g]
            vg = v_ref[pl.ds(off, kc), g * DSA_HEAD_DIM:(g + 1) * DSA_HEAD_DIM]
            e = jnp.concatenate(
                [s[r * tq:(r + 1) * tq] - md * (LOG2E * 2.0 ** (-8.0 * (g * DSA_REP + r + 1) / N_DSA_HEADS))
                 for r in range(DSA_REP)], axis=0)
            m_old = m_ref[g]
            m_new = jnp.maximum(m_old, _row_max(e))
            p = jnp.exp2(e - m_new)
            alpha = jnp.exp2(m_old - m_new)
            l_ref[g] = alpha * l_ref[g] + _row_sum(p)
            acc_ref[g] = alpha * acc_ref[g] + jnp.dot(p.astype(BF16), vg, preferred_element_type=F32)
            m_ref[g] = m_new

    scores(0, 0)
    n_pairs = (nch - 1) // 2

    def att_body(p, carry):
        c = 2 * p
        scores(c + 1, 1)
        update(c, 0)
        scores(c + 2, 0)
        update(c + 1, 1)
        return carry

    lax.fori_loop(0, n_pairs, att_body, 0)
    c_tail = 2 * n_pairs

    @pl.when(nch - c_tail == 2)
    def _():
        scores(c_tail + 1, 1)
        update(c_tail, 0)
        update(c_tail + 1, 1)

    @pl.when(nch - c_tail == 1)
    def _():
        update(c_tail, 0)

    for g in range(DSA_KV_HEADS):
        o = acc_ref[g] / l_ref[g]
        for r in range(DSA_REP):
            hh = g * DSA_REP + r
            o_ref[:, hh * DSA_HEAD_DIM:(hh + 1) * DSA_HEAD_DIM] = o[r * tq:(r + 1) * tq].astype(o_ref.dtype)


def _dsa_attention(zs, zb, batch, seq, off_i, off_k, off_v):
    tq = min(DSA_TQ, seq)
    kc = min(DSA_KC, seq)
    nq = seq // tq
    n_sel = min(TOPK_MAX, seq // 4)
    kern = functools.partial(_dsa_kernel, tq=tq, kc=kc, n_sel=n_sel, seq=seq)
    return pl.pallas_call(
        kern,
        name="dsa",
        grid=(batch, nq),
        in_specs=[
            pl.BlockSpec((tq, DSA_Q_DIM), lambda b, i: (b * nq + i, 0)),
            pl.BlockSpec((tq, IDX_Q_DIM), lambda b, i: (b * nq + i, 0)),
            pl.BlockSpec((tq, LANES), lambda b, i: (b * nq + i, off_i // LANES)),
            pl.BlockSpec((seq, LANES), lambda b, i: (b, off_i // LANES)),
            pl.BlockSpec((seq, DSA_KV_DIM), lambda b, i: (b, off_k // DSA_KV_DIM)),
            pl.BlockSpec((seq, DSA_KV_DIM), lambda b, i: (b, off_v // DSA_KV_DIM)),
        ],
        out_specs=pl.BlockSpec((tq, DSA_Q_DIM), lambda b, i: (b * nq + i, 0)),
        out_shape=jax.ShapeDtypeStruct((batch * seq, DSA_Q_DIM), BF16),
        scratch_shapes=[
            pltpu.VMEM((seq // kc, tq, kc), F32),
            pltpu.VMEM((tq, 1), F32),
            pltpu.VMEM((tq, 1), jnp.int32),
            pltpu.VMEM((2, DSA_KV_HEADS, DSA_REP * tq, kc), F32),
            pltpu.VMEM((DSA_KV_HEADS, DSA_REP * tq, 1), F32),
            pltpu.VMEM((DSA_KV_HEADS, DSA_REP * tq, 1), F32),
            pltpu.VMEM((DSA_KV_HEADS, DSA_REP * tq, DSA_HEAD_DIM), F32),
        ],
        compiler_params=_cparams(2),
    )(zs, zb, zs, zs, zb, zb)


def _mla_q_kernel(qn_ref, qr_ref, qs_ref, cos_ref, sin_ref, o_ref):
    c = MLA_QK_DIM ** -0.5 * LOG2E
    pe = ((qr_ref[...] * cos_ref[...] + qs_ref[...] * sin_ref[...]) * c).astype(o_ref.dtype)
    qn = (qn_ref[...] * c).astype(o_ref.dtype)
    for h in range(MLA_HEADS):
        o_ref[0, h, :, 0:MLA_NOPE] = qn[:, h * MLA_NOPE:(h + 1) * MLA_NOPE]
        o_ref[0, h, :, MLA_NOPE:MLA_QK_DIM] = pe[:, h * MLA_ROPE:(h + 1) * MLA_ROPE]


def _mla_q_prep(zs, cos_q, sin_q, batch, seq, off_qn, off_qr, off_qs):
    tm = _pick(seq, (512, 256, 128))
    ns = seq // tm
    wr = MLA_HEADS * MLA_ROPE
    return pl.pallas_call(
        _mla_q_kernel,
        name="mla_q",
        grid=(batch * ns,),
        in_specs=[
            pl.BlockSpec((tm, MLA_HEADS * MLA_NOPE), lambda i: (i, off_qn // (MLA_HEADS * MLA_NOPE))),
            pl.BlockSpec((tm, wr), lambda i: (i, off_qr // wr)),
            pl.BlockSpec((tm, wr), lambda i: (i, off_qs // wr)),
            pl.BlockSpec((tm, wr), lambda i: (i % ns, 0)),
            pl.BlockSpec((tm, wr), lambda i: (i % ns, 0)),
        ],
        out_specs=pl.BlockSpec((1, MLA_HEADS, tm, MLA_QK_DIM), lambda i: (i // ns, 0, i % ns, 0)),
        out_shape=jax.ShapeDtypeStruct((batch, MLA_HEADS, seq, MLA_QK_DIM), BF16),
        compiler_params=_cparams(1),
    )(zs, zs, zs, cos_q, sin_q)


def _mla_kv_kernel(c_ref, kr_ref, g_ref, w_ref, cos_ref, sin_ref, k_ref, v_ref):
    c = c_ref[...]
    cn = c * lax.rsqrt(jnp.mean(c * c, axis=-1, keepdims=True) + RMS_EPS) * g_ref[...]
    kv = jnp.dot(cn.astype(BF16), w_ref[...], preferred_element_type=F32)
    kr = kr_ref[...]
    pe = (kr[:, 0:MLA_ROPE] * cos_ref[...] + kr[:, MLA_ROPE:2 * MLA_ROPE] * sin_ref[...]).astype(k_ref.dtype)
    for h in range(MLA_HEADS):
        k_ref[0, h, :, 0:MLA_NOPE] = kv[:, h * MLA_NOPE:(h + 1) * MLA_NOPE].astype(k_ref.dtype)
        k_ref[0, h, :, MLA_NOPE:MLA_QK_DIM] = pe
        v0 = MLA_HEADS * MLA_NOPE + h * MLA_V
        v_ref[0, h] = kv[:, v0:v0 + MLA_V].astype(v_ref.dtype)


def _mla_kv_prep(zs, g_ckv, w_kv, cos_k, sin_k, batch, seq, off_c, off_kr):
    tm = _pick(seq, (512, 256, 128))
    ns = seq // tm
    nkv = MLA_HEADS * (MLA_NOPE + MLA_V)
    return pl.pallas_call(
        _mla_kv_kernel,
        name="mla_kv",
        grid=(batch * ns,),
        in_specs=[
            pl.BlockSpec((tm, KV_RANK), lambda i: (i, off_c // KV_RANK)),
            pl.BlockSpec((tm, 2 * MLA_ROPE), lambda i: (i, off_kr // (2 * MLA_ROPE))),
            pl.BlockSpec((1, KV_RANK), lambda i: (0, 0)),
            pl.BlockSpec((KV_RANK, nkv), lambda i: (0, 0)),
            pl.BlockSpec((tm, MLA_ROPE), lambda i: (i % ns, 0)),
            pl.BlockSpec((tm, MLA_ROPE), lambda i: (i % ns, 0)),
        ],
        out_specs=[
            pl.BlockSpec((1, MLA_HEADS, tm, MLA_QK_DIM), lambda i: (i // ns, 0, i % ns, 0)),
            pl.BlockSpec((1, MLA_HEADS, tm, MLA_V), lambda i: (i // ns, 0, i % ns, 0)),
        ],
        out_shape=[jax.ShapeDtypeStruct((batch, MLA_HEADS, seq, MLA_QK_DIM), BF16),
                   jax.ShapeDtypeStruct((batch, MLA_HEADS, seq, MLA_V), BF16)],
        compiler_params=_cparams(1),
    )(zs, zs, g_ckv, w_kv, cos_k, sin_k)


def _mla_flash_kernel(q_ref, k_ref, v_ref, o_ref, s_ref, m_ref, l_ref, acc_ref, *, t):
    qi = pl.program_id(2)
    q = q_ref[0, 0]
    m_ref[...] = jnp.full(m_ref.shape, NEG, F32)
    l_ref[...] = jnp.zeros(l_ref.shape, F32)
    acc_ref[...] = jnp.zeros(acc_ref.shape, F32)

    def scores(j, slot):
        off = pl.multiple_of(j * t, t)
        s_ref[slot] = lax.dot_general(q, k_ref[0, 0, pl.ds(off, t), :], (((1,), (1,)), ((), ())),
                                      preferred_element_type=F32)

    def update(j, slot, diagonal):
        off = pl.multiple_of(j * t, t)
        s = s_ref[slot]
        if diagonal:
            row = lax.broadcasted_iota(jnp.int32, (t, t), 0)
            col = lax.broadcasted_iota(jnp.int32, (t, t), 1)
            s = jnp.where(col <= row, s, NEG)
        m_old = m_ref[...]
        m_new = jnp.maximum(m_old, _row_max(s))
        p = jnp.exp2(s - m_new)
        alpha = jnp.exp2(m_old - m_new)
        l_ref[...] = alpha * l_ref[...] + _row_sum(p)
        acc_ref[...] = alpha * acc_ref[...] + jnp.dot(p.astype(BF16), v_ref[0, 0, pl.ds(off, t), :],
                                                      preferred_element_type=F32)
        m_ref[...] = m_new

    scores(0, 0)
    n_pairs = qi // 2

    def body(p, carry):
        j = 2 * p
        scores(j + 1, 1)
        update(j, 0, False)
        scores(j + 2, 0)
        update(j + 1, 1, False)
        return carry

    lax.fori_loop(0, n_pairs, body, 0)
    j_tail = 2 * n_pairs

    @pl.when(qi - j_tail == 1)
    def _():
        scores(j_tail + 1, 1)
        update(j_tail, 0, False)
        update(j_tail + 1, 1, True)

    @pl.when(qi == j_tail)
    def _():
        update(j_tail, 0, True)

    o_ref[0] = (acc_ref[...] / l_ref[...]).astype(o_ref.dtype)


def _mla_flash(q_cat, k_cat, v, batch, seq):
    t = min(MLA_T, seq)
    n = seq // t
    kern = functools.partial(_mla_flash_kernel, t=t)
    return pl.pallas_call(
        kern,
        name="mla_flash",
        grid=(batch, MLA_HEADS, n),
        in_specs=[
            pl.BlockSpec((1, 1, t, MLA_QK_DIM), lambda b, h, qi: (b, h, qi, 0)),
            pl.BlockSpec((1, 1, seq, MLA_QK_DIM), lambda b, h, qi: (b, h, 0, 0)),
            pl.BlockSpec((1, 1, seq, MLA_V), lambda b, h, qi: (b, h, 0, 0)),
        ],
        out_specs=pl.BlockSpec((1, t, MLA_V), lambda b, h, qi: (b, qi, h)),
        out_shape=jax.ShapeDtypeStruct((batch, seq, MLA_OUT_DIM), BF16),
        scratch_shapes=[pltpu.VMEM((2, t, t), F32), pltpu.VMEM((t, 1), F32), pltpu.VMEM((t, 1), F32),
                        pltpu.VMEM((t, MLA_V), F32)],
        compiler_params=_cparams(3),
    )(q_cat, k_cat, v)


def _merge_kernel(oa_ref, ob_ref, wa_ref, wb_ref, ga_ref, gb_ref, o_ref):
    pa = jnp.dot(oa_ref[...], wa_ref[...], preferred_element_type=F32)
    pb = jnp.dot(ob_ref[...], wb_ref[...], preferred_element_type=F32)
    o_ref[...] = (jax.nn.sigmoid(ga_ref[...]) * pa + jax.nn.sigmoid(gb_ref[...]) * pb).astype(o_ref.dtype)


def _gated_merge(o_a, o_b, w_pa, w_pb, gates):
    m = o_a.shape[0]
    d = w_pa.shape[1]
    tm = _pick(m, (512, 256, 128))
    tn = _pick(d, (1024, 512, 256, 128))
    nn = d // tn
    return pl.pallas_call(
        _merge_kernel,
        name="gated_merge",
        grid=(nn, m // tm),
        in_specs=[
            pl.BlockSpec((tm, o_a.shape[1]), lambda j, i: (i, 0)),
            pl.BlockSpec((tm, o_b.shape[1]), lambda j, i: (i, 0)),
            pl.BlockSpec((w_pa.shape[0], tn), lambda j, i: (0, j)),
            pl.BlockSpec((w_pb.shape[0], tn), lambda j, i: (0, j)),
            pl.BlockSpec((tm, tn), lambda j, i: (i, j)),
            pl.BlockSpec((tm, tn), lambda j, i: (i, j + nn)),
        ],
        out_specs=pl.BlockSpec((tm, tn), lambda j, i: (i, j)),
        out_shape=jax.ShapeDtypeStruct((m, d), BF16),
        compiler_params=_cparams(2),
    )(o_a, o_b, w_pa, w_pb, gates, gates)


def _layer_norm(y, g, b):
    mu = jnp.mean(y, axis=-1, keepdims=True)
    dlt = y - mu
    var = jnp.mean(dlt * dlt, axis=-1, keepdims=True)
    return dlt * lax.rsqrt(var + LN_EPS) * g + b


def _proj_ln_kernel(a_ref, w_ref, x_ref, g_ref, b_ref, o_ref, slab_ref, *, alpha):
    y = alpha * x_ref[...] + jnp.dot(a_ref[...], w_ref[...], preferred_element_type=F32)
    h = _layer_norm(y, g_ref[...], b_ref[...])
    o_ref[...] = h
    tm, d = h.shape
    spt = d // LANES
    for s in range(spt):
        slab_ref[pl.ds(s, tm, stride=spt), :] = h[:, s * LANES:(s + 1) * LANES]


def _proj_ln(a, w, x, g, b, alpha):
    m, d = x.shape
    tm = _pick(m, (512, 256, 128))
    spt = d // LANES
    return pl.pallas_call(
        functools.partial(_proj_ln_kernel, alpha=alpha),
        name="proj_ln",
        grid=(m // tm,),
        in_specs=[
            pl.BlockSpec((tm, a.shape[1]), lambda i: (i, 0)),
            pl.BlockSpec(w.shape, lambda i: (0, 0)),
            pl.BlockSpec((tm, d), lambda i: (i, 0)),
            pl.BlockSpec((1, d), lambda i: (0, 0)),
            pl.BlockSpec((1, d), lambda i: (0, 0)),
        ],
        out_specs=[pl.BlockSpec((tm, d), lambda i: (i, 0)),
                   pl.BlockSpec((tm * spt, LANES), lambda i: (i, 0))],
        out_shape=[jax.ShapeDtypeStruct((m, d), F32),
                   jax.ShapeDtypeStruct((m * spt, LANES), F32)],
        compiler_params=_cparams(1),
    )(a, w, x, g, b)


def _router_kernel(h_ref, whi_ref, wlo_ref, b_ref, ids_ref, wts_ref):
    h = h_ref[...]
    h_hi = h.astype(BF16)
    h_lo = (h - h_hi.astype(F32)).astype(BF16)
    logits = (jnp.dot(h_hi, whi_ref[...], preferred_element_type=F32)
              + jnp.dot(h_hi, wlo_ref[...], preferred_element_type=F32)
              + jnp.dot(h_lo, whi_ref[...], preferred_element_type=F32)) + b_ref[...]
    tm = h.shape[0]
    lane = lax.broadcasted_iota(jnp.int32, (tm, LANES), 1)
    lane_f = lane.astype(F32)
    big = float(4 * LANES)
    ninf = -jnp.inf
    gl = jnp.where(lane < N_GROUPS, logits, ninf)
    ge = jnp.exp(gl - jnp.max(gl, axis=1, keepdims=True))
    gprob = ge / jnp.sum(ge, axis=1, keepdims=True)
    gw = jnp.max(gprob, axis=1, keepdims=True)
    gidx = jnp.min(jnp.where(gprob == gw, lane_f, big), axis=1, keepdims=True).astype(jnp.int32)
    in_group = lax.shift_right_arithmetic(lane - N_GROUPS, 3) == gidx
    el = jnp.where(in_group, logits, ninf)
    ee = jnp.exp(el - jnp.max(el, axis=1, keepdims=True))
    ep = jnp.where(in_group, ee / jnp.sum(ee, axis=1, keepdims=True), -1.0)
    p1 = jnp.max(ep, axis=1, keepdims=True)
    i1 = jnp.min(jnp.where(ep == p1, lane_f, big), axis=1, keepdims=True).astype(jnp.int32)
    ep2 = jnp.where(lane == i1, -1.0, ep)
    p2 = jnp.max(ep2, axis=1, keepdims=True)
    i2 = jnp.min(jnp.where(ep2 == p2, lane_f, big), axis=1, keepdims=True).astype(jnp.int32)
    den = p1 + p2
    w1 = gw * (p1 / den)
    w2 = gw * (p2 / den)
    ids_ref[...] = jnp.where(lane == 0, i1 - N_GROUPS, jnp.where(lane == 1, i2 - N_GROUPS, 0))
    wts_ref[...] = jnp.where(lane == 0, w1, jnp.where(lane == 1, w2, 0.0))


def _router(h, w_hi, w_lo, b_r):
    m, d = h.shape
    tm = _pick(m, (512, 256, 128))
    return pl.pallas_call(
        _router_kernel,
        name="router",
        grid=(m // tm,),
        in_specs=[
            pl.BlockSpec((tm, d), lambda i: (i, 0)),
            pl.BlockSpec((d, LANES), lambda i: (0, 0)),
            pl.BlockSpec((d, LANES), lambda i: (0, 0)),
            pl.BlockSpec((1, LANES), lambda i: (0, 0)),
        ],
        out_specs=[pl.BlockSpec((tm, LANES), lambda i: (i, 0)),
                   pl.BlockSpec((tm, LANES), lambda i: (i, 0))],
        out_shape=[jax.ShapeDtypeStruct((m, LANES), jnp.int32),
                   jax.ShapeDtypeStruct((m, LANES), F32)],
        compiler_params=_cparams(1),
    )(h, w_hi, w_lo, b_r)


def _row_copy(src_hbm, src_row, dst_vmem, dst_row, sem):
    return pltpu.make_async_copy(src_hbm.at[pl.ds(src_row, 1)], dst_vmem.at[pl.ds(dst_row, 1)], sem)


def _expert_kernel(ie_ref, row0_ref, nsub_ref, skip_ref, nrows_ref, tok_ref,
                   slab_hbm, wg_hbm, wu_hbm, wd_hbm, y_hbm,
                   slab_buf, x_ref, g_ref, u_ref, a_ref, o_buf, zbuf, wgu_buf, wd_buf, pend_ref,
                   gsem, osem, zsem, wsem, *, d, nck):
    it = pl.program_id(0)
    j = pl.program_id(1)
    n_items = pl.num_programs(0)
    nj = pl.num_programs(1)
    nsub = nsub_ref[it]
    spt = d // LANES
    ck = d // nck
    sub_rows = MOE_BLK * spt
    n_fill = (y_hbm.shape[0] - nrows_ref[0]) // MOE_BLK
    step = it * nj + j
    wslot = step % W_SLOTS

    def weight_copies(item, jj, slot, fn):
        e = ie_ref[item]

        @pl.when(jj < nck)
        def _():
            k0 = pl.multiple_of(jj * ck, ck)
            fn(pltpu.make_async_copy(wg_hbm.at[e, pl.ds(k0, ck), :], wgu_buf.at[slot, 0], wsem.at[slot]))
            fn(pltpu.make_async_copy(wu_hbm.at[e, pl.ds(k0, ck), :], wgu_buf.at[slot, 1], wsem.at[slot]))

        @pl.when(jj >= nck)
        def _():
            c0 = pl.multiple_of((jj - nck) * ck, ck)
            fn(pltpu.make_async_copy(wd_hbm.at[e, :, pl.ds(c0, ck)], wd_buf.at[slot], wsem.at[slot]))

    def fetch_weights(s):
        item = s // nj

        @pl.when(item < n_items)
        def _():
            @pl.when(skip_ref[item] == 0)
            def _():
                weight_copies(item, s % nj, s % W_SLOTS, lambda cp: cp.start())

    def fill_copy(b):
        row = pl.multiple_of(nrows_ref[0] + b * MOE_BLK, MOE_BLK)
        return pltpu.make_async_copy(zbuf, y_hbm.at[pl.ds(row, MOE_BLK)], zsem)

    def issue_gather(item):
        for sb in range(ITEM_SUB):
            @pl.when(sb < nsub_ref[item])
            def _(sb=sb):
                base = row0_ref[item] + sb * MOE_BLK

                def start(r, c):
                    src = pl.multiple_of(tok_ref[base + r] * spt, spt)
                    dst = pl.multiple_of((sb * MOE_BLK + r) * spt, spt)
                    pltpu.make_async_copy(slab_hbm.at[pl.ds(src, spt)], slab_buf.at[pl.ds(dst, spt)],
                                          gsem.at[sb]).start()
                    return c
                lax.fori_loop(0, MOE_BLK, start, 0, unroll=8)

    def unpack(item):
        for sb in range(ITEM_SUB):
            @pl.when(sb < nsub_ref[item])
            def _(sb=sb):
                pltpu.make_async_copy(slab_hbm.at[pl.ds(0, sub_rows)],
                                      slab_buf.at[pl.ds(sb * sub_rows, sub_rows)], gsem.at[sb]).wait()
                for s in range(spt):
                    col = s * LANES
                    x_ref[col // ck, sb * MOE_BLK:(sb + 1) * MOE_BLK, col % ck:col % ck + LANES] = (
                        slab_buf[pl.ds(sb * sub_rows + s, MOE_BLK, stride=spt), :].astype(BF16))

    @pl.when((it == 0) & (j == 0))
    def _():
        pend_ref[0] = 0
        pend_ref[1] = 0
        fetch_weights(0)
        fetch_weights(1)
        issue_gather(0)
        zbuf[...] = jnp.zeros(zbuf.shape, F32)
        lax.fori_loop(0, n_fill, lambda b, c: (fill_copy(b).start(), c)[1], 0)
        unpack(0)

        @pl.when(1 < pl.num_programs(0))
        def _():
            issue_gather(1)

    fetch_weights(step + 2)

    @pl.when(nsub > 0)
    def _():
        weight_copies(it, j, wslot, lambda cp: cp.wait())

    for n in range(1, ITEM_SUB + 1):
        rows = n * MOE_BLK

        @pl.when((nsub == n) & (j == 0))
        def _(rows=rows):
            xk = x_ref[0, 0:rows, :]
            g_ref[0:rows] = jnp.dot(xk, wgu_buf[wslot, 0].astype(BF16), preferred_element_type=F32)
            u_ref[0:rows] = jnp.dot(xk, wgu_buf[wslot, 1].astype(BF16), preferred_element_type=F32)

        @pl.when((nsub == n) & (j > 0) & (j < nck))
        def _(rows=rows):
            xk = x_ref[j, 0:rows, :]
            g_ref[0:rows] += jnp.dot(xk, wgu_buf[wslot, 0].astype(BF16), preferred_element_type=F32)
            u_ref[0:rows] += jnp.dot(xk, wgu_buf[wslot, 1].astype(BF16), preferred_element_type=F32)

        @pl.when((nsub == n) & (j == nck))
        def _(rows=rows):
            g = g_ref[0:rows]
            a_ref[0:rows] = (g * jax.nn.sigmoid(g) * u_ref[0:rows]).astype(BF16)

    @pl.when(j >= nck)
    def _():
        slot = j % 2
        other = 1 - slot
        col0 = pl.multiple_of((j - nck) * ck, ck)
        for n in range(1, ITEM_SUB + 1):
            rows = n * MOE_BLK

            @pl.when(nsub == n)
            def _(rows=rows):
                o_buf[slot, 0:rows] = jnp.dot(a_ref[0:rows], wd_buf[wslot].astype(BF16),
                                              preferred_element_type=F32)
                pltpu.make_async_copy(o_buf.at[slot, pl.ds(0, rows)],
                                      y_hbm.at[pl.ds(pl.multiple_of(row0_ref[it], MOE_BLK), rows), pl.ds(col0, ck)],
                                      osem.at[slot]).start()

        def drain(s):
            for n in range(1, ITEM_SUB + 1):
                rows = n * MOE_BLK

                @pl.when(pend_ref[s] == n)
                def _(rows=rows):
                    pltpu.make_async_copy(o_buf.at[s, pl.ds(0, rows)],
                                          y_hbm.at[pl.ds(0, rows), pl.ds(0, ck)], osem.at[s]).wait()
            pend_ref[s] = 0

        drain(other)
        pend_ref[slot] = nsub

        @pl.when(j == pl.num_programs(1) - 1)
        def _():
            @pl.when(it + 1 < pl.num_programs(0))
            def _():
                unpack(it + 1)

            @pl.when(it + 2 < pl.num_programs(0))
            def _():
                issue_gather(it + 2)

        @pl.when((it == pl.num_programs(0) - 1) & (j == pl.num_programs(1) - 1))
        def _():
            drain(slot)
            lax.fori_loop(0, n_fill, lambda b, c: (fill_copy(b).wait(), c)[1], 0)


def _experts(tables, buf_tok, slabs, wg, wu, wd, cap):
    item_expert, item_row0, item_nsub, item_skip, n_rows_used = tables
    n_items = item_expert.shape[0]
    n_exp, d, ff = wg.shape
    nck = MOE_CHUNKS
    ck = d // nck
    rows = ITEM_SUB * MOE_BLK
    spt = d // LANES

    return pl.pallas_call(
        functools.partial(_expert_kernel, d=d, nck=nck),
        name="moe_experts",
        grid_spec=pltpu.PrefetchScalarGridSpec(
            num_scalar_prefetch=6,
            grid=(n_items, 2 * nck),
            in_specs=[pl.BlockSpec(memory_space=pl.ANY)] * 4,
            out_specs=pl.BlockSpec(memory_space=pl.ANY),
            scratch_shapes=[
                pltpu.VMEM((rows * spt, LANES), F32),
                pltpu.VMEM((nck, rows, ck), BF16),
                pltpu.VMEM((rows, ff), F32),
                pltpu.VMEM((rows, ff), F32),
                pltpu.VMEM((rows, ff), BF16),
                pltpu.VMEM((2, rows, ck), F32),
                pltpu.VMEM((MOE_BLK, d), F32),
                pltpu.VMEM((W_SLOTS, 2, ck, ff), F32),
                pltpu.VMEM((W_SLOTS, ff, ck), F32),
                pltpu.SMEM((2,), jnp.int32),
                pltpu.SemaphoreType.DMA((ITEM_SUB,)),
                pltpu.SemaphoreType.DMA((2,)),
                pltpu.SemaphoreType.DMA(()),
                pltpu.SemaphoreType.DMA((W_SLOTS,)),
            ],
        ),
        out_shape=jax.ShapeDtypeStruct((cap, d), F32),
        compiler_params=_cparams(2, vmem_mb=EXPERT_VMEM_MB),
    )(item_expert, item_row0, item_nsub, item_skip, n_rows_used, buf_tok, slabs, wg, wu, wd)


def _combine_kernel(dest_ref, y_hbm, h_ref, wts_ref, g_ref, b_ref, o_ref, buf, sem, *, tm, alpha):
    i = pl.program_id(0)
    slot = i % 2

    def issue(tile, dst_slot):
        def start(r, c):
            a = 2 * (tile * tm + r)
            _row_copy(y_hbm, dest_ref[a], buf.at[dst_slot, 0], r, sem.at[dst_slot]).start()
            _row_copy(y_hbm, dest_ref[a + 1], buf.at[dst_slot, 1], r, sem.at[dst_slot]).start()
            return c
        lax.fori_loop(0, tm, start, 0, unroll=8)

    @pl.when(i == 0)
    def _():
        issue(0, 0)

    @pl.when(i + 1 < pl.num_programs(0))
    def _():
        issue(i + 1, 1 - slot)

    for k in range(2):
        pltpu.make_async_copy(y_hbm.at[pl.ds(0, tm)], buf.at[slot, k], sem.at[slot]).wait()
    wts = wts_ref[...]
    f = buf[slot, 0] * wts[:, 0:1] + buf[slot, 1] * wts[:, 1:2]
    o_ref[...] = _layer_norm(alpha * h_ref[...] + f, g_ref[...], b_ref[...])


def _combine_ln(dest, yb, h, wts, g, b, alpha):
    m, d = h.shape
    tm = _pick(m, (256, 128))
    return pl.pallas_call(
        functools.partial(_combine_kernel, tm=tm, alpha=alpha),
        name="moe_combine",
        grid_spec=pltpu.PrefetchScalarGridSpec(
            num_scalar_prefetch=1,
            grid=(m // tm,),
            in_specs=[
                pl.BlockSpec(memory_space=pl.ANY),
                pl.BlockSpec((tm, d), lambda i, dst: (i, 0)),
                pl.BlockSpec((tm, LANES), lambda i, dst: (i, 0)),
                pl.BlockSpec((1, d), lambda i, dst: (0, 0)),
                pl.BlockSpec((1, d), lambda i, dst: (0, 0)),
            ],
            out_specs=pl.BlockSpec((tm, d), lambda i, dst: (i, 0)),
            scratch_shapes=[pltpu.VMEM((2, 2, tm, d), F32), pltpu.SemaphoreType.DMA((2,))],
        ),
        out_shape=jax.ShapeDtypeStruct((m, d), F32),
        compiler_params=_cparams(1),
    )(dest, yb, h, wts, g, b)


def _dispatch_tables(ids, n_tok):
    flat_e = ids[:, :2].reshape(-1)
    n_assign = flat_e.shape[0]
    onehot = (flat_e[:, None] == jnp.arange(N_EXPERTS, dtype=jnp.int32)[None, :]).astype(jnp.int32)
    csum = jnp.cumsum(onehot, axis=0)
    rank = jnp.sum(onehot * csum, axis=1) - 1
    counts = csum[-1]
    padded = (counts + MOE_BLK - 1) // MOE_BLK * MOE_BLK
    pend = jnp.cumsum(padded)
    pstart = pend - padded
    dest = (pstart[flat_e] + rank).astype(jnp.int32)
    cap = -(-n_assign // MOE_BLK) * MOE_BLK + N_EXPERTS * MOE_BLK
    buf_tok = jnp.zeros((cap,), jnp.int32).at[dest].set(jnp.arange(n_assign, dtype=jnp.int32) // 2)

    item_rows = ITEM_SUB * MOE_BLK
    items_e = (padded + item_rows - 1) // item_rows
    item_end = jnp.cumsum(items_e)
    item_start = item_end - items_e
    n_used = item_end[-1]
    n_items = -(-N_EXPERTS * (ITEM_SUB - 1) // ITEM_SUB) + cap // item_rows + 1
    it = jnp.arange(n_items, dtype=jnp.int32)
    e_it = jnp.minimum(jnp.sum((item_end[None, :] <= it[:, None]).astype(jnp.int32), axis=1), N_EXPERTS - 1)
    k_it = it - item_start[e_it]
    used = it < n_used
    e_last = e_it[jnp.maximum(n_used - 1, 0)]
    item_expert = jnp.where(used, e_it, e_last).astype(jnp.int32)
    item_row0 = jnp.where(used, pstart[e_it] + k_it * item_rows, 0).astype(jnp.int32)
    item_nsub = jnp.where(used, jnp.clip((padded[e_it] - k_it * item_rows) // MOE_BLK, 0, ITEM_SUB), 0)
    item_skip = jnp.where(used, 0, 1).astype(jnp.int32)
    n_rows_used = pend[-1].astype(jnp.int32).reshape(1)
    return dest, buf_tok, (item_expert, item_row0, item_nsub.astype(jnp.int32), item_skip, n_rows_used), cap


def _rope_tables(seq):
    half = MLA_ROPE // 2
    freqs = ROPE_THETA ** (-jnp.arange(half, dtype=F32) / half)
    ang = jnp.arange(seq, dtype=F32)[:, None] * freqs
    cos = jnp.cos(ang)
    sin = jnp.sin(ang)
    return jnp.concatenate([cos, cos], -1), jnp.concatenate([sin, sin], -1)


def _swap_halves_neg(w):
    half = w.shape[-1] // 2
    return jnp.concatenate([-w[..., half:], w[..., :half]], axis=-1)


def _layer(x, w_in, w_ukv, g_ckv, w_pa, w_pb, w_o, ln1_g, ln1_b, w_rg, b_rg, w_re, b_re,
           w_gate_e, w_up_e, w_down_e, ln2_g, ln2_b, alpha):
    batch, seq, d = x.shape
    n_tok = batch * seq
    x2 = x.reshape(n_tok, d)

    splits = (DSA_Q_DIM, DSA_KV_DIM, DSA_KV_DIM, IDX_Q_DIM, IDX_DIM, IDX_HEADS, MLA_Q_DIM, KV_RANK, MLA_ROPE, 2 * d)
    offs = np.cumsum((0,) + splits)
    w_dq, w_dk, w_dv, w_iq, w_ik, w_iw, w_mq, w_ckv, w_kr, w_gates = [
        w_in[:, offs[j]:offs[j + 1]] for j in range(len(splits))]
    w_mq3 = w_mq.reshape(d, MLA_HEADS, MLA_QK_DIM)
    w_qn = w_mq3[:, :, :MLA_NOPE].reshape(d, MLA_HEADS * MLA_NOPE)
    w_qr = w_mq3[:, :, MLA_NOPE:]
    w_qs = _swap_halves_neg(w_qr)
    wr = MLA_HEADS * MLA_ROPE
    wb = jnp.concatenate([w_iq, w_dk, w_dv], axis=1).astype(BF16)
    off_k = IDX_Q_DIM
    off_v = off_k + DSA_KV_DIM
    pad_i = jnp.zeros((d, LANES - IDX_DIM - IDX_HEADS), F32)
    ws = jnp.concatenate([w_dq, w_qn, w_qr.reshape(d, wr), w_qs.reshape(d, wr), w_ckv, w_ik, w_iw, pad_i,
                          w_kr, _swap_halves_neg(w_kr)], axis=1).astype(BF16)
    off_qn = DSA_Q_DIM
    off_qr = off_qn + MLA_HEADS * MLA_NOPE
    off_qs = off_qr + wr
    off_c = off_qs + wr
    off_i = off_c + KV_RANK
    off_kr = off_i + LANES

    zb, xb16 = _matmul_cast(x2, wb, BF16)
    zs = _matmul(xb16, ws, F32, tn=ws.shape[1] // 2)
    gates = _matmul(xb16, w_gates.astype(BF16), F32, tn=_pick(2 * d, (2048, 1024, 512, 256)))

    o_a = _dsa_attention(zs, zb, batch, seq, off_i, off_k, off_v)

    cos2, sin2 = _rope_tables(seq)
    w_ukv3 = w_ukv.reshape(KV_RANK, MLA_HEADS, MLA_NOPE + MLA_V)
    w_kv = jnp.concatenate([w_ukv3[:, :, :MLA_NOPE].reshape(KV_RANK, MLA_HEADS * MLA_NOPE),
                            w_ukv3[:, :, MLA_NOPE:].reshape(KV_RANK, MLA_HEADS * MLA_V)], axis=1).astype(BF16)
    q_cat = _mla_q_prep(zs, jnp.tile(cos2, (1, MLA_HEADS)), jnp.tile(sin2, (1, MLA_HEADS)),
                        batch, seq, off_qn, off_qr, off_qs)
    k_cat, v_mla = _mla_kv_prep(zs, g_ckv.reshape(1, KV_RANK), w_kv, cos2, sin2, batch, seq, off_c, off_kr)
    o_b = _mla_flash(q_cat, k_cat, v_mla, batch, seq).reshape(n_tok, MLA_OUT_DIM)

    merged = _gated_merge(o_a, o_b, w_pa.astype(BF16), w_pb.astype(BF16), gates)
    h, slabs = _proj_ln(merged, w_o.astype(BF16), x2, ln1_g.reshape(1, d), ln1_b.reshape(1, d), alpha)

    pad_r = jnp.zeros((d, LANES - N_GROUPS - N_EXPERTS), F32)
    w_r = jnp.concatenate([w_rg, w_re, pad_r], axis=1)
    w_r_hi = w_r.astype(BF16)
    w_r_lo = (w_r - w_r_hi.astype(F32)).astype(BF16)
    b_r = jnp.concatenate([b_rg, b_re, jnp.zeros((LANES - N_GROUPS - N_EXPERTS,), F32)]).reshape(1, LANES)
    ids, wts = _router(h, w_r_hi, w_r_lo, b_r)
    dest, buf_tok, item_tables, cap = _dispatch_tables(ids, n_tok)
    yb = _experts(item_tables, buf_tok, slabs, w_gate_e, w_up_e, w_down_e, cap)
    out = _combine_ln(dest, yb, h, wts, ln2_g.reshape(1, d), ln2_b.reshape(1, d), alpha)
    return out.reshape(batch, seq, d)


def kernel(x, w_in, w_ukv, g_ckv, w_pa, w_pb, w_o, ln1_g, ln1_b, w_rg, b_rg, w_re, b_re,
           w_gate_e, w_up_e, w_down_e, ln2_g, ln2_b):
    depth = w_in.shape[0]
    alpha = (2.0 * depth) ** 0.25
    for l in range(depth):
        x = _layer(x, w_in[l], w_ukv[l], g_ckv[l], w_pa[l], w_pb[l], w_o[l], ln1_g[l], ln1_b[l],
                   w_rg[l], b_rg[l], w_re[l], b_re[l], w_gate_e[l], w_up_e[l], w_down_e[l],
                   ln2_g[l], ln2_b[l], alpha)
    return x
```

```python
import functools

import numpy as np
import jax
import jax.numpy as jnp
from jax import lax
from jax.experimental import pallas as pl
from jax.experimental.pallas import tpu as pltpu

N_DSA_HEADS = 8
DSA_HEAD_DIM = 128
DSA_KV_HEADS = 2
DSA_REP = N_DSA_HEADS // DSA_KV_HEADS
IDX_HEADS = 16
IDX_DIM = 64
TOPK_MAX = 256
MLA_HEADS = 8
MLA_NOPE = 128
MLA_ROPE = 64
MLA_V = 128
KV_RANK = 512
ROPE_THETA = 10000.0
N_GROUPS = 8
EXPERTS_PER_GROUP = 8
N_EXPERTS = N_GROUPS * EXPERTS_PER_GROUP
LN_EPS = 1e-5
RMS_EPS = 1e-6

DSA_Q_DIM = N_DSA_HEADS * DSA_HEAD_DIM
DSA_KV_DIM = DSA_KV_HEADS * DSA_HEAD_DIM
IDX_Q_DIM = IDX_HEADS * IDX_DIM
MLA_QK_DIM = MLA_NOPE + MLA_ROPE
MLA_Q_DIM = MLA_HEADS * MLA_QK_DIM
MLA_OUT_DIM = MLA_HEADS * MLA_V

LANES = 128
VMEM_LIMIT_MB = 56

DSA_TQ = 128
DSA_KC = 512
MLA_T = 512
MOE_BLK = 128
ITEM_SUB = 6
MOE_CHUNKS = 4
W_SLOTS = 3
EXPERT_VMEM_MB = 58
NEG = -1e30
LOG2E = 1.4426950408889634
INT_MIN = -2 ** 31
FLT_MAX = 3.4028234663852886e38

F32 = jnp.float32
BF16 = jnp.bfloat16


def _cparams(n_axes, vmem_mb=VMEM_LIMIT_MB):
    return pltpu.CompilerParams(dimension_semantics=("arbitrary",) * n_axes,
                                vmem_limit_bytes=vmem_mb * 1024 * 1024)


def _pick(n, prefs):
    for p in prefs:
        if n % p == 0:
            return p
    return n


def _lane_tiles(x):
    return [x[:, j * LANES:(j + 1) * LANES] for j in range(x.shape[1] // LANES)]


def _row_max(x):
    return jnp.max(functools.reduce(jnp.maximum, _lane_tiles(x)), axis=1, keepdims=True)


def _row_sum(x):
    return jnp.sum(functools.reduce(jnp.add, _lane_tiles(x)), axis=1, keepdims=True)


def _mm_kernel(a_ref, w_ref, o_ref):
    o_ref[...] = jnp.dot(a_ref[...], w_ref[...], preferred_element_type=F32).astype(o_ref.dtype)


def _matmul(a, w, out_dtype, tn):
    m, k = a.shape
    n = w.shape[1]
    tm = _pick(m, (512, 256, 128))
    return pl.pallas_call(
        _mm_kernel,
        grid=(n // tn, m // tm),
        in_specs=[pl.BlockSpec((tm, k), lambda j, i: (i, 0)),
                  pl.BlockSpec((k, tn), lambda j, i: (0, j))],
        out_specs=pl.BlockSpec((tm, tn), lambda j, i: (i, j)),
        out_shape=jax.ShapeDtypeStruct((m, n), out_dtype),
        compiler_params=_cparams(2),
    )(a, w)


def _mm_cast_kernel(a_ref, w_ref, o_ref, a16_ref):
    a16 = a_ref[...].astype(BF16)
    a16_ref[...] = a16
    o_ref[...] = jnp.dot(a16, w_ref[...], preferred_element_type=F32).astype(o_ref.dtype)


def _matmul_cast(a, w, out_dtype):
    m, k = a.shape
    n = w.shape[1]
    tm = _pick(m, (512, 256, 128))
    return pl.pallas_call(
        _mm_cast_kernel,
        grid=(m // tm,),
        in_specs=[pl.BlockSpec((tm, k), lambda i: (i, 0)),
                  pl.BlockSpec((k, n), lambda i: (0, 0))],
        out_specs=[pl.BlockSpec((tm, n), lambda i: (i, 0)),
                   pl.BlockSpec((tm, k), lambda i: (i, 0))],
        out_shape=[jax.ShapeDtypeStruct((m, n), out_dtype), jax.ShapeDtypeStruct((m, k), BF16)],
        compiler_params=_cparams(1),
    )(a, w)


def _dsa_kernel(q_ref, iq_ref, w_ref, kidx_ref, k_ref, v_ref, o_ref,
                sc_ref, thr_ref, jcut_ref, s_ref, m_ref, l_ref, acc_ref, *, tq, kc, n_sel, seq):
    i = pl.program_id(1)
    nch = (i * tq + tq + kc - 1) // kc
    row = lax.broadcasted_iota(jnp.int32, (tq, kc), 0)
    lane = lax.broadcasted_iota(jnp.int32, (tq, kc), 1)
    qpos = i * tq + row

    iq_all = iq_ref[...]
    iq = jnp.concatenate([iq_all[:, h * IDX_DIM:(h + 1) * IDX_DIM] for h in range(IDX_HEADS)],
                         axis=0)
    w = w_ref[:, IDX_DIM:IDX_DIM + IDX_HEADS] * (IDX_DIM ** -0.5 * IDX_HEADS ** -0.5)

    def score_body(c, carry):
        off = pl.multiple_of(c * kc, kc)
        kx = kidx_ref[pl.ds(off, kc), 0:IDX_DIM].astype(BF16)
        logits = lax.dot_general(iq, kx, (((1,), (1,)), ((), ())), preferred_element_type=F32)
        sc = jnp.zeros((tq, kc), F32)
        for h in range(IDX_HEADS):
            sc = sc + jnp.maximum(logits[h * tq:(h + 1) * tq], 0.0) * w[:, h:h + 1]
        sc_ref[c] = jnp.where(off + lane <= qpos, sc, -jnp.inf)
        return carry

    lax.fori_loop(0, nch, score_body, 0)

    def key_to_f32(key):
        return pltpu.bitcast(jnp.where(key < 0, key ^ jnp.int32(0x7FFFFFFF), key), F32)

    key_lowest = INT_MIN + 0x00800000
    thr_ref[...] = jnp.full((tq, 1), -FLT_MAX, F32)
    jcut_ref[...] = jnp.full((tq, 1), seq, jnp.int32)

    def count(pred_fn):
        def body(c, acc):
            sch = sc_ref[c]
            for j in range(kc // LANES):
                acc = acc + pred_fn(sch[:, j * LANES:(j + 1) * LANES], c * kc + j * LANES)
            return acc
        acc = lax.fori_loop(0, nch, body, jnp.zeros((tq, LANES), jnp.int32))
        return jnp.sum(acc.astype(F32), axis=1, keepdims=True).astype(jnp.int32)

    lane1 = lax.broadcasted_iota(jnp.int32, (tq, LANES), 1)

    @pl.when(i * tq + tq > n_sel)
    def _():
        def bit_body(b, t):
            cand = t + lax.shift_left(jnp.int32(1), 31 - b)
            cand_b = jnp.broadcast_to(key_to_f32(cand), (tq, LANES))
            cnt = count(lambda sch, _: jnp.where(sch >= cand_b, 1, 0))
            return jnp.where(cnt >= n_sel, cand, t)

        t = lax.fori_loop(0, 32, bit_body, jnp.full((tq, 1), INT_MIN, jnp.int32))
        t_f = key_to_f32(jnp.maximum(t, key_lowest))
        thr_ref[...] = t_f
        t_b = jnp.broadcast_to(t_f, (tq, LANES))
        c_ge = count(lambda sch, _: jnp.where(sch >= t_b, 1, 0))
        c_gt = count(lambda sch, _: jnp.where(sch > t_b, 1, 0))
        excess = c_ge > n_sel
        need = n_sel - c_gt

        @pl.when(jnp.max(jnp.where(excess, 1.0, 0.0)) > 0.0)
        def _():
            nbits = max(1, int(np.ceil(np.log2(seq))))

            def jbit(b, jc):
                cand = jc + lax.shift_left(jnp.int32(1), nbits - 1 - b)
                cand_b = jnp.broadcast_to(cand, (tq, LANES))
                cnt = count(lambda sch, base: jnp.where(
                    sch == t_b, jnp.where(base + lane1 < cand_b, 1, 0), 0))
                return jnp.where(cnt < need, cand, jc)

            jc = lax.fori_loop(0, nbits, jbit, jnp.zeros((tq, 1), jnp.int32))
            jcut_ref[...] = jnp.where(excess, jc, seq)

    qs = (q_ref[...] * (DSA_HEAD_DIM ** -0.5 * LOG2E)).astype(BF16)
    qg = [jnp.concatenate([qs[:, (g * DSA_REP + r) * DSA_HEAD_DIM:(g * DSA_REP + r + 1) * DSA_HEAD_DIM]
                           for r in range(DSA_REP)], axis=0) for g in range(DSA_KV_HEADS)]
    m_ref[...] = jnp.full(m_ref.shape, NEG, F32)
    l_ref[...] = jnp.zeros(l_ref.shape, F32)
    acc_ref[...] = jnp.zeros(acc_ref.shape, F32)
    t = thr_ref[...]
    jc = jcut_ref[...]

    def scores(c, slot):
        off = pl.multiple_of(c * kc, kc)
        for g in range(DSA_KV_HEADS):
            s_ref[slot, g] = lax.dot_general(
                qg[g], k_ref[pl.ds(off, kc), g * DSA_HEAD_DIM:(g + 1) * DSA_HEAD_DIM],
                (((1,), (1,)), ((), ())), preferred_element_type=F32)

    def update(c, slot):
        off = pl.multiple_of(c * kc, kc)
        sc = sc_ref[c]
        kpos = off + lane
        sel = jnp.where(kpos <= jc, jnp.where(sc >= t, 1, 0), jnp.where(sc > t, 1, 0)) > 0
        md = jnp.where(sel, (qpos - kpos).astype(F32), -NEG)
        for g in range(DSA_KV_HEADS):
            s = s_ref[slot, g]
            vg = v_ref[pl.ds(off, kc), g * DSA_HEAD_DIM:(g + 1) * DSA_HEAD_DIM]
            e = jnp.concatenate(
                [s[r * tq:(r + 1) * tq] - md * (LOG2E * 2.0 ** (-8.0 * (g * DSA_REP + r + 1) / N_DSA_HEADS))
                 for r in range(DSA_REP)], axis=0)
            m_old = m_ref[g]
            m_new = jnp.maximum(m_old, _row_max(e))
            p = jnp.exp2(e - m_new)
            alpha = jnp.exp2(m_old - m_new)
            l_ref[g] = alpha * l_ref[g] + _row_sum(p)
            acc_ref[g] = alpha * acc_ref[g] + jnp.dot(p.astype(BF16), vg, preferred_element_type=F32)
            m_ref[g] = m_new

    scores(0, 0)
    n_pairs = (nch - 1) // 2

    def att_body(p, carry):
        c = 2 * p
        scores(c + 1, 1)
        update(c, 0)
        scores(c + 2, 0)
        update(c + 1, 1)
        return carry

    lax.fori_loop(0, n_pairs, att_body, 0)
    c_tail = 2 * n_pairs

    @pl.when(nch - c_tail == 2)
    def _():
        scores(c_tail + 1, 1)
        update(c_tail, 0)
        update(c_tail + 1, 1)

    @pl.when(nch - c_tail == 1)
    def _():
        update(c_tail, 0)

    for g in range(DSA_KV_HEADS):
        o = acc_ref[g] / l_ref[g]
        for r in range(DSA_REP):
            hh = g * DSA_REP + r
            o_ref[:, hh * DSA_HEAD_DIM:(hh + 1) * DSA_HEAD_DIM] = o[r * tq:(r + 1) * tq].astype(o_ref.dtype)


def _dsa_attention(zs, zb, batch, seq, off_i, off_k, off_v):
    tq = min(DSA_TQ, seq)
    kc = min(DSA_KC, seq)
    nq = seq // tq
    n_sel = min(TOPK_MAX, seq // 4)
    kern = functools.partial(_dsa_kernel, tq=tq, kc=kc, n_sel=n_sel, seq=seq)
    return pl.pallas_call(
        kern,
        name="dsa",
        grid=(batch, nq),
        in_specs=[
            pl.BlockSpec((tq, DSA_Q_DIM), lambda b, i: (b * nq + i, 0)),
            pl.BlockSpec((tq, IDX_Q_DIM), lambda b, i: (b * nq + i, 0)),
            pl.BlockSpec((tq, LANES), lambda b, i: (b * nq + i, off_i // LANES)),
            pl.BlockSpec((seq, LANES), lambda b, i: (b, off_i // LANES)),
            pl.BlockSpec((seq, DSA_KV_DIM), lambda b, i: (b, off_k // DSA_KV_DIM)),
            pl.BlockSpec((seq, DSA_KV_DIM), lambda b, i: (b, off_v // DSA_KV_DIM)),
        ],
        out_specs=pl.BlockSpec((tq, DSA_Q_DIM), lambda b, i: (b * nq + i, 0)),
        out_shape=jax.ShapeDtypeStruct((batch * seq, DSA_Q_DIM), BF16),
        scratch_shapes=[
            pltpu.VMEM((seq // kc, tq, kc), F32),
            pltpu.VMEM((tq, 1), F32),
            pltpu.VMEM((tq, 1), jnp.int32),
            pltpu.VMEM((2, DSA_KV_HEADS, DSA_REP * tq, kc), F32),
            pltpu.VMEM((DSA_KV_HEADS, DSA_REP * tq, 1), F32),
            pltpu.VMEM((DSA_KV_HEADS, DSA_REP * tq, 1), F32),
            pltpu.VMEM((DSA_KV_HEADS, DSA_REP * tq, DSA_HEAD_DIM), F32),
        ],
        compiler_params=_cparams(2),
    )(zs, zb, zs, zs, zb, zb)


def _mla_q_kernel(qn_ref, qr_ref, qs_ref, cos_ref, sin_ref, o_ref):
    c = MLA_QK_DIM ** -0.5 * LOG2E
    pe = ((qr_ref[...] * cos_ref[...] + qs_ref[...] * sin_ref[...]) * c).astype(o_ref.dtype)
    qn = (qn_ref[...] * c).astype(o_ref.dtype)
    for h in range(MLA_HEADS):
        o_ref[0, h, :, 0:MLA_NOPE] = qn[:, h * MLA_NOPE:(h + 1) * MLA_NOPE]
        o_ref[0, h, :, MLA_NOPE:MLA_QK_DIM] = pe[:, h * MLA_ROPE:(h + 1) * MLA_ROPE]


def _mla_q_prep(zs, cos_q, sin_q, batch, seq, off_qn, off_qr, off_qs):
    tm = _pick(seq, (512, 256, 128))
    ns = seq // tm
    wr = MLA_HEADS * MLA_ROPE
    return pl.pallas_call(
        _mla_q_kernel,
        name="mla_q",
        grid=(batch * ns,),
        in_specs=[
            pl.BlockSpec((tm, MLA_HEADS * MLA_NOPE), lambda i: (i, off_qn // (MLA_HEADS * MLA_NOPE))),
            pl.BlockSpec((tm, wr), lambda i: (i, off_qr // wr)),
            pl.BlockSpec((tm, wr), lambda i: (i, off_qs // wr)),
            pl.BlockSpec((tm, wr), lambda i: (i % ns, 0)),
            pl.BlockSpec((tm, wr), lambda i: (i % ns, 0)),
        ],
        out_specs=pl.BlockSpec((1, MLA_HEADS, tm, MLA_QK_DIM), lambda i: (i // ns, 0, i % ns, 0)),
        out_shape=jax.ShapeDtypeStruct((batch, MLA_HEADS, seq, MLA_QK_DIM), BF16),
        compiler_params=_cparams(1),
    )(zs, zs, zs, cos_q, sin_q)


def _mla_kv_kernel(c_ref, kr_ref, g_ref, w_ref, cos_ref, sin_ref, k_ref, v_ref):
    c = c_ref[...]
    cn = c * lax.rsqrt(jnp.mean(c * c, axis=-1, keepdims=True) + RMS_EPS) * g_ref[...]
    kv = jnp.dot(cn.astype(BF16), w_ref[...], preferred_element_type=F32)
    kr = kr_ref[...]
    pe = (kr[:, 0:MLA_ROPE] * cos_ref[...] + kr[:, MLA_ROPE:2 * MLA_ROPE] * sin_ref[...]).astype(k_ref.dtype)
    for h in range(MLA_HEADS):
        k_ref[0, h, :, 0:MLA_NOPE] = kv[:, h * MLA_NOPE:(h + 1) * MLA_NOPE].astype(k_ref.dtype)
        k_ref[0, h, :, MLA_NOPE:MLA_QK_DIM] = pe
        v0 = MLA_HEADS * MLA_NOPE + h * MLA_V
        v_ref[0, h] = kv[:, v0:v0 + MLA_V].astype(v_ref.dtype)


def _mla_kv_prep(zs, g_ckv, w_kv, cos_k, sin_k, batch, seq, off_c, off_kr):
    tm = _pick(seq, (512, 256, 128))
    ns = seq // tm
    nkv = MLA_HEADS * (MLA_NOPE + MLA_V)
    return pl.pallas_call(
        _mla_kv_kernel,
        name="mla_kv",
        grid=(batch * ns,),
        in_specs=[
            pl.BlockSpec((tm, KV_RANK), lambda i: (i, off_c // KV_RANK)),
            pl.BlockSpec((tm, 2 * MLA_ROPE), lambda i: (i, off_kr // (2 * MLA_ROPE))),
            pl.BlockSpec((1, KV_RANK), lambda i: (0, 0)),
            pl.BlockSpec((KV_RANK, nkv), lambda i: (0, 0)),
            pl.BlockSpec((tm, MLA_ROPE), lambda i: (i % ns, 0)),
            pl.BlockSpec((tm, MLA_ROPE), lambda i: (i % ns, 0)),
        ],
        out_specs=[
            pl.BlockSpec((1, MLA_HEADS, tm, MLA_QK_DIM), lambda i: (i // ns, 0, i % ns, 0)),
            pl.BlockSpec((1, MLA_HEADS, tm, MLA_V), lambda i: (i // ns, 0, i % ns, 0)),
        ],
        out_shape=[jax.ShapeDtypeStruct((batch, MLA_HEADS, seq, MLA_QK_DIM), BF16),
                   jax.ShapeDtypeStruct((batch, MLA_HEADS, seq, MLA_V), BF16)],
        compiler_params=_cparams(1),
    )(zs, zs, g_ckv, w_kv, cos_k, sin_k)


def _mla_flash_kernel(q_ref, k_ref, v_ref, o_ref, s_ref, m_ref, l_ref, acc_ref, *, t):
    qi = pl.program_id(2)
    q = q_ref[0, 0]
    m_ref[...] = jnp.full(m_ref.shape, NEG, F32)
    l_ref[...] = jnp.zeros(l_ref.shape, F32)
    acc_ref[...] = jnp.zeros(acc_ref.shape, F32)

    def scores(j, slot):
        off = pl.multiple_of(j * t, t)
        s_ref[slot] = lax.dot_general(q, k_ref[0, 0, pl.ds(off, t), :], (((1,), (1,)), ((), ())),
                                      preferred_element_type=F32)

    def update(j, slot, diagonal):
        off = pl.multiple_of(j * t, t)
        s = s_ref[slot]
        if diagonal:
            row = lax.broadcasted_iota(jnp.int32, (t, t), 0)
            col = lax.broadcasted_iota(jnp.int32, (t, t), 1)
            s = jnp.where(col <= row, s, NEG)
        m_old = m_ref[...]
        m_new = jnp.maximum(m_old, _row_max(s))
        p = jnp.exp2(s - m_new)
        alpha = jnp.exp2(m_old - m_new)
        l_ref[...] = alpha * l_ref[...] + _row_sum(p)
        acc_ref[...] = alpha * acc_ref[...] + jnp.dot(p.astype(BF16), v_ref[0, 0, pl.ds(off, t), :],
                                                      preferred_element_type=F32)
        m_ref[...] = m_new

    scores(0, 0)
    n_pairs = qi // 2

    def body(p, carry):
        j = 2 * p
        scores(j + 1, 1)
        update(j, 0, False)
        scores(j + 2, 0)
        update(j + 1, 1, False)
        return carry

    lax.fori_loop(0, n_pairs, body, 0)
    j_tail = 2 * n_pairs

    @pl.when(qi - j_tail == 1)
    def _():
        scores(j_tail + 1, 1)
        update(j_tail, 0, False)
        update(j_tail + 1, 1, True)

    @pl.when(qi == j_tail)
    def _():
        update(j_tail, 0, True)

    o_ref[0] = (acc_ref[...] / l_ref[...]).astype(o_ref.dtype)


def _mla_flash(q_cat, k_cat, v, batch, seq):
    t = min(MLA_T, seq)
    n = seq // t
    kern = functools.partial(_mla_flash_kernel, t=t)
    return pl.pallas_call(
        kern,
        name="mla_flash",
        grid=(batch, MLA_HEADS, n),
        in_specs=[
            pl.BlockSpec((1, 1, t, MLA_QK_DIM), lambda b, h, qi: (b, h, qi, 0)),
            pl.BlockSpec((1, 1, seq, MLA_QK_DIM), lambda b, h, qi: (b, h, 0, 0)),
            pl.BlockSpec((1, 1, seq, MLA_V), lambda b, h, qi: (b, h, 0, 0)),
        ],
        out_specs=pl.BlockSpec((1, t, MLA_V), lambda b, h, qi: (b, qi, h)),
        out_shape=jax.ShapeDtypeStruct((batch, seq, MLA_OUT_DIM), BF16),
        scratch_shapes=[pltpu.VMEM((2, t, t), F32), pltpu.VMEM((t, 1), F32), pltpu.VMEM((t, 1), F32),
                        pltpu.VMEM((t, MLA_V), F32)],
        compiler_params=_cparams(3),
    )(q_cat, k_cat, v)


def _merge_kernel(oa_ref, ob_ref, wa_ref, wb_ref, ga_ref, gb_ref, o_ref):
    pa = jnp.dot(oa_ref[...], wa_ref[...], preferred_element_type=F32)
    pb = jnp.dot(ob_ref[...], wb_ref[...], preferred_element_type=F32)
    o_ref[...] = (jax.nn.sigmoid(ga_ref[...]) * pa + jax.nn.sigmoid(gb_ref[...]) * pb).astype(o_ref.dtype)


def _gated_merge(o_a, o_b, w_pa, w_pb, gates):
    m = o_a.shape[0]
    d = w_pa.shape[1]
    tm = _pick(m, (512, 256, 128))
    tn = _pick(d, (1024, 512, 256, 128))
    nn = d // tn
    return pl.pallas_call(
        _merge_kernel,
        name="gated_merge",
        grid=(nn, m // tm),
        in_specs=[
            pl.BlockSpec((tm, o_a.shape[1]), lambda j, i: (i, 0)),
            pl.BlockSpec((tm, o_b.shape[1]), lambda j, i: (i, 0)),
            pl.BlockSpec((w_pa.shape[0], tn), lambda j, i: (0, j)),
            pl.BlockSpec((w_pb.shape[0], tn), lambda j, i: (0, j)),
            pl.BlockSpec((tm, tn), lambda j, i: (i, j)),
            pl.BlockSpec((tm, tn), lambda j, i: (i, j + nn)),
        ],
        out_specs=pl.BlockSpec((tm, tn), lambda j, i: (i, j)),
        out_shape=jax.ShapeDtypeStruct((m, d), BF16),
        compiler_params=_cparams(2),
    )(o_a, o_b, w_pa, w_pb, gates, gates)


def _layer_norm(y, g, b):
    mu = jnp.mean(y, axis=-1, keepdims=True)
    dlt = y - mu
    var = jnp.mean(dlt * dlt, axis=-1, keepdims=True)
    return dlt * lax.rsqrt(var + LN_EPS) * g + b


def _proj_ln_kernel(a_ref, w_ref, x_ref, g_ref, b_ref, o_ref, slab_ref, *, alpha):
    y = alpha * x_ref[...] + jnp.dot(a_ref[...], w_ref[...], preferred_element_type=F32)
    h = _layer_norm(y, g_ref[...], b_ref[...])
    o_ref[...] = h
    tm, d = h.shape
    spt = d // LANES
    for s in range(spt):
        slab_ref[pl.ds(s, tm, stride=spt), :] = h[:, s * LANES:(s + 1) * LANES]


def _proj_ln(a, w, x, g, b, alpha):
    m, d = x.shape
    tm = _pick(m, (512, 256, 128))
    spt = d // LANES
    return pl.pallas_call(
        functools.partial(_proj_ln_kernel, alpha=alpha),
        name="proj_ln",
        grid=(m // tm,),
        in_specs=[
            pl.BlockSpec((tm, a.shape[1]), lambda i: (i, 0)),
            pl.BlockSpec(w.shape, lambda i: (0, 0)),
            pl.BlockSpec((tm, d), lambda i: (i, 0)),
            pl.BlockSpec((1, d), lambda i: (0, 0)),
            pl.BlockSpec((1, d), lambda i: (0, 0)),
        ],
        out_specs=[pl.BlockSpec((tm, d), lambda i: (i, 0)),
                   pl.BlockSpec((tm * spt, LANES), lambda i: (i, 0))],
        out_shape=[jax.ShapeDtypeStruct((m, d), F32),
                   jax.ShapeDtypeStruct((m * spt, LANES), F32)],
        compiler_params=_cparams(1),
    )(a, w, x, g, b)


def _router_kernel(h_ref, whi_ref, wlo_ref, b_ref, ids_ref, wts_ref):
    h = h_ref[...]
    h_hi = h.astype(BF16)
    h_lo = (h - h_hi.astype(F32)).astype(BF16)
    logits = (jnp.dot(h_hi, whi_ref[...], preferred_element_type=F32)
              + jnp.dot(h_hi, wlo_ref[...], preferred_element_type=F32)
              + jnp.dot(h_lo, whi_ref[...], preferred_element_type=F32)) + b_ref[...]
    tm = h.shape[0]
    lane = lax.broadcasted_iota(jnp.int32, (tm, LANES), 1)
    lane_f = lane.astype(F32)
    big = float(4 * LANES)
    ninf = -jnp.inf
    gl = jnp.where(lane < N_GROUPS, logits, ninf)
    ge = jnp.exp(gl - jnp.max(gl, axis=1, keepdims=True))
    gprob = ge / jnp.sum(ge, axis=1, keepdims=True)
    gw = jnp.max(gprob, axis=1, keepdims=True)
    gidx = jnp.min(jnp.where(gprob == gw, lane_f, big), axis=1, keepdims=True).astype(jnp.int32)
    in_group = lax.shift_right_arithmetic(lane - N_GROUPS, 3) == gidx
    el = jnp.where(in_group, logits, ninf)
    ee = jnp.exp(el - jnp.max(el, axis=1, keepdims=True))
    ep = jnp.where(in_group, ee / jnp.sum(ee, axis=1, keepdims=True), -1.0)
    p1 = jnp.max(ep, axis=1, keepdims=True)
    i1 = jnp.min(jnp.where(ep == p1, lane_f, big), axis=1, keepdims=True).astype(jnp.int32)
    ep2 = jnp.where(lane == i1, -1.0, ep)
    p2 = jnp.max(ep2, axis=1, keepdims=True)
    i2 = jnp.min(jnp.where(ep2 == p2, lane_f, big), axis=1, keepdims=True).astype(jnp.int32)
    den = p1 + p2
    w1 = gw * (p1 / den)
    w2 = gw * (p2 / den)
    ids_ref[...] = jnp.where(lane == 0, i1 - N_GROUPS, jnp.where(lane == 1, i2 - N_GROUPS, 0))
    wts_ref[...] = jnp.where(lane == 0, w1, jnp.where(lane == 1, w2, 0.0))


def _router(h, w_hi, w_lo, b_r):
    m, d = h.shape
    tm = _pick(m, (512, 256, 128))
    return pl.pallas_call(
        _router_kernel,
        name="router",
        grid=(m // tm,),
        in_specs=[
            pl.BlockSpec((tm, d), lambda i: (i, 0)),
            pl.BlockSpec((d, LANES), lambda i: (0, 0)),
            pl.BlockSpec((d, LANES), lambda i: (0, 0)),
            pl.BlockSpec((1, LANES), lambda i: (0, 0)),
        ],
        out_specs=[pl.BlockSpec((tm, LANES), lambda i: (i, 0)),
                   pl.BlockSpec((tm, LANES), lambda i: (i, 0))],
        out_shape=[jax.ShapeDtypeStruct((m, LANES), jnp.int32),
                   jax.ShapeDtypeStruct((m, LANES), F32)],
        compiler_params=_cparams(1),
    )(h, w_hi, w_lo, b_r)


def _row_copy(src_hbm, src_row, dst_vmem, dst_row, sem):
    return pltpu.make_async_copy(src_hbm.at[pl.ds(src_row, 1)], dst_vmem.at[pl.ds(dst_row, 1)], sem)


def _expert_kernel(ie_ref, row0_ref, nsub_ref, skip_ref, nrows_ref, tok_ref,
                   slab_hbm, wg_hbm, wu_hbm, wd_hbm, y_hbm,
                   slab_buf, x_ref, g_ref, u_ref, a_ref, o_buf, zbuf, wgu_buf, wd_buf, pend_ref,
                   gsem, osem, zsem, wsem, *, d, nck):
    it = pl.program_id(0)
    j = pl.program_id(1)
    n_items = pl.num_programs(0)
    nj = pl.num_programs(1)
    nsub = nsub_ref[it]
    spt = d // LANES
    ck = d // nck
    sub_rows = MOE_BLK * spt
    n_fill = (y_hbm.shape[0] - nrows_ref[0]) // MOE_BLK
    step = it * nj + j
    wslot = step % W_SLOTS

    def weight_copies(item, jj, slot, fn):
        e = ie_ref[item]

        @pl.when(jj < nck)
        def _():
            k0 = pl.multiple_of(jj * ck, ck)
            fn(pltpu.make_async_copy(wg_hbm.at[e, pl.ds(k0, ck), :], wgu_buf.at[slot, 0], wsem.at[slot]))
            fn(pltpu.make_async_copy(wu_hbm.at[e, pl.ds(k0, ck), :], wgu_buf.at[slot, 1], wsem.at[slot]))

        @pl.when(jj >= nck)
        def _():
            c0 = pl.multiple_of((jj - nck) * ck, ck)
            fn(pltpu.make_async_copy(wd_hbm.at[e, :, pl.ds(c0, ck)], wd_buf.at[slot], wsem.at[slot]))

    def fetch_weights(s):
        item = s // nj

        @pl.when(item < n_items)
        def _():
            @pl.when(skip_ref[item] == 0)
            def _():
                weight_copies(item, s % nj, s % W_SLOTS, lambda cp: cp.start())

    def fill_copy(b):
        row = pl.multiple_of(nrows_ref[0] + b * MOE_BLK, MOE_BLK)
        return pltpu.make_async_copy(zbuf, y_hbm.at[pl.ds(row, MOE_BLK)], zsem)

    def issue_gather(item):
        for sb in range(ITEM_SUB):
            @pl.when(sb < nsub_ref[item])
            def _(sb=sb):
                base = row0_ref[item] + sb * MOE_BLK

                def start(r, c):
                    src = pl.multiple_of(tok_ref[base + r] * spt, spt)
                    dst = pl.multiple_of((sb * MOE_BLK + r) * spt, spt)
                    pltpu.make_async_copy(slab_hbm.at[pl.ds(src, spt)], slab_buf.at[pl.ds(dst, spt)],
                                          gsem.at[sb]).start()
                    return c
                lax.fori_loop(0, MOE_BLK, start, 0, unroll=8)

    def unpack(item):
        for sb in range(ITEM_SUB):
            @pl.when(sb < nsub_ref[item])
            def _(sb=sb):
                pltpu.make_async_copy(slab_hbm.at[pl.ds(0, sub_rows)],
                                      slab_buf.at[pl.ds(sb * sub_rows, sub_rows)], gsem.at[sb]).wait()
                for s in range(spt):
                    col = s * LANES
                    x_ref[col // ck, sb * MOE_BLK:(sb + 1) * MOE_BLK, col % ck:col % ck + LANES] = (
                        slab_buf[pl.ds(sb * sub_rows + s, MOE_BLK, stride=spt), :].astype(BF16))

    @pl.when((it == 0) & (j == 0))
    def _():
        pend_ref[0] = 0
        pend_ref[1] = 0
        fetch_weights(0)
        fetch_weights(1)
        issue_gather(0)
        zbuf[...] = jnp.zeros(zbuf.shape, F32)
        lax.fori_loop(0, n_fill, lambda b, c: (fill_copy(b).start(), c)[1], 0)
        unpack(0)

        @pl.when(1 < pl.num_programs(0))
        def _():
            issue_gather(1)

    fetch_weights(step + 2)

    @pl.when(nsub > 0)
    def _():
        weight_copies(it, j, wslot, lambda cp: cp.wait())

    for n in range(1, ITEM_SUB + 1):
        rows = n * MOE_BLK

        @pl.when((nsub == n) & (j == 0))
        def _(rows=rows):
            xk = x_ref[0, 0:rows, :]
            g_ref[0:rows] = jnp.dot(xk, wgu_buf[wslot, 0].astype(BF16), preferred_element_type=F32)
            u_ref[0:rows] = jnp.dot(xk, wgu_buf[wslot, 1].astype(BF16), preferred_element_type=F32)

        @pl.when((nsub == n) & (j > 0) & (j < nck))
        def _(rows=rows):
            xk = x_ref[j, 0:rows, :]
            g_ref[0:rows] += jnp.dot(xk, wgu_buf[wslot, 0].astype(BF16), preferred_element_type=F32)
            u_ref[0:rows] += jnp.dot(xk, wgu_buf[wslot, 1].astype(BF16), preferred_element_type=F32)

        @pl.when((nsub == n) & (j == nck))
        def _(rows=rows):
            g = g_ref[0:rows]
            a_ref[0:rows] = (g * jax.nn.sigmoid(g) * u_ref[0:rows]).astype(BF16)

    @pl.when(j >= nck)
    def _():
        slot = j % 2
        other = 1 - slot
        col0 = pl.multiple_of((j - nck) * ck, ck)
        for n in range(1, ITEM_SUB + 1):
            rows = n * MOE_BLK

            @pl.when(nsub == n)
            def _(rows=rows):
                o_buf[slot, 0:rows] = jnp.dot(a_ref[0:rows], wd_buf[wslot].astype(BF16),
                                              preferred_element_type=F32)
                pltpu.make_async_copy(o_buf.at[slot, pl.ds(0, rows)],
                                      y_hbm.at[pl.ds(pl.multiple_of(row0_ref[it], MOE_BLK), rows), pl.ds(col0, ck)],
                                      osem.at[slot]).start()

        def drain(s):
            for n in range(1, ITEM_SUB + 1):
                rows = n * MOE_BLK

                @pl.when(pend_ref[s] == n)
                def _(rows=rows):
                    pltpu.make_async_copy(o_buf.at[s, pl.ds(0, rows)],
                                          y_hbm.at[pl.ds(0, rows), pl.ds(0, ck)], osem.at[s]).wait()
            pend_ref[s] = 0

        drain(other)
        pend_ref[slot] = nsub

        @pl.when(j == pl.num_programs(1) - 1)
        def _():
            @pl.when(it + 1 < pl.num_programs(0))
            def _():
                unpack(it + 1)

            @pl.when(it + 2 < pl.num_programs(0))
            def _():
                issue_gather(it + 2)

        @pl.when((it == pl.num_programs(0) - 1) & (j == pl.num_programs(1) - 1))
        def _():
            drain(slot)
            lax.fori_loop(0, n_fill, lambda b, c: (fill_copy(b).wait(), c)[1], 0)


def _experts(tables, buf_tok, slabs, wg, wu, wd, cap):
    item_expert, item_row0, item_nsub, item_skip, n_rows_used = tables
    n_items = item_expert.shape[0]
    n_exp, d, ff = wg.shape
    nck = MOE_CHUNKS
    ck = d // nck
    rows = ITEM_SUB * MOE_BLK
    spt = d // LANES

    return pl.pallas_call(
        functools.partial(_expert_kernel, d=d, nck=nck),
        name="moe_experts",
        grid_spec=pltpu.PrefetchScalarGridSpec(
            num_scalar_prefetch=6,
            grid=(n_items, 2 * nck),
            in_specs=[pl.BlockSpec(memory_space=pl.ANY)] * 4,
            out_specs=pl.BlockSpec(memory_space=pl.ANY),
            scratch_shapes=[
                pltpu.VMEM((rows * spt, LANES), F32),
                pltpu.VMEM((nck, rows, ck), BF16),
                pltpu.VMEM((rows, ff), F32),
                pltpu.VMEM((rows, ff), F32),
                pltpu.VMEM((rows, ff), BF16),
                pltpu.VMEM((2, rows, ck), F32),
                pltpu.VMEM((MOE_BLK, d), F32),
                pltpu.VMEM((W_SLOTS, 2, ck, ff), F32),
                pltpu.VMEM((W_SLOTS, ff, ck), F32),
                pltpu.SMEM((2,), jnp.int32),
                pltpu.SemaphoreType.DMA((ITEM_SUB,)),
                pltpu.SemaphoreType.DMA((2,)),
                pltpu.SemaphoreType.DMA(()),
                pltpu.SemaphoreType.DMA((W_SLOTS,)),
            ],
        ),
        out_shape=jax.ShapeDtypeStruct((cap, d), F32),
        compiler_params=_cparams(2, vmem_mb=EXPERT_VMEM_MB),
    )(item_expert, item_row0, item_nsub, item_skip, n_rows_used, buf_tok, slabs, wg, wu, wd)


def _combine_kernel(dest_ref, y_hbm, h_ref, wts_ref, g_ref, b_ref, o_ref, buf, sem, *, tm, alpha):
    i = pl.program_id(0)
    slot = i % 2

    def issue(tile, dst_slot):
        def start(r, c):
            a = 2 * (tile * tm + r)
            _row_copy(y_hbm, dest_ref[a], buf.at[dst_slot, 0], r, sem.at[dst_slot]).start()
            _row_copy(y_hbm, dest_ref[a + 1], buf.at[dst_slot, 1], r, sem.at[dst_slot]).start()
            return c
        lax.fori_loop(0, tm, start, 0, unroll=8)

    @pl.when(i == 0)
    def _():
        issue(0, 0)

    @pl.when(i + 1 < pl.num_programs(0))
    def _():
        issue(i + 1, 1 - slot)

    for k in range(2):
        pltpu.make_async_copy(y_hbm.at[pl.ds(0, tm)], buf.at[slot, k], sem.at[slot]).wait()
    wts = wts_ref[...]
    f = buf[slot, 0] * wts[:, 0:1] + buf[slot, 1] * wts[:, 1:2]
    o_ref[...] = _layer_norm(alpha * h_ref[...] + f, g_ref[...], b_ref[...])


def _combine_ln(dest, yb, h, wts, g, b, alpha):
    m, d = h.shape
    tm = _pick(m, (256, 128))
    return pl.pallas_call(
        functools.partial(_combine_kernel, tm=tm, alpha=alpha),
        name="moe_combine",
        grid_spec=pltpu.PrefetchScalarGridSpec(
            num_scalar_prefetch=1,
            grid=(m // tm,),
            in_specs=[
                pl.BlockSpec(memory_space=pl.ANY),
                pl.BlockSpec((tm, d), lambda i, dst: (i, 0)),
                pl.BlockSpec((tm, LANES), lambda i, dst: (i, 0)),
                pl.BlockSpec((1, d), lambda i, dst: (0, 0)),
                pl.BlockSpec((1, d), lambda i, dst: (0, 0)),
            ],
            out_specs=pl.BlockSpec((tm, d), lambda i, dst: (i, 0)),
            scratch_shapes=[pltpu.VMEM((2, 2, tm, d), F32), pltpu.SemaphoreType.DMA((2,))],
        ),
        out_shape=jax.ShapeDtypeStruct((m, d), F32),
        compiler_params=_cparams(1),
    )(dest, yb, h, wts, g, b)


def _dispatch_tables(ids, n_tok):
    flat_e = ids[:, :2].reshape(-1)
    n_assign = flat_e.shape[0]
    onehot = (flat_e[:, None] == jnp.arange(N_EXPERTS, dtype=jnp.int32)[None, :]).astype(jnp.int32)
    csum = jnp.cumsum(onehot, axis=0)
    rank = jnp.sum(onehot * csum, axis=1) - 1
    counts = csum[-1]
    padded = (counts + MOE_BLK - 1) // MOE_BLK * MOE_BLK
    pend = jnp.cumsum(padded)
    pstart = pend - padded
    dest = (pstart[flat_e] + rank).astype(jnp.int32)
    cap = -(-n_assign // MOE_BLK) * MOE_BLK + N_EXPERTS * MOE_BLK
    buf_tok = jnp.zeros((cap,), jnp.int32).at[dest].set(jnp.arange(n_assign, dtype=jnp.int32) // 2)

    item_rows = ITEM_SUB * MOE_BLK
    items_e = (padded + item_rows - 1) // item_rows
    item_end = jnp.cumsum(items_e)
    item_start = item_end - items_e
    n_used = item_end[-1]
    n_items = -(-N_EXPERTS * (ITEM_SUB - 1) // ITEM_SUB) + cap // item_rows + 1
    it = jnp.arange(n_items, dtype=jnp.int32)
    e_it = jnp.minimum(jnp.sum((item_end[None, :] <= it[:, None]).astype(jnp.int32), axis=1), N_EXPERTS - 1)
    k_it = it - item_start[e_it]
    used = it < n_used
    e_last = e_it[jnp.maximum(n_used - 1, 0)]
    item_expert = jnp.where(used, e_it, e_last).astype(jnp.int32)
    item_row0 = jnp.where(used, pstart[e_it] + k_it * item_rows, 0).astype(jnp.int32)
    item_nsub = jnp.where(used, jnp.clip((padded[e_it] - k_it * item_rows) // MOE_BLK, 0, ITEM_SUB), 0)
    item_skip = jnp.where(used, 0, 1).astype(jnp.int32)
    n_rows_used = pend[-1].astype(jnp.int32).reshape(1)
    return dest, buf_tok, (item_expert, item_row0, item_nsub.astype(jnp.int32), item_skip, n_rows_used), cap


def _rope_tables(seq):
    half = MLA_ROPE // 2
    freqs = ROPE_THETA ** (-jnp.arange(half, dtype=F32) / half)
    ang = jnp.arange(seq, dtype=F32)[:, None] * freqs
    cos = jnp.cos(ang)
    sin = jnp.sin(ang)
    return jnp.concatenate([cos, cos], -1), jnp.concatenate([sin, sin], -1)


def _swap_halves_neg(w):
    half = w.shape[-1] // 2
    return jnp.concatenate([-w[..., half:], w[..., :half]], axis=-1)


def _layer(x, w_in, w_ukv, g_ckv, w_pa, w_pb, w_o, ln1_g, ln1_b, w_rg, b_rg, w_re, b_re,
           w_gate_e, w_up_e, w_down_e, ln2_g, ln2_b, alpha):
    batch, seq, d = x.shape
    n_tok = batch * seq
    x2 = x.reshape(n_tok, d)

    splits = (DSA_Q_DIM, DSA_KV_DIM, DSA_KV_DIM, IDX_Q_DIM, IDX_DIM, IDX_HEADS, MLA_Q_DIM, KV_RANK, MLA_ROPE, 2 * d)
    offs = np.cumsum((0,) + splits)
    w_dq, w_dk, w_dv, w_iq, w_ik, w_iw, w_mq, w_ckv, w_kr, w_gates = [
        w_in[:, offs[j]:offs[j + 1]] for j in range(len(splits))]
    w_mq3 = w_mq.reshape(d, MLA_HEADS, MLA_QK_DIM)
    w_qn = w_mq3[:, :, :MLA_NOPE].reshape(d, MLA_HEADS * MLA_NOPE)
    w_qr = w_mq3[:, :, MLA_NOPE:]
    w_qs = _swap_halves_neg(w_qr)
    wr = MLA_HEADS * MLA_ROPE
    wb = jnp.concatenate([w_iq, w_dk, w_dv], axis=1).astype(BF16)
    off_k = IDX_Q_DIM
    off_v = off_k + DSA_KV_DIM
    pad_i = jnp.zeros((d, LANES - IDX_DIM - IDX_HEADS), F32)
    ws = jnp.concatenate([w_dq, w_qn, w_qr.reshape(d, wr), w_qs.reshape(d, wr), w_ckv, w_ik, w_iw, pad_i,
                          w_kr, _swap_halves_neg(w_kr)], axis=1).astype(BF16)
    off_qn = DSA_Q_DIM
    off_qr = off_qn + MLA_HEADS * MLA_NOPE
    off_qs = off_qr + wr
    off_c = off_qs + wr
    off_i = off_c + KV_RANK
    off_kr = off_i + LANES

    zb, xb16 = _matmul_cast(x2, wb, BF16)
    zs = _matmul(xb16, ws, F32, tn=ws.shape[1] // 2)
    gates = _matmul(xb16, w_gates.astype(BF16), F32, tn=_pick(2 * d, (2048, 1024, 512, 256)))

    o_a = _dsa_attention(zs, zb, batch, seq, off_i, off_k, off_v)

    cos2, sin2 = _rope_tables(seq)
    w_ukv3 = w_ukv.reshape(KV_RANK, MLA_HEADS, MLA_NOPE + MLA_V)
    w_kv = jnp.concatenate([w_ukv3[:, :, :MLA_NOPE].reshape(KV_RANK, MLA_HEADS * MLA_NOPE),
                            w_ukv3[:, :, MLA_NOPE:].reshape(KV_RANK, MLA_HEADS * MLA_V)], axis=1).astype(BF16)
    q_cat = _mla_q_prep(zs, jnp.tile(cos2, (1, MLA_HEADS)), jnp.tile(sin2, (1, MLA_HEADS)),
                        batch, seq, off_qn, off_qr, off_qs)
    k_cat, v_mla = _mla_kv_prep(zs, g_ckv.reshape(1, KV_RANK), w_kv, cos2, sin2, batch, seq, off_c, off_kr)
    o_b = _mla_flash(q_cat, k_cat, v_mla, batch, seq).reshape(n_tok, MLA_OUT_DIM)

    merged = _gated_merge(o_a, o_b, w_pa.astype(BF16), w_pb.astype(BF16), gates)
    h, slabs = _proj_ln(merged, w_o.astype(BF16), x2, ln1_g.reshape(1, d), ln1_b.reshape(1, d), alpha)

    pad_r = jnp.zeros((d, LANES - N_GROUPS - N_EXPERTS), F32)
    w_r = jnp.concatenate([w_rg, w_re, pad_r], axis=1)
    w_r_hi = w_r.astype(BF16)
    w_r_lo = (w_r - w_r_hi.astype(F32)).astype(BF16)
    b_r = jnp.concatenate([b_rg, b_re, jnp.zeros((LANES - N_GROUPS - N_EXPERTS,), F32)]).reshape(1, LANES)
    ids, wts = _router(h, w_r_hi, w_r_lo, b_r)
    dest, buf_tok, item_tables, cap = _dispatch_tables(ids, n_tok)
    yb = _experts(item_tables, buf_tok, slabs, w_gate_e, w_up_e, w_down_e, cap)
    out = _combine_ln(dest, yb, h, wts, ln2_g.reshape(1, d), ln2_b.reshape(1, d), alpha)
    return out.reshape(batch, seq, d)


def kernel(x, w_in, w_ukv, g_ckv, w_pa, w_pb, w_o, ln1_g, ln1_b, w_rg, b_rg, w_re, b_re,
           w_gate_e, w_up_e, w_down_e, ln2_g, ln2_b):
    depth = w_in.shape[0]
    alpha = (2.0 * depth) ** 0.25
    for l in range(depth):
        x = _layer(x, w_in[l], w_ukv[l], g_ckv[l], w_pa[l], w_pb[l], w_o[l], ln1_g[l], ln1_b[l],
                   w_rg[l], b_rg[l], w_re[l], b_re[l], w_gate_e[l], w_up_e[l], w_down_e[l],
                   ln2_g[l], ln2_b[l], alpha)
    return x
```

```python
import functools

import numpy as np
import jax
import jax.numpy as jnp
from jax import lax
from jax.experimental import pallas as pl
from jax.experimental.pallas import tpu as pltpu

N_DSA_HEADS = 8
DSA_HEAD_DIM = 128
DSA_KV_HEADS = 2
DSA_REP = N_DSA_HEADS // DSA_KV_HEADS
IDX_HEADS = 16
IDX_DIM = 64
TOPK_MAX = 256
MLA_HEADS = 8
MLA_NOPE = 128
MLA_ROPE = 64
MLA_V = 128
KV_RANK = 512
ROPE_THETA = 10000.0
N_GROUPS = 8
EXPERTS_PER_GROUP = 8
N_EXPERTS = N_GROUPS * EXPERTS_PER_GROUP
LN_EPS = 1e-5
RMS_EPS = 1e-6

DSA_Q_DIM = N_DSA_HEADS * DSA_HEAD_DIM
DSA_KV_DIM = DSA_KV_HEADS * DSA_HEAD_DIM
IDX_Q_DIM = IDX_HEADS * IDX_DIM
MLA_QK_DIM = MLA_NOPE + MLA_ROPE
MLA_Q_DIM = MLA_HEADS * MLA_QK_DIM
MLA_OUT_DIM = MLA_HEADS * MLA_V

LANES = 128
VMEM_LIMIT_MB = 56

DSA_TQ = 128
DSA_KC = 512
MLA_T = 512
MOE_BLK = 128
ITEM_SUB = 6
MOE_CHUNKS = 4
W_SLOTS = 3
EXPERT_VMEM_MB = 58
NEG = -1e30
LOG2E = 1.4426950408889634
INT_MIN = -2 ** 31
FLT_MAX = 3.4028234663852886e38

F32 = jnp.float32
BF16 = jnp.bfloat16


def _cparams(n_axes, vmem_mb=VMEM_LIMIT_MB):
    return pltpu.CompilerParams(dimension_semantics=("arbitrary",) * n_axes,
                                vmem_limit_bytes=vmem_mb * 1024 * 1024)


def _pick(n, prefs):
    for p in prefs:
        if n % p == 0:
            return p
    return n


def _lane_tiles(x):
    return [x[:, j * LANES:(j + 1) * LANES] for j in range(x.shape[1] // LANES)]


def _row_max(x):
    return jnp.max(functools.reduce(jnp.maximum, _lane_tiles(x)), axis=1, keepdims=True)


def _row_sum(x):
    return jnp.sum(functools.reduce(jnp.add, _lane_tiles(x)), axis=1, keepdims=True)


def _mm_kernel(a_ref, w_ref, o_ref):
    o_ref[...] = jnp.dot(a_ref[...], w_ref[...], preferred_element_type=F32).astype(o_ref.dtype)


def _matmul(a, w, out_dtype, tn):
    m, k = a.shape
    n = w.shape[1]
    tm = _pick(m, (512, 256, 128))
    return pl.pallas_call(
        _mm_kernel,
        grid=(n // tn, m // tm),
        in_specs=[pl.BlockSpec((tm, k), lambda j, i: (i, 0)),
                  pl.BlockSpec((k, tn), lambda j, i: (0, j))],
        out_specs=pl.BlockSpec((tm, tn), lambda j, i: (i, j)),
        out_shape=jax.ShapeDtypeStruct((m, n), out_dtype),
        compiler_params=_cparams(2),
    )(a, w)


def _mm_cast_kernel(a_ref, w_ref, o_ref, a16_ref):
    a16 = a_ref[...].astype(BF16)
    a16_ref[...] = a16
    o_ref[...] = jnp.dot(a16, w_ref[...], preferred_element_type=F32).astype(o_ref.dtype)


def _matmul_cast(a, w, out_dtype):
    m, k = a.shape
    n = w.shape[1]
    tm = _pick(m, (512, 256, 128))
    return pl.pallas_call(
        _mm_cast_kernel,
        grid=(m // tm,),
        in_specs=[pl.BlockSpec((tm, k), lambda i: (i, 0)),
                  pl.BlockSpec((k, n), lambda i: (0, 0))],
        out_specs=[pl.BlockSpec((tm, n), lambda i: (i, 0)),
                   pl.BlockSpec((tm, k), lambda i: (i, 0))],
        out_shape=[jax.ShapeDtypeStruct((m, n), out_dtype), jax.ShapeDtypeStruct((m, k), BF16)],
        compiler_params=_cparams(1),
    )(a, w)


def _dsa_kernel(q_ref, iq_ref, w_ref, kidx_ref, k_ref, v_ref, o_ref,
                sc_ref, thr_ref, jcut_ref, s_ref, m_ref, l_ref, acc_ref, *, tq, kc, n_sel, seq):
    i = pl.program_id(1)
    nch = (i * tq + tq + kc - 1) // kc
    row = lax.broadcasted_iota(jnp.int32, (tq, kc), 0)
    lane = lax.broadcasted_iota(jnp.int32, (tq, kc), 1)
    qpos = i * tq + row

    iq_all = iq_ref[...]
    iq = jnp.concatenate([iq_all[:, h * IDX_DIM:(h + 1) * IDX_DIM] for h in range(IDX_HEADS)],
                         axis=0)
    w = w_ref[:, IDX_DIM:IDX_DIM + IDX_HEADS] * (IDX_DIM ** -0.5 * IDX_HEADS ** -0.5)

    def score_body(c, carry):
        off = pl.multiple_of(c * kc, kc)
        kx = kidx_ref[pl.ds(off, kc), 0:IDX_DIM].astype(BF16)
        logits = lax.dot_general(iq, kx, (((1,), (1,)), ((), ())), preferred_element_type=F32)
        sc = jnp.zeros((tq, kc), F32)
        for h in range(IDX_HEADS):
            sc = sc + jnp.maximum(logits[h * tq:(h + 1) * tq], 0.0) * w[:, h:h + 1]
        sc_ref[c] = jnp.where(off + lane <= qpos, sc, -jnp.inf)
        return carry

    lax.fori_loop(0, nch, score_body, 0)

    def key_to_f32(key):
        return pltpu.bitcast(jnp.where(key < 0, key ^ jnp.int32(0x7FFFFFFF), key), F32)

    key_lowest = INT_MIN + 0x00800000
    thr_ref[...] = jnp.full((tq, 1), -FLT_MAX, F32)
    jcut_ref[...] = jnp.full((tq, 1), seq, jnp.int32)

    def count(pred_fn):
        def body(c, acc):
            sch = sc_ref[c]
            for j in range(kc // LANES):
                acc = acc + pred_fn(sch[:, j * LANES:(j + 1) * LANES], c * kc + j * LANES)
            return acc
        acc = lax.fori_loop(0, nch, body, jnp.zeros((tq, LANES), jnp.int32))
        return jnp.sum(acc.astype(F32), axis=1, keepdims=True).astype(jnp.int32)

    lane1 = lax.broadcasted_iota(jnp.int32, (tq, LANES), 1)

    @pl.when(i * tq + tq > n_sel)
    def _():
        def bit_body(b, t):
            cand = t + lax.shift_left(jnp.int32(1), 31 - b)
            cand_b = jnp.broadcast_to(key_to_f32(cand), (tq, LANES))
            cnt = count(lambda sch, _: jnp.where(sch >= cand_b, 1, 0))
            return jnp.where(cnt >= n_sel, cand, t)

        t = lax.fori_loop(0, 32, bit_body, jnp.full((tq, 1), INT_MIN, jnp.int32))
        t_f = key_to_f32(jnp.maximum(t, key_lowest))
        thr_ref[...] = t_f
        t_b = jnp.broadcast_to(t_f, (tq, LANES))
        c_ge = count(lambda sch, _: jnp.where(sch >= t_b, 1, 0))
        c_gt = count(lambda sch, _: jnp.where(sch > t_b, 1, 0))
        excess = c_ge > n_sel
        need = n_sel - c_gt

        @pl.when(jnp.max(jnp.where(excess, 1.0, 0.0)) > 0.0)
        def _():
            nbits = max(1, int(np.ceil(np.log2(seq))))

            def jbit(b, jc):
                cand = jc + lax.shift_left(jnp.int32(1), nbits - 1 - b)
                cand_b = jnp.broadcast_to(cand, (tq, LANES))
                cnt = count(lambda sch, base: jnp.where(
                    sch == t_b, jnp.where(base + lane1 < cand_b, 1, 0), 0))
                return jnp.where(cnt < need, cand, jc)

            jc = lax.fori_loop(0, nbits, jbit, jnp.zeros((tq, 1), jnp.int32))
            jcut_ref[...] = jnp.where(excess, jc, seq)

    qs = (q_ref[...] * (DSA_HEAD_DIM ** -0.5 * LOG2E)).astype(BF16)
    qg = [jnp.concatenate([qs[:, (g * DSA_REP + r) * DSA_HEAD_DIM:(g * DSA_REP + r + 1) * DSA_HEAD_DIM]
                           for r in range(DSA_REP)], axis=0) for g in range(DSA_KV_HEADS)]
    m_ref[...] = jnp.full(m_ref.shape, NEG, F32)
    l_ref[...] = jnp.zeros(l_ref.shape, F32)
    acc_ref[...] = jnp.zeros(acc_ref.shape, F32)
    t = thr_ref[...]
    jc = jcut_ref[...]

    def scores(c, slot):
        off = pl.multiple_of(c * kc, kc)
        for g in range(DSA_KV_HEADS):
            s_ref[slot, g] = lax.dot_general(
                qg[g], k_ref[pl.ds(off, kc), g * DSA_HEAD_DIM:(g + 1) * DSA_HEAD_DIM],
                (((1,), (1,)), ((), ())), preferred_element_type=F32)

    def update(c, slot):
        off = pl.multiple_of(c * kc, kc)
        sc = sc_ref[c]
        kpos = off + lane
        sel = jnp.where(kpos <= jc, jnp.where(sc >= t, 1, 0), jnp.where(sc > t, 1, 0)) > 0
        md = jnp.where(sel, (qpos - kpos).astype(F32), -NEG)
        for g in range(DSA_KV_HEADS):
            s = s_ref[slot, g]
            vg = v_ref[pl.ds(off, kc), g * DSA_HEAD_DIM:(g + 1) * DSA_HEAD_DIM]
            e = jnp.concatenate(
                [s[r * tq:(r + 1) * tq] - md * (LOG2E * 2.0 ** (-8.0 * (g * DSA_REP + r + 1) / N_DSA_HEADS))
                 for r in range(DSA_REP)], axis=0)
            m_old = m_ref[g]
            m_new = jnp.maximum(m_old, _row_max(e))
            p = jnp.exp2(e - m_new)
            alpha = jnp.exp2(m_old - m_new)
            l_ref[g] = alpha * l_ref[g] + _row_sum(p)
            acc_ref[g] = alpha * acc_ref[g] + jnp.dot(p.astype(BF16), vg, preferred_element_type=F32)
            m_ref[g] = m_new

    scores(0, 0)
    n_pairs = (nch - 1) // 2

    def att_body(p, carry):
        c = 2 * p
        scores(c + 1, 1)
        update(c, 0)
        scores(c + 2, 0)
        update(c + 1, 1)
        return carry

    lax.fori_loop(0, n_pairs, att_body, 0)
    c_tail = 2 * n_pairs

    @pl.when(nch - c_tail == 2)
    def _():
        scores(c_tail + 1, 1)
        update(c_tail, 0)
        update(c_tail + 1, 1)

    @pl.when(nch - c_tail == 1)
    def _():
        update(c_tail, 0)

    for g in range(DSA_KV_HEADS):
        o = acc_ref[g] / l_ref[g]
        for r in range(DSA_REP):
            hh = g * DSA_REP + r
            o_ref[:, hh * DSA_HEAD_DIM:(hh + 1) * DSA_HEAD_DIM] = o[r * tq:(r + 1) * tq].astype(o_ref.dtype)


def _dsa_attention(zs, zb, batch, seq, off_i, off_k, off_v):
    tq = min(DSA_TQ, seq)
    kc = min(DSA_KC, seq)
    nq = seq // tq
    n_sel = min(TOPK_MAX, seq // 4)
    kern = functools.partial(_dsa_kernel, tq=tq, kc=kc, n_sel=n_sel, seq=seq)
    return pl.pallas_call(
        kern,
        name="dsa",
        grid=(batch, nq),
        in_specs=[
            pl.BlockSpec((tq, DSA_Q_DIM), lambda b, i: (b * nq + i, 0)),
            pl.BlockSpec((tq, IDX_Q_DIM), lambda b, i: (b * nq + i, 0)),
            pl.BlockSpec((tq, LANES), lambda b, i: (b * nq + i, off_i // LANES)),
            pl.BlockSpec((seq, LANES), lambda b, i: (b, off_i // LANES)),
            pl.BlockSpec((seq, DSA_KV_DIM), lambda b, i: (b, off_k // DSA_KV_DIM)),
            pl.BlockSpec((seq, DSA_KV_DIM), lambda b, i: (b, off_v // DSA_KV_DIM)),
        ],
        out_specs=pl.BlockSpec((tq, DSA_Q_DIM), lambda b, i: (b * nq + i, 0)),
        out_shape=jax.ShapeDtypeStruct((batch * seq, DSA_Q_DIM), BF16),
        scratch_shapes=[
            pltpu.VMEM((seq // kc, tq, kc), F32),
            pltpu.VMEM((tq, 1), F32),
            pltpu.VMEM((tq, 1), jnp.int32),
            pltpu.VMEM((2, DSA_KV_HEADS, DSA_REP * tq, kc), F32),
            pltpu.VMEM((DSA_KV_HEADS, DSA_REP * tq, 1), F32),
            pltpu.VMEM((DSA_KV_HEADS, DSA_REP * tq, 1), F32),
            pltpu.VMEM((DSA_KV_HEADS, DSA_REP * tq, DSA_HEAD_DIM), F32),
        ],
        compiler_params=_cparams(2),
    )(zs, zb, zs, zs, zb, zb)


def _mla_q_kernel(qn_ref, qr_ref, qs_ref, cos_ref, sin_ref, o_ref):
    c = MLA_QK_DIM ** -0.5 * LOG2E
    pe = ((qr_ref[...] * cos_ref[...] + qs_ref[...] * sin_ref[...]) * c).astype(o_ref.dtype)
    qn = (qn_ref[...] * c).astype(o_ref.dtype)
    for h in range(MLA_HEADS):
        o_ref[0, h, :, 0:MLA_NOPE] = qn[:, h * MLA_NOPE:(h + 1) * MLA_NOPE]
        o_ref[0, h, :, MLA_NOPE:MLA_QK_DIM] = pe[:, h * MLA_ROPE:(h + 1) * MLA_ROPE]


def _mla_q_prep(zs, cos_q, sin_q, batch, seq, off_qn, off_qr, off_qs):
    tm = _pick(seq, (512, 256, 128))
    ns = seq // tm
    wr = MLA_HEADS * MLA_ROPE
    return pl.pallas_call(
        _mla_q_kernel,
        name="mla_q",
        grid=(batch * ns,),
        in_specs=[
            pl.BlockSpec((tm, MLA_HEADS * MLA_NOPE), lambda i: (i, off_qn // (MLA_HEADS * MLA_NOPE))),
            pl.BlockSpec((tm, wr), lambda i: (i, off_qr // wr)),
            pl.BlockSpec((tm, wr), lambda i: (i, off_qs // wr)),
            pl.BlockSpec((tm, wr), lambda i: (i % ns, 0)),
            pl.BlockSpec((tm, wr), lambda i: (i % ns, 0)),
        ],
        out_specs=pl.BlockSpec((1, MLA_HEADS, tm, MLA_QK_DIM), lambda i: (i // ns, 0, i % ns, 0)),
        out_shape=jax.ShapeDtypeStruct((batch, MLA_HEADS, seq, MLA_QK_DIM), BF16),
        compiler_params=_cparams(1),
    )(zs, zs, zs, cos_q, sin_q)


def _mla_kv_kernel(c_ref, kr_ref, g_ref, w_ref, cos_ref, sin_ref, k_ref, v_ref):
    c = c_ref[...]
    cn = c * lax.rsqrt(jnp.mean(c * c, axis=-1, keepdims=True) + RMS_EPS) * g_ref[...]
    kv = jnp.dot(cn.astype(BF16), w_ref[...], preferred_element_type=F32)
    kr = kr_ref[...]
    pe = (kr[:, 0:MLA_ROPE] * cos_ref[...] + kr[:, MLA_ROPE:2 * MLA_ROPE] * sin_ref[...]).astype(k_ref.dtype)
    for h in range(MLA_HEADS):
        k_ref[0, h, :, 0:MLA_NOPE] = kv[:, h * MLA_NOPE:(h + 1) * MLA_NOPE].astype(k_ref.dtype)
        k_ref[0, h, :, MLA_NOPE:MLA_QK_DIM] = pe
        v0 = MLA_HEADS * MLA_NOPE + h * MLA_V
        v_ref[0, h] = kv[:, v0:v0 + MLA_V].astype(v_ref.dtype)


def _mla_kv_prep(zs, g_ckv, w_kv, cos_k, sin_k, batch, seq, off_c, off_kr):
    tm = _pick(seq, (512, 256, 128))
    ns = seq // tm
    nkv = MLA_HEADS * (MLA_NOPE + MLA_V)
    return pl.pallas_call(
        _mla_kv_kernel,
        name="mla_kv",
        grid=(batch * ns,),
        in_specs=[
            pl.BlockSpec((tm, KV_RANK), lambda i: (i, off_c // KV_RANK)),
            pl.BlockSpec((tm, 2 * MLA_ROPE), lambda i: (i, off_kr // (2 * MLA_ROPE))),
            pl.BlockSpec((1, KV_RANK), lambda i: (0, 0)),
            pl.BlockSpec((KV_RANK, nkv), lambda i: (0, 0)),
            pl.BlockSpec((tm, MLA_ROPE), lambda i: (i % ns, 0)),
            pl.BlockSpec((tm, MLA_ROPE), lambda i: (i % ns, 0)),
        ],
        out_specs=[
            pl.BlockSpec((1, MLA_HEADS, tm, MLA_QK_DIM), lambda i: (i // ns, 0, i % ns, 0)),
            pl.BlockSpec((1, MLA_HEADS, tm, MLA_V), lambda i: (i // ns, 0, i % ns, 0)),
        ],
        out_shape=[jax.ShapeDtypeStruct((batch, MLA_HEADS, seq, MLA_QK_DIM), BF16),
                   jax.ShapeDtypeStruct((batch, MLA_HEADS, seq, MLA_V), BF16)],
        compiler_params=_cparams(1),
    )(zs, zs, g_ckv, w_kv, cos_k, sin_k)


def _mla_flash_kernel(q_ref, k_ref, v_ref, o_ref, s_ref, m_ref, l_ref, acc_ref, *, t):
    qi = pl.program_id(2)
    q = q_ref[0, 0]
    m_ref[...] = jnp.full(m_ref.shape, NEG, F32)
    l_ref[...] = jnp.zeros(l_ref.shape, F32)
    acc_ref[...] = jnp.zeros(acc_ref.shape, F32)

    def scores(j, slot):
        off = pl.multiple_of(j * t, t)
        s_ref[slot] = lax.dot_general(q, k_ref[0, 0, pl.ds(off, t), :], (((1,), (1,)), ((), ())),
                                      preferred_element_type=F32)

    def update(j, slot, diagonal):
        off = pl.multiple_of(j * t, t)
        s = s_ref[slot]
        if diagonal:
            row = lax.broadcasted_iota(jnp.int32, (t, t), 0)
            col = lax.broadcasted_iota(jnp.int32, (t, t), 1)
            s = jnp.where(col <= row, s, NEG)
        m_old = m_ref[...]
        m_new = jnp.maximum(m_old, _row_max(s))
        p = jnp.exp2(s - m_new)
        alpha = jnp.exp2(m_old - m_new)
        l_ref[...] = alpha * l_ref[...] + _row_sum(p)
        acc_ref[...] = alpha * acc_ref[...] + jnp.dot(p.astype(BF16), v_ref[0, 0, pl.ds(off, t), :],
                                                      preferred_element_type=F32)
        m_ref[...] = m_new

    scores(0, 0)
    n_pairs = qi // 2

    def body(p, carry):
        j = 2 * p
        scores(j + 1, 1)
        update(j, 0, False)
        scores(j + 2, 0)
        update(j + 1, 1, False)
        return carry

    lax.fori_loop(0, n_pairs, body, 0)
    j_tail = 2 * n_pairs

    @pl.when(qi - j_tail == 1)
    def _():
        scores(j_tail + 1, 1)
        update(j_tail, 0, False)
        update(j_tail + 1, 1, True)

    @pl.when(qi == j_tail)
    def _():
        update(j_tail, 0, True)

    o_ref[0] = (acc_ref[...] / l_ref[...]).astype(o_ref.dtype)


def _mla_flash(q_cat, k_cat, v, batch, seq):
    t = min(MLA_T, seq)
    n = seq // t
    kern = functools.partial(_mla_flash_kernel, t=t)
    return pl.pallas_call(
        kern,
        name="mla_flash",
        grid=(batch, MLA_HEADS, n),
        in_specs=[
            pl.BlockSpec((1, 1, t, MLA_QK_DIM), lambda b, h, qi: (b, h, qi, 0)),
            pl.BlockSpec((1, 1, seq, MLA_QK_DIM), lambda b, h, qi: (b, h, 0, 0)),
            pl.BlockSpec((1, 1, seq, MLA_V), lambda b, h, qi: (b, h, 0, 0)),
        ],
        out_specs=pl.BlockSpec((1, t, MLA_V), lambda b, h, qi: (b, qi, h)),
        out_shape=jax.ShapeDtypeStruct((batch, seq, MLA_OUT_DIM), BF16),
        scratch_shapes=[pltpu.VMEM((2, t, t), F32), pltpu.VMEM((t, 1), F32), pltpu.VMEM((t, 1), F32),
                        pltpu.VMEM((t, MLA_V), F32)],
        compiler_params=_cparams(3),
    )(q_cat, k_cat, v)


def _merge_kernel(oa_ref, ob_ref, wa_ref, wb_ref, ga_ref, gb_ref, o_ref):
    pa = jnp.dot(oa_ref[...], wa_ref[...], preferred_element_type=F32)
    pb = jnp.dot(ob_ref[...], wb_ref[...], preferred_element_type=F32)
    o_ref[...] = (jax.nn.sigmoid(ga_ref[...]) * pa + jax.nn.sigmoid(gb_ref[...]) * pb).astype(o_ref.dtype)


def _gated_merge(o_a, o_b, w_pa, w_pb, gates):
    m = o_a.shape[0]
    d = w_pa.shape[1]
    tm = _pick(m, (512, 256, 128))
    tn = _pick(d, (1024, 512, 256, 128))
    nn = d // tn
    return pl.pallas_call(
        _merge_kernel,
        name="gated_merge",
        grid=(nn, m // tm),
        in_specs=[
            pl.BlockSpec((tm, o_a.shape[1]), lambda j, i: (i, 0)),
            pl.BlockSpec((tm, o_b.shape[1]), lambda j, i: (i, 0)),
            pl.BlockSpec((w_pa.shape[0], tn), lambda j, i: (0, j)),
            pl.BlockSpec((w_pb.shape[0], tn), lambda j, i: (0, j)),
            pl.BlockSpec((tm, tn), lambda j, i: (i, j)),
            pl.BlockSpec((tm, tn), lambda j, i: (i, j + nn)),
        ],
        out_specs=pl.BlockSpec((tm, tn), lambda j, i: (i, j)),
        out_shape=jax.ShapeDtypeStruct((m, d), BF16),
        compiler_params=_cparams(2),
    )(o_a, o_b, w_pa, w_pb, gates, gates)


def _layer_norm(y, g, b):
    mu = jnp.mean(y, axis=-1, keepdims=True)
    dlt = y - mu
    var = jnp.mean(dlt * dlt, axis=-1, keepdims=True)
    return dlt * lax.rsqrt(var + LN_EPS) * g + b


def _proj_ln_kernel(a_ref, w_ref, x_ref, g_ref, b_ref, o_ref, slab_ref, *, alpha):
    y = alpha * x_ref[...] + jnp.dot(a_ref[...], w_ref[...], preferred_element_type=F32)
    h = _layer_norm(y, g_ref[...], b_ref[...])
    o_ref[...] = h
    tm, d = h.shape
    spt = d // LANES
    for s in range(spt):
        slab_ref[pl.ds(s, tm, stride=spt), :] = h[:, s * LANES:(s + 1) * LANES]


def _proj_ln(a, w, x, g, b, alpha):
    m, d = x.shape
    tm = _pick(m, (512, 256, 128))
    spt = d // LANES
    return pl.pallas_call(
        functools.partial(_proj_ln_kernel, alpha=alpha),
        name="proj_ln",
        grid=(m // tm,),
        in_specs=[
            pl.BlockSpec((tm, a.shape[1]), lambda i: (i, 0)),
            pl.BlockSpec(w.shape, lambda i: (0, 0)),
            pl.BlockSpec((tm, d), lambda i: (i, 0)),
            pl.BlockSpec((1, d), lambda i: (0, 0)),
            pl.BlockSpec((1, d), lambda i: (0, 0)),
        ],
        out_specs=[pl.BlockSpec((tm, d), lambda i: (i, 0)),
                   pl.BlockSpec((tm * spt, LANES), lambda i: (i, 0))],
        out_shape=[jax.ShapeDtypeStruct((m, d), F32),
                   jax.ShapeDtypeStruct((m * spt, LANES), F32)],
        compiler_params=_cparams(1),
    )(a, w, x, g, b)


def _router_kernel(h_ref, whi_ref, wlo_ref, b_ref, ids_ref, wts_ref):
    h = h_ref[...]
    h_hi = h.astype(BF16)
    h_lo = (h - h_hi.astype(F32)).astype(BF16)
    logits = (jnp.dot(h_hi, whi_ref[...], preferred_element_type=F32)
              + jnp.dot(h_hi, wlo_ref[...], preferred_element_type=F32)
              + jnp.dot(h_lo, whi_ref[...], preferred_element_type=F32)) + b_ref[...]
    tm = h.shape[0]
    lane = lax.broadcasted_iota(jnp.int32, (tm, LANES), 1)
    lane_f = lane.astype(F32)
    big = float(4 * LANES)
    ninf = -jnp.inf
    gl = jnp.where(lane < N_GROUPS, logits, ninf)
    ge = jnp.exp(gl - jnp.max(gl, axis=1, keepdims=True))
    gprob = ge / jnp.sum(ge, axis=1, keepdims=True)
    gw = jnp.max(gprob, axis=1, keepdims=True)
    gidx = jnp.min(jnp.where(gprob == gw, lane_f, big), axis=1, keepdims=True).astype(jnp.int32)
    in_group = lax.shift_right_arithmetic(lane - N_GROUPS, 3) == gidx
    el = jnp.where(in_group, logits, ninf)
    ee = jnp.exp(el - jnp.max(el, axis=1, keepdims=True))
    ep = jnp.where(in_group, ee / jnp.sum(ee, axis=1, keepdims=True), -1.0)
    p1 = jnp.max(ep, axis=1, keepdims=True)
    i1 = jnp.min(jnp.where(ep == p1, lane_f, big), axis=1, keepdims=True).astype(jnp.int32)
    ep2 = jnp.where(lane == i1, -1.0, ep)
    p2 = jnp.max(ep2, axis=1, keepdims=True)
    i2 = jnp.min(jnp.where(ep2 == p2, lane_f, big), axis=1, keepdims=True).astype(jnp.int32)
    den = p1 + p2
    w1 = gw * (p1 / den)
    w2 = gw * (p2 / den)
    ids_ref[...] = jnp.where(lane == 0, i1 - N_GROUPS, jnp.where(lane == 1, i2 - N_GROUPS, 0))
    wts_ref[...] = jnp.where(lane == 0, w1, jnp.where(lane == 1, w2, 0.0))


def _router(h, w_hi, w_lo, b_r):
    m, d = h.shape
    tm = _pick(m, (512, 256, 128))
    return pl.pallas_call(
        _router_kernel,
        name="router",
        grid=(m // tm,),
        in_specs=[
            pl.BlockSpec((tm, d), lambda i: (i, 0)),
            pl.BlockSpec((d, LANES), lambda i: (0, 0)),
            pl.BlockSpec((d, LANES), lambda i: (0, 0)),
            pl.BlockSpec((1, LANES), lambda i: (0, 0)),
        ],
        out_specs=[pl.BlockSpec((tm, LANES), lambda i: (i, 0)),
                   pl.BlockSpec((tm, LANES), lambda i: (i, 0))],
        out_shape=[jax.ShapeDtypeStruct((m, LANES), jnp.int32),
                   jax.ShapeDtypeStruct((m, LANES), F32)],
        compiler_params=_cparams(1),
    )(h, w_hi, w_lo, b_r)


def _row_copy(src_hbm, src_row, dst_vmem, dst_row, sem):
    return pltpu.make_async_copy(src_hbm.at[pl.ds(src_row, 1)], dst_vmem.at[pl.ds(dst_row, 1)], sem)


def _expert_kernel(ie_ref, row0_ref, nsub_ref, skip_ref, nrows_ref, tok_ref,
                   slab_hbm, wg_hbm, wu_hbm, wd_hbm, y_hbm,
                   slab_buf, x_ref, g_ref, u_ref, a_ref, o_buf, zbuf, wgu_buf, wd_buf, pend_ref,
                   gsem, osem, zsem, wsem, *, d, nck):
    it = pl.program_id(0)
    j = pl.program_id(1)
    n_items = pl.num_programs(0)
    nj = pl.num_programs(1)
    nsub = nsub_ref[it]
    spt = d // LANES
    ck = d // nck
    sub_rows = MOE_BLK * spt
    n_fill = (y_hbm.shape[0] - nrows_ref[0]) // MOE_BLK
    step = it * nj + j
    wslot = step % W_SLOTS

    def weight_copies(item, jj, slot, fn):
        e = ie_ref[item]

        @pl.when(jj < nck)
        def _():
            k0 = pl.multiple_of(jj * ck, ck)
            fn(pltpu.make_async_copy(wg_hbm.at[e, pl.ds(k0, ck), :], wgu_buf.at[slot, 0], wsem.at[slot]))
            fn(pltpu.make_async_copy(wu_hbm.at[e, pl.ds(k0, ck), :], wgu_buf.at[slot, 1], wsem.at[slot]))

        @pl.when(jj >= nck)
        def _():
            c0 = pl.multiple_of((jj - nck) * ck, ck)
            fn(pltpu.make_async_copy(wd_hbm.at[e, :, pl.ds(c0, ck)], wd_buf.at[slot], wsem.at[slot]))

    def fetch_weights(s):
        item = s // nj

        @pl.when(item < n_items)
        def _():
            @pl.when(skip_ref[item] == 0)
            def _():
                weight_copies(item, s % nj, s % W_SLOTS, lambda cp: cp.start())

    def fill_copy(b):
        row = pl.multiple_of(nrows_ref[0] + b * MOE_BLK, MOE_BLK)
        return pltpu.make_async_copy(zbuf, y_hbm.at[pl.ds(row, MOE_BLK)], zsem)

    def issue_gather(item):
        for sb in range(ITEM_SUB):
            @pl.when(sb < nsub_ref[item])
            def _(sb=sb):
                base = row0_ref[item] + sb * MOE_BLK

                def start(r, c):
                    src = pl.multiple_of(tok_ref[base + r] * spt, spt)
                    dst = pl.multiple_of((sb * MOE_BLK + r) * spt, spt)
                    pltpu.make_async_copy(slab_hbm.at[pl.ds(src, spt)], slab_buf.at[pl.ds(dst, spt)],
                                          gsem.at[sb]).start()
                    return c
                lax.fori_loop(0, MOE_BLK, start, 0, unroll=8)

    def unpack(item):
        for sb in range(ITEM_SUB):
            @pl.when(sb < nsub_ref[item])
            def _(sb=sb):
                pltpu.make_async_copy(slab_hbm.at[pl.ds(0, sub_rows)],
                                      slab_buf.at[pl.ds(sb * sub_rows, sub_rows)], gsem.at[sb]).wait()
                for s in range(spt):
                    col = s * LANES
                    x_ref[col // ck, sb * MOE_BLK:(sb + 1) * MOE_BLK, col % ck:col % ck + LANES] = (
                        slab_buf[pl.ds(sb * sub_rows + s, MOE_BLK, stride=spt), :].astype(BF16))

    @pl.when((it == 0) & (j == 0))
    def _():
        pend_ref[0] = 0
        pend_ref[1] = 0
        fetch_weights(0)
        fetch_weights(1)
        issue_gather(0)
        zbuf[...] = jnp.zeros(zbuf.shape, F32)
        lax.fori_loop(0, n_fill, lambda b, c: (fill_copy(b).start(), c)[1], 0)
        unpack(0)

        @pl.when(1 < pl.num_programs(0))
        def _():
            issue_gather(1)

    fetch_weights(step + 2)

    @pl.when(nsub > 0)
    def _():
        weight_copies(it, j, wslot, lambda cp: cp.wait())

    for n in range(1, ITEM_SUB + 1):
        rows = n * MOE_BLK

        @pl.when((nsub == n) & (j == 0))
        def _(rows=rows):
            xk = x_ref[0, 0:rows, :]
            g_ref[0:rows] = jnp.dot(xk, wgu_buf[wslot, 0].astype(BF16), preferred_element_type=F32)
            u_ref[0:rows] = jnp.dot(xk, wgu_buf[wslot, 1].astype(BF16), preferred_element_type=F32)

        @pl.when((nsub == n) & (j > 0) & (j < nck))
        def _(rows=rows):
            xk = x_ref[j, 0:rows, :]
            g_ref[0:rows] += jnp.dot(xk, wgu_buf[wslot, 0].astype(BF16), preferred_element_type=F32)
            u_ref[0:rows] += jnp.dot(xk, wgu_buf[wslot, 1].astype(BF16), preferred_element_type=F32)

        @pl.when((nsub == n) & (j == nck))
        def _(rows=rows):
            g = g_ref[0:rows]
            a_ref[0:rows] = (g * jax.nn.sigmoid(g) * u_ref[0:rows]).astype(BF16)

    @pl.when(j >= nck)
    def _():
        slot = j % 2
        other = 1 - slot
        col0 = pl.multiple_of((j - nck) * ck, ck)
        for n in range(1, ITEM_SUB + 1):
            rows = n * MOE_BLK

            @pl.when(nsub == n)
            def _(rows=rows):
                o_buf[slot, 0:rows] = jnp.dot(a_ref[0:rows], wd_buf[wslot].astype(BF16),
                                              preferred_element_type=F32)
                pltpu.make_async_copy(o_buf.at[slot, pl.ds(0, rows)],
                                      y_hbm.at[pl.ds(pl.multiple_of(row0_ref[it], MOE_BLK), rows), pl.ds(col0, ck)],
                                      osem.at[slot]).start()

        def drain(s):
            for n in range(1, ITEM_SUB + 1):
                rows = n * MOE_BLK

                @pl.when(pend_ref[s] == n)
                def _(rows=rows):
                    pltpu.make_async_copy(o_buf.at[s, pl.ds(0, rows)],
                                          y_hbm.at[pl.ds(0, rows), pl.ds(0, ck)], osem.at[s]).wait()
            pend_ref[s] = 0

        drain(other)
        pend_ref[slot] = nsub

        @pl.when(j == pl.num_programs(1) - 1)
        def _():
            @pl.when(it + 1 < pl.num_programs(0))
            def _():
                unpack(it + 1)

            @pl.when(it + 2 < pl.num_programs(0))
            def _():
                issue_gather(it + 2)

        @pl.when((it == pl.num_programs(0) - 1) & (j == pl.num_programs(1) - 1))
        def _():
            drain(slot)
            lax.fori_loop(0, n_fill, lambda b, c: (fill_copy(b).wait(), c)[1], 0)


def _experts(tables, buf_tok, slabs, wg, wu, wd, cap):
    item_expert, item_row0, item_nsub, item_skip, n_rows_used = tables
    n_items = item_expert.shape[0]
    n_exp, d, ff = wg.shape
    nck = MOE_CHUNKS
    ck = d // nck
    rows = ITEM_SUB * MOE_BLK
    spt = d // LANES

    return pl.pallas_call(
        functools.partial(_expert_kernel, d=d, nck=nck),
        name="moe_experts",
        grid_spec=pltpu.PrefetchScalarGridSpec(
            num_scalar_prefetch=6,
            grid=(n_items, 2 * nck),
            in_specs=[pl.BlockSpec(memory_space=pl.ANY)] * 4,
            out_specs=pl.BlockSpec(memory_space=pl.ANY),
            scratch_shapes=[
                pltpu.VMEM((rows * spt, LANES), F32),
                pltpu.VMEM((nck, rows, ck), BF16),
                pltpu.VMEM((rows, ff), F32),
                pltpu.VMEM((rows, ff), F32),
                pltpu.VMEM((rows, ff), BF16),
                pltpu.VMEM((2, rows, ck), F32),
                pltpu.VMEM((MOE_BLK, d), F32),
                pltpu.VMEM((W_SLOTS, 2, ck, ff), F32),
                pltpu.VMEM((W_SLOTS, ff, ck), F32),
                pltpu.SMEM((2,), jnp.int32),
                pltpu.SemaphoreType.DMA((ITEM_SUB,)),
                pltpu.SemaphoreType.DMA((2,)),
                pltpu.SemaphoreType.DMA(()),
                pltpu.SemaphoreType.DMA((W_SLOTS,)),
            ],
        ),
        out_shape=jax.ShapeDtypeStruct((cap, d), F32),
        compiler_params=_cparams(2, vmem_mb=EXPERT_VMEM_MB),
    )(item_expert, item_row0, item_nsub, item_skip, n_rows_used, buf_tok, slabs, wg, wu, wd)


def _combine_kernel(dest_ref, y_hbm, h_ref, wts_ref, g_ref, b_ref, o_ref, buf, sem, *, tm, alpha):
    i = pl.program_id(0)
    slot = i % 2

    def issue(tile, dst_slot):
        def start(r, c):
            a = 2 * (tile * tm + r)
            _row_copy(y_hbm, dest_ref[a], buf.at[dst_slot, 0], r, sem.at[dst_slot]).start(priority=0)
            _row_copy(y_hbm, dest_ref[a + 1], buf.at[dst_slot, 1], r, sem.at[dst_slot]).start(priority=1)
            return c
        lax.fori_loop(0, tm, start, 0, unroll=8)

    @pl.when(i == 0)
    def _():
        issue(0, 0)

    @pl.when(i + 1 < pl.num_programs(0))
    def _():
        issue(i + 1, 1 - slot)

    for k in range(2):
        pltpu.make_async_copy(y_hbm.at[pl.ds(0, tm)], buf.at[slot, k], sem.at[slot]).wait()
    wts = wts_ref[...]
    f = buf[slot, 0] * wts[:, 0:1] + buf[slot, 1] * wts[:, 1:2]
    o_ref[...] = _layer_norm(alpha * h_ref[...] + f, g_ref[...], b_ref[...])


def _combine_ln(dest, yb, h, wts, g, b, alpha):
    m, d = h.shape
    tm = _pick(m, (256, 128))
    return pl.pallas_call(
        functools.partial(_combine_kernel, tm=tm, alpha=alpha),
        name="moe_combine",
        grid_spec=pltpu.PrefetchScalarGridSpec(
            num_scalar_prefetch=1,
            grid=(m // tm,),
            in_specs=[
                pl.BlockSpec(memory_space=pl.ANY),
                pl.BlockSpec((tm, d), lambda i, dst: (i, 0)),
                pl.BlockSpec((tm, LANES), lambda i, dst: (i, 0)),
                pl.BlockSpec((1, d), lambda i, dst: (0, 0)),
                pl.BlockSpec((1, d), lambda i, dst: (0, 0)),
            ],
            out_specs=pl.BlockSpec((tm, d), lambda i, dst: (i, 0)),
            scratch_shapes=[pltpu.VMEM((2, 2, tm, d), F32), pltpu.SemaphoreType.DMA((2,))],
        ),
        out_shape=jax.ShapeDtypeStruct((m, d), F32),
        compiler_params=_cparams(1),
    )(dest, yb, h, wts, g, b)


def _dispatch_tables(ids, n_tok):
    flat_e = ids[:, :2].reshape(-1)
    n_assign = flat_e.shape[0]
    onehot = (flat_e[:, None] == jnp.arange(N_EXPERTS, dtype=jnp.int32)[None, :]).astype(jnp.int32)
    csum = jnp.cumsum(onehot, axis=0)
    rank = jnp.sum(onehot * csum, axis=1) - 1
    counts = csum[-1]
    padded = (counts + MOE_BLK - 1) // MOE_BLK * MOE_BLK
    pend = jnp.cumsum(padded)
    pstart = pend - padded
    dest = (pstart[flat_e] + rank).astype(jnp.int32)
    cap = -(-n_assign // MOE_BLK) * MOE_BLK + N_EXPERTS * MOE_BLK
    buf_tok = jnp.zeros((cap,), jnp.int32).at[dest].set(jnp.arange(n_assign, dtype=jnp.int32) // 2)

    item_rows = ITEM_SUB * MOE_BLK
    items_e = (padded + item_rows - 1) // item_rows
    item_end = jnp.cumsum(items_e)
    item_start = item_end - items_e
    n_used = item_end[-1]
    n_items = -(-N_EXPERTS * (ITEM_SUB - 1) // ITEM_SUB) + cap // item_rows + 1
    it = jnp.arange(n_items, dtype=jnp.int32)
    e_it = jnp.minimum(jnp.sum((item_end[None, :] <= it[:, None]).astype(jnp.int32), axis=1), N_EXPERTS - 1)
    k_it = it - item_start[e_it]
    used = it < n_used
    e_last = e_it[jnp.maximum(n_used - 1, 0)]
    item_expert = jnp.where(used, e_it, e_last).astype(jnp.int32)
    item_row0 = jnp.where(used, pstart[e_it] + k_it * item_rows, 0).astype(jnp.int32)
    item_nsub = jnp.where(used, jnp.clip((padded[e_it] - k_it * item_rows) // MOE_BLK, 0, ITEM_SUB), 0)
    item_skip = jnp.where(used, 0, 1).astype(jnp.int32)
    n_rows_used = pend[-1].astype(jnp.int32).reshape(1)
    return dest, buf_tok, (item_expert, item_row0, item_nsub.astype(jnp.int32), item_skip, n_rows_used), cap


def _rope_tables(seq):
    half = MLA_ROPE // 2
    freqs = ROPE_THETA ** (-jnp.arange(half, dtype=F32) / half)
    ang = jnp.arange(seq, dtype=F32)[:, None] * freqs
    cos = jnp.cos(ang)
    sin = jnp.sin(ang)
    return jnp.concatenate([cos, cos], -1), jnp.concatenate([sin, sin], -1)


def _swap_halves_neg(w):
    half = w.shape[-1] // 2
    return jnp.concatenate([-w[..., half:], w[..., :half]], axis=-1)


def _layer(x, w_in, w_ukv, g_ckv, w_pa, w_pb, w_o, ln1_g, ln1_b, w_rg, b_rg, w_re, b_re,
           w_gate_e, w_up_e, w_down_e, ln2_g, ln2_b, alpha):
    batch, seq, d = x.shape
    n_tok = batch * seq
    x2 = x.reshape(n_tok, d)

    splits = (DSA_Q_DIM, DSA_KV_DIM, DSA_KV_DIM, IDX_Q_DIM, IDX_DIM, IDX_HEADS, MLA_Q_DIM, KV_RANK, MLA_ROPE, 2 * d)
    offs = np.cumsum((0,) + splits)
    w_dq, w_dk, w_dv, w_iq, w_ik, w_iw, w_mq, w_ckv, w_kr, w_gates = [
        w_in[:, offs[j]:offs[j + 1]] for j in range(len(splits))]
    w_mq3 = w_mq.reshape(d, MLA_HEADS, MLA_QK_DIM)
    w_qn = w_mq3[:, :, :MLA_NOPE].reshape(d, MLA_HEADS * MLA_NOPE)
    w_qr = w_mq3[:, :, MLA_NOPE:]
    w_qs = _swap_halves_neg(w_qr)
    wr = MLA_HEADS * MLA_ROPE
    wb = jnp.concatenate([w_iq, w_dk, w_dv], axis=1).astype(BF16)
    off_k = IDX_Q_DIM
    off_v = off_k + DSA_KV_DIM
    pad_i = jnp.zeros((d, LANES - IDX_DIM - IDX_HEADS), F32)
    ws = jnp.concatenate([w_dq, w_qn, w_qr.reshape(d, wr), w_qs.reshape(d, wr), w_ckv, w_ik, w_iw, pad_i,
                          w_kr, _swap_halves_neg(w_kr)], axis=1).astype(BF16)
    off_qn = DSA_Q_DIM
    off_qr = off_qn + MLA_HEADS * MLA_NOPE
    off_qs = off_qr + wr
    off_c = off_qs + wr
    off_i = off_c + KV_RANK
    off_kr = off_i + LANES

    zb, xb16 = _matmul_cast(x2, wb, BF16)
    zs = _matmul(xb16, ws, F32, tn=ws.shape[1] // 2)
    gates = _matmul(xb16, w_gates.astype(BF16), F32, tn=_pick(2 * d, (2048, 1024, 512, 256)))

    o_a = _dsa_attention(zs, zb, batch, seq, off_i, off_k, off_v)

    cos2, sin2 = _rope_tables(seq)
    w_ukv3 = w_ukv.reshape(KV_RANK, MLA_HEADS, MLA_NOPE + MLA_V)
    w_kv = jnp.concatenate([w_ukv3[:, :, :MLA_NOPE].reshape(KV_RANK, MLA_HEADS * MLA_NOPE),
                            w_ukv3[:, :, MLA_NOPE:].reshape(KV_RANK, MLA_HEADS * MLA_V)], axis=1).astype(BF16)
    q_cat = _mla_q_prep(zs, jnp.tile(cos2, (1, MLA_HEADS)), jnp.tile(sin2, (1, MLA_HEADS)),
                        batch, seq, off_qn, off_qr, off_qs)
    k_cat, v_mla = _mla_kv_prep(zs, g_ckv.reshape(1, KV_RANK), w_kv, cos2, sin2, batch, seq, off_c, off_kr)
    o_b = _mla_flash(q_cat, k_cat, v_mla, batch, seq).reshape(n_tok, MLA_OUT_DIM)

    merged = _gated_merge(o_a, o_b, w_pa.astype(BF16), w_pb.astype(BF16), gates)
    h, slabs = _proj_ln(merged, w_o.astype(BF16), x2, ln1_g.reshape(1, d), ln1_b.reshape(1, d), alpha)

    pad_r = jnp.zeros((d, LANES - N_GROUPS - N_EXPERTS), F32)
    w_r = jnp.concatenate([w_rg, w_re, pad_r], axis=1)
    w_r_hi = w_r.astype(BF16)
    w_r_lo = (w_r - w_r_hi.astype(F32)).astype(BF16)
    b_r = jnp.concatenate([b_rg, b_re, jnp.zeros((LANES - N_GROUPS - N_EXPERTS,), F32)]).reshape(1, LANES)
    ids, wts = _router(h, w_r_hi, w_r_lo, b_r)
    dest, buf_tok, item_tables, cap = _dispatch_tables(ids, n_tok)
    yb = _experts(item_tables, buf_tok, slabs, w_gate_e, w_up_e, w_down_e, cap)
    out = _combine_ln(dest, yb, h, wts, ln2_g.reshape(1, d), ln2_b.reshape(1, d), alpha)
    return out.reshape(batch, seq, d)


def kernel(x, w_in, w_ukv, g_ckv, w_pa, w_pb, w_o, ln1_g, ln1_b, w_rg, b_rg, w_re, b_re,
           w_gate_e, w_up_e, w_down_e, ln2_g, ln2_b):
    depth = w_in.shape[0]
    alpha = (2.0 * depth) ** 0.25
    for l in range(depth):
        x = _layer(x, w_in[l], w_ukv[l], g_ckv[l], w_pa[l], w_pb[l], w_o[l], ln1_g[l], ln1_b[l],
                   w_rg[l], b_rg[l], w_re[l], b_re[l], w_gate_e[l], w_up_e[l], w_down_e[l],
                   ln2_g[l], ln2_b[l], alpha)
    return x
```
